```python
import jax, jax.numpy as jnp
from jax import lax
import numpy as np


D_MODEL = 1024
BATCH = 8
SEQ = 2048
DEPTH = 1
DEC_BATCH = 128
DEC_SEQ = 1
PAST_LEN = 16384
PAGE_SIZE = 128

N_META = 16
D_POOL = D_MODEL // 2
N_POOL_GROUPS = 4
POOL_GROUP = D_POOL // N_POOL_GROUPS
POOL_WINDOWS = (2, 4, 8, 16)
POOL_BUF = max(POOL_WINDOWS) - 1
D_CONV = D_MODEL // 2
CONV_WIDTH = 3
CONV_BUF = CONV_WIDTH - 1
D_IN_PROJ = D_POOL + 3 * D_CONV + 2 * D_MODEL
N_EXPERTS = 32
TOP_K = 4
D_FF = D_MODEL
SWIGLU_LIMIT = 7.0
SWIGLU_ALPHA = 1.702
MOE_BLOCK = 128
RMS_EPS = 1e-5

kernel_name = 'hybrid_pool_shortconv_moe_step'


def rms_norm(x, g):
    xf = x.astype(jnp.float32)
    y = xf * lax.rsqrt(jnp.mean(xf * xf, axis=-1, keepdims=True) + RMS_EPS)
    return (y * g.astype(jnp.float32)).astype(x.dtype)


def multiscale_pool(u, buf, p0):
    L = u.shape[1]
    ext = jnp.concatenate([buf, u], axis=1).astype(jnp.float32)
    cs = jnp.cumsum(ext, axis=1)
    cs = jnp.concatenate([jnp.zeros_like(cs[:, :1]), cs], axis=1)
    pos = p0 + jnp.arange(L)
    outs = []
    for g, w in enumerate(POOL_WINDOWS):
        sl = slice(g * POOL_GROUP, (g + 1) * POOL_GROUP)
        s = cs[:, POOL_BUF + 1:POOL_BUF + 1 + L, sl] - cs[:, POOL_BUF + 1 - w:POOL_BUF + 1 - w + L, sl]
        cnt = jnp.minimum(w, pos + 1).astype(jnp.float32)
        outs.append(s / cnt[None, :, None])
    mean = jnp.concatenate(outs, axis=-1)
    return (mean - u.astype(jnp.float32)).astype(u.dtype)


def short_conv(v, buf, w_conv):
    L = v.shape[1]
    ext = jnp.concatenate([buf, v], axis=1)
    return sum(ext[:, k:k + L] * w_conv[k] for k in range(CONV_WIDTH))


def moe_ffn(x, w_router, b_router, w_gu, b_gu, w_down, b_down):
    b, L, D = x.shape
    T = b * L
    xt = x.reshape(T, D)
    logits = (xt @ w_router).astype(jnp.float32) + b_router.astype(jnp.float32)
    top_val, top_idx = lax.top_k(logits, TOP_K)
    gates = jax.nn.softmax(top_val, axis=-1).astype(x.dtype)
    A = T * TOP_K
    e_flat = top_idx.reshape(A).astype(jnp.int32)
    g_flat = gates.reshape(A)
    tok_flat = jnp.arange(A, dtype=jnp.int32) // TOP_K
    order = jnp.argsort(e_flat)
    e_sorted = e_flat[order]
    counts = jnp.zeros((N_EXPERTS,), jnp.int32).at[e_flat].add(1)
    padded = (counts + MOE_BLOCK - 1) // MOE_BLOCK * MOE_BLOCK
    start = jnp.cumsum(counts) - counts
    pend = jnp.cumsum(padded)
    pstart = pend - padded
    rank = jnp.arange(A, dtype=jnp.int32) - start[e_sorted]
    dest = pstart[e_sorted] + rank
    n_blocks = -(-A // MOE_BLOCK) + N_EXPERTS
    R = n_blocks * MOE_BLOCK
    row_tok = jnp.full((R,), T, jnp.int32).at[dest].set(tok_flat[order])
    row_gate = jnp.zeros((R,), x.dtype).at[dest].set(g_flat[order])
    block_start = jnp.arange(n_blocks, dtype=jnp.int32) * MOE_BLOCK
    block_exp = jnp.minimum(jnp.searchsorted(pend, block_start, side='right'), N_EXPERTS - 1).astype(jnp.int32)
    x_pad = jnp.concatenate([xt, jnp.zeros((1, D), xt.dtype)], axis=0)
    xb = x_pad[row_tok].reshape(n_blocks, MOE_BLOCK, D)

    def expert_block(args):
        xblk, e = args
        hgu = xblk @ w_gu[e] + b_gu[e]
        gate = jnp.minimum(hgu[:, :D_FF], SWIGLU_LIMIT)
        up = jnp.clip(hgu[:, D_FF:], -SWIGLU_LIMIT, SWIGLU_LIMIT)
        act = gate * jax.nn.sigmoid(SWIGLU_ALPHA * gate) * (up + 1)
        return act @ w_down[e] + b_down[e]

    yb = lax.map(expert_block, (xb, block_exp))
    y = jax.ops.segment_sum(yb.reshape(R, D) * row_gate[:, None], row_tok, num_segments=T + 1)[:T]
    return y.reshape(b, L, D).astype(x.dtype)


def hybrid_layer(x, buf_pool, buf_conv, p0, g_mix, w_in, w_pool_grp, pool_scale, w_pool_up, w_conv,
                 w_conv_out, w_o, g_ffn, w_router, b_router, w_gu, b_gu, w_down, b_down):
    b, L, _ = x.shape
    h = rms_norm(x, g_mix)
    z = h @ w_in
    o1 = D_POOL
    o2 = o1 + D_CONV
    o3 = o2 + D_CONV
    o4 = o3 + D_CONV
    u_pool, c_gate, b_gate, v, gate_logits = z[..., :o1], z[..., o1:o2], z[..., o2:o3], z[..., o3:o4], z[..., o4:]
    pm = multiscale_pool(u_pool, buf_pool, p0)
    pm = jnp.einsum('blgc,gcd->blgd', pm.reshape(b, L, N_POOL_GROUPS, POOL_GROUP), w_pool_grp)
    branch_a = (pm.reshape(b, L, D_POOL) * pool_scale) @ w_pool_up
    cv = c_gate * v
    branch_b = (b_gate * short_conv(cv, buf_conv, w_conv)) @ w_conv_out
    gates = jax.nn.sigmoid(gate_logits.astype(jnp.float32)).astype(x.dtype)
    g_a, g_b = gates[..., :D_MODEL], gates[..., D_MODEL:]
    x = x + (g_a * branch_a + g_b * branch_b) @ w_o
    x = x + moe_ffn(rms_norm(x, g_ffn), w_router, b_router, w_gu, b_gu, w_down, b_down)
    new_pool = jnp.concatenate([buf_pool, u_pool], axis=1)[:, -POOL_BUF:]
    new_conv = jnp.concatenate([buf_conv, cv], axis=1)[:, -CONV_BUF:]
    return x, new_pool, new_conv


def setup_inputs(seed: int = 0) -> dict:
    key = jax.random.key(seed)
    ks = jax.random.split(key, 24)
    n = jax.random.normal
    f32 = jnp.float32
    return {
        'x_prompt': n(ks[0], (BATCH, SEQ, D_MODEL), f32),
        'x_sample': n(ks[1], (DEC_BATCH, DEC_SEQ, D_MODEL), f32),
        'state_pool': n(ks[2], (DEPTH, DEC_BATCH, POOL_BUF, D_POOL), f32),
        'state_conv': n(ks[3], (DEPTH, DEC_BATCH, CONV_BUF, D_CONV), f32),
        'meta_tokens': n(ks[4], (N_META, D_MODEL), f32),
        'g_mix': 1.0 + 0.01 * n(ks[5], (DEPTH, D_MODEL), f32),
        'w_in': n(ks[6], (DEPTH, D_MODEL, D_IN_PROJ), f32) * D_MODEL ** -0.5,
        'w_pool_grp': n(ks[7], (DEPTH, N_POOL_GROUPS, POOL_GROUP, POOL_GROUP), f32) * POOL_GROUP ** -0.5,
        'pool_scale': 1.0 + 0.02 * n(ks[8], (DEPTH, D_POOL), f32),
        'w_pool_up': n(ks[9], (DEPTH, D_POOL, D_MODEL), f32) * D_POOL ** -0.5,
        'w_conv': n(ks[10], (DEPTH, CONV_WIDTH, D_CONV), f32) * CONV_WIDTH ** -0.5,
        'w_conv_out': n(ks[11], (DEPTH, D_CONV, D_MODEL), f32) * D_CONV ** -0.5,
        'w_o': n(ks[12], (DEPTH, D_MODEL, D_MODEL), f32) * D_MODEL ** -0.5,
        'g_ffn': 1.0 + 0.01 * n(ks[13], (DEPTH, D_MODEL), f32),
        'w_router': n(ks[14], (DEPTH, D_MODEL, N_EXPERTS), f32) * D_MODEL ** -0.5,
        'b_router': 0.01 * n(ks[15], (DEPTH, N_EXPERTS), f32),
        'w_gu': n(ks[16], (DEPTH, N_EXPERTS, D_MODEL, 2 * D_FF), f32) * D_MODEL ** -0.5,
        'b_gu': 0.01 * n(ks[17], (DEPTH, N_EXPERTS, 2 * D_FF), f32),
        'w_down': n(ks[18], (DEPTH, N_EXPERTS, D_FF, D_MODEL), f32) * D_FF ** -0.5,
        'b_down': 0.01 * n(ks[19], (DEPTH, N_EXPERTS, D_MODEL), f32),
        'g_final': 1.0 + 0.01 * n(ks[20], (D_MODEL,), f32),
    }


def reference(x_prompt, x_sample, state_pool, state_conv, meta_tokens, g_mix, w_in, w_pool_grp, pool_scale,
              w_pool_up, w_conv, w_conv_out, w_o, g_ffn, w_router, b_router, w_gu, b_gu, w_down, b_down, g_final):
    bp = x_prompt.shape[0]
    meta = jnp.broadcast_to(meta_tokens[None].astype(x_prompt.dtype), (bp, N_META, D_MODEL))
    hp = jnp.concatenate([meta, x_prompt], axis=1)
    hs = x_sample
    zero_pool = jnp.zeros((bp, POOL_BUF, D_POOL), hp.dtype)
    zero_conv = jnp.zeros((bp, CONV_BUF, D_CONV), hp.dtype)
    pool_p, conv_p, pool_s, conv_s = [], [], [], []
    for l in range(DEPTH):
        wl = (g_mix[l], w_in[l], w_pool_grp[l], pool_scale[l], w_pool_up[l], w_conv[l], w_conv_out[l], w_o[l],
              g_ffn[l], w_router[l], b_router[l], w_gu[l], b_gu[l], w_down[l], b_down[l])
        hp, npool, nconv = hybrid_layer(hp, zero_pool, zero_conv, 0, *wl)
        pool_p.append(npool)
        conv_p.append(nconv)
        hs, spool, sconv = hybrid_layer(hs, state_pool[l], state_conv[l], PAST_LEN, *wl)
        pool_s.append(spool)
        conv_s.append(sconv)
    y_prompt = rms_norm(hp[:, N_META:], g_final)
    y_sample = rms_norm(hs, g_final)
    return (y_prompt, y_sample, jnp.stack(pool_p), jnp.stack(conv_p), jnp.stack(pool_s), jnp.stack(conv_s))
```

```python
import jax
import jax.numpy as jnp
from jax import lax
from jax.experimental import pallas as pl
from jax.experimental.pallas import tpu as pltpu

D_MODEL = 1024
BATCH = 8
SEQ = 2048
DEC_BATCH = 128
N_META = 16
D_POOL = 512
POOL_GROUP = 128
POOL_WINDOWS = (2, 4, 8, 16)
POOL_BUF = 15
D_CONV = 512
CONV_BUF = 2
N_EXPERTS = 32
TOP_K = 4
D_FF = 1024
SWIGLU_LIMIT = 7.0
SWIGLU_ALPHA = 1.702
RMS_EPS = 1e-5

O_CG = D_POOL
O_BG = O_CG + D_CONV
O_V = O_BG + D_CONV
O_GL = O_V + D_CONV

T_PROMPT = BATCH * SEQ
T_ALL = T_PROMPT + DEC_BATCH
N_ASSIGN = TOP_K * T_ALL

MIX_TILE = 256
MIX_PROMPT_STEPS = T_PROMPT // MIX_TILE
SEQ_TILES = SEQ // MIX_TILE
POOL_HALO = 16
CONV_HALO = 8
MOE_BLK = 256
N_BLOCKS = -(-N_ASSIGN // MOE_BLK) + N_EXPERTS
COMB_TILE = 512
COMB_PROMPT_STEPS = T_PROMPT // COMB_TILE
T_PAD = (T_ALL + COMB_TILE - 1) // COMB_TILE * COMB_TILE
SLAB_GAP = T_PAD - T_ALL
OUT_ROWS = (N_BLOCKS + 2) * MOE_BLK
TAIL0 = TOP_K * T_PAD
assert OUT_ROWS >= TAIL0

VMEM_LIMIT = 56 * 1024 * 1024

_F32 = jnp.float32
_BF16 = jnp.bfloat16


def _rms(x, g):
    ms = jnp.mean(x * x, axis=-1, keepdims=True)
    return x * lax.rsqrt(ms + RMS_EPS) * g


def _bdot(a, w):
    return jnp.dot(a.astype(_BF16), w, preferred_element_type=_F32)


def _branches(z, pm, conv, wgrp_ref, pscale_ref, wup_ref, wcout_ref):
    pmb = pm.astype(_BF16)
    half = 2 * POOL_GROUP
    pg = jnp.concatenate(
        [jnp.dot(pmb[:, :half], wgrp_ref[0], preferred_element_type=_F32),
         jnp.dot(pmb[:, half:], wgrp_ref[1], preferred_element_type=_F32)], axis=-1)
    branch_a = _bdot(pg * pscale_ref[...], wup_ref[...])
    branch_b = _bdot(z[:, O_BG:O_V] * conv, wcout_ref[...])
    return branch_a, branch_b


def _merge_and_route(x, z, branch_a, branch_b, wo_ref, gffn_ref, wrt_ref, br_ref):
    gates = jax.nn.sigmoid(z[:, O_GL:])
    merged = gates[:, :D_MODEL] * branch_a + gates[:, D_MODEL:] * branch_b
    x1 = x + _bdot(merged, wo_ref[...])
    h2 = _rms(x1, gffn_ref[...])
    logits = lax.dot_general(wrt_ref[...], h2, (((1,), (1,)), ((), ())),
                             precision=lax.Precision.HIGHEST,
                             preferred_element_type=_F32) + br_ref[...]
    iota = lax.broadcasted_iota(jnp.int32, logits.shape, 0)
    vals, idxs = [], []
    cur = logits
    for _ in range(TOP_K):
        m = jnp.max(cur, axis=0, keepdims=True)
        ik = jnp.min(jnp.where(cur == m, iota, N_EXPERTS), axis=0, keepdims=True)
        vals.append(m)
        idxs.append(ik)
        cur = jnp.where(iota == ik, -jnp.inf, cur)
    exps = [jnp.exp(v - vals[0]) for v in vals]
    denom = (exps[0] + exps[1]) + (exps[2] + exps[3])
    gate = [e / denom for e in exps]
    return x1, h2, idxs, gate


def _mixer_kernel(xp_ref, xs_ref, spool_ref, sconv_ref, meta_ref, gmix_ref, win_ref, wgrp_ref, pscale_ref,
                  wup_ref, wconv_ref, wcout_ref, wo_ref, gffn_ref, wrt_ref, br_ref,
                  x1_ref, h2_ref, idx_ref, gate_ref, pool_out_ref, conv_out_ref, us_ref, cvs_ref,
                  pool_ext, conv_ext, meta_pool, meta_conv, pm_scr, conv_scr):
    s = pl.program_id(0)
    is_prompt = s < MIX_PROMPT_STEPS
    j = lax.rem(s, SEQ_TILES)

    @pl.when(s == 0)
    def _():
        hm = _rms(meta_ref[...], gmix_ref[...]).astype(_BF16)
        zm = jnp.dot(hm, win_ref[:, :O_GL], preferred_element_type=_F32)
        meta_pool[...] = zm[:, :D_POOL]
        meta_conv[...] = zm[:, O_CG:O_BG] * zm[:, O_V:O_GL]

    @pl.when(is_prompt & (j == 0))
    def _():
        pool_ext[0:POOL_HALO, :] = meta_pool[...]
        conv_ext[0:CONV_HALO, :] = meta_conv[N_META - CONV_HALO:, :]

    x = jnp.where(is_prompt, xp_ref[...], xs_ref[...])
    h = _rms(x, gmix_ref[...]).astype(_BF16)
    z = jnp.dot(h, win_ref[...], preferred_element_type=_F32)
    u = z[:, :D_POOL]
    cv = z[:, O_CG:O_BG] * z[:, O_V:O_GL]
    wc = wconv_ref[...]

    @pl.when(is_prompt)
    def _():
        pool_ext[POOL_HALO:, :] = u
        ext = pool_ext[...]
        for g, w in enumerate(POOL_WINDOWS):
            lanes = slice(g * POOL_GROUP, (g + 1) * POOL_GROUP)
            acc = ext[:, lanes]
            sh = 1
            while sh < w:
                acc = acc + pltpu.roll(acc, sh, 0)
                sh *= 2
            pm_scr[:, lanes] = acc[POOL_HALO:, :] * (1.0 / w) - u[:, lanes]
        conv_ext[CONV_HALO:, :] = cv
        cext = conv_ext[...]
        conv_scr[...] = (wc[0:1, :] * pltpu.roll(cext, 2, 0)[CONV_HALO:, :]
                         + wc[1:2, :] * pltpu.roll(cext, 1, 0)[CONV_HALO:, :]) + wc[2:3, :] * cv
        pool_ext[0:POOL_HALO, :] = u[MIX_TILE - POOL_HALO:, :]
        conv_ext[0:CONV_HALO, :] = cv[MIX_TILE - CONV_HALO:, :]

    @pl.when(is_prompt & (j == SEQ_TILES - 1))
    def _():
        pool_out_ref[...] = u[MIX_TILE - POOL_HALO:, :]
        conv_out_ref[...] = cv[MIX_TILE - CONV_HALO:, :]

    @pl.when(jnp.logical_not(is_prompt))
    def _():
        us = u[:DEC_BATCH, :]
        cvs = cv[:DEC_BATCH, :]
        pm_scr[...] = jnp.zeros_like(pm_scr)
        conv_scr[...] = jnp.zeros_like(conv_scr)
        for g, w in enumerate(POOL_WINDOWS):
            lanes = slice(g * POOL_GROUP, (g + 1) * POOL_GROUP)
            acc = us[:, lanes]
            for t in range(POOL_BUF - (w - 1), POOL_BUF):
                acc = acc + spool_ref[t, :, lanes]
            pm_scr[0:DEC_BATCH, lanes] = acc * (1.0 / w) - us[:, lanes]
        conv_scr[0:DEC_BATCH, :] = (wc[0:1, :] * sconv_ref[0] + wc[1:2, :] * sconv_ref[1]) + wc[2:3, :] * cvs
        us_ref[...] = us
        cvs_ref[...] = cvs

    branch_a, branch_b = _branches(z, pm_scr[...], conv_scr[...], wgrp_ref, pscale_ref, wup_ref, wcout_ref)
    x1, h2, idxs, gate = _merge_and_route(x, z, branch_a, branch_b, wo_ref, gffn_ref, wrt_ref, br_ref)
    x1_ref[...] = x1
    h2_ref[...] = h2
    for k in range(TOP_K):
        idx_ref[k:k + 1, :] = idxs[k]
        gate_ref[k:k + 1, :] = gate[k]


def _const_spec(shape):
    nd = len(shape)
    return pl.BlockSpec(shape, lambda *_: (0,) * nd)


def _mixer(x_prompt, xs_pad, spool_t, sconv_t, meta, w):
    last = MIX_PROMPT_STEPS - 1

    def x_map(s):
        sp = jnp.minimum(s, last)
        return (sp // SEQ_TILES, sp % SEQ_TILES, 0)

    def seq_map(s):
        return (jnp.minimum(s, last) // SEQ_TILES, 0, 0)

    consts = (xs_pad, spool_t, sconv_t, meta) + tuple(w)
    return pl.pallas_call(
        _mixer_kernel,
        grid=(MIX_PROMPT_STEPS + 1,),
        in_specs=[pl.BlockSpec((None, MIX_TILE, D_MODEL), x_map)] + [_const_spec(a.shape) for a in consts],
        out_specs=[
            pl.BlockSpec((MIX_TILE, D_MODEL), lambda s: (s, 0)),
            pl.BlockSpec((MIX_TILE, D_MODEL), lambda s: (s, 0)),
            pl.BlockSpec((TOP_K, MIX_TILE), lambda s: (0, s)),
            pl.BlockSpec((TOP_K, MIX_TILE), lambda s: (0, s)),
            pl.BlockSpec((None, POOL_HALO, D_POOL), seq_map),
            pl.BlockSpec((None, CONV_HALO, D_CONV), seq_map),
            _const_spec((DEC_BATCH, D_POOL)),
            _const_spec((DEC_BATCH, D_CONV)),
        ],
        out_shape=[
            jax.ShapeDtypeStruct((T_ALL, D_MODEL), _F32),
            jax.ShapeDtypeStruct((T_ALL, D_MODEL), _F32),
            jax.ShapeDtypeStruct((TOP_K, T_ALL), jnp.int32),
            jax.ShapeDtypeStruct((TOP_K, T_ALL), _F32),
            jax.ShapeDtypeStruct((BATCH, POOL_HALO, D_POOL), _F32),
            jax.ShapeDtypeStruct((BATCH, CONV_HALO, D_CONV), _F32),
            jax.ShapeDtypeStruct((DEC_BATCH, D_POOL), _F32),
            jax.ShapeDtypeStruct((DEC_BATCH, D_CONV), _F32),
        ],
        scratch_shapes=[
            pltpu.VMEM((POOL_HALO + MIX_TILE, D_POOL), _F32),
            pltpu.VMEM((CONV_HALO + MIX_TILE, D_CONV), _F32),
            pltpu.VMEM((N_META, D_POOL), _F32),
            pltpu.VMEM((N_META, D_CONV), _F32),
            pltpu.VMEM((MIX_TILE, D_POOL), _F32),
            pltpu.VMEM((MIX_TILE, D_CONV), _F32),
        ],
        compiler_params=pltpu.CompilerParams(
            dimension_semantics=("arbitrary",), vmem_limit_bytes=VMEM_LIMIT),
        name="mixer",
    )(x_prompt, *consts)


def _routing_tables(idx_all):
    e_flat = idx_all.reshape(N_ASSIGN)
    order = jnp.argsort(e_flat, stable=True).astype(jnp.int32)
    experts = jnp.arange(N_EXPERTS, dtype=jnp.int32)
    counts = jnp.sum((e_flat[:, None] == experts[None, :]).astype(jnp.int32), axis=0)
    padded = (counts + MOE_BLK - 1) // MOE_BLK * MOE_BLK
    pend = jnp.cumsum(padded)
    pstart = pend - padded
    start = jnp.cumsum(counts) - counts
    block_start = jnp.arange(N_BLOCKS, dtype=jnp.int32) * MOE_BLK
    block_exp = jnp.minimum(jnp.searchsorted(pend, block_start, side='right'),
                            N_EXPERTS - 1).astype(jnp.int32)
    r = jnp.arange(N_BLOCKS * MOE_BLK, dtype=jnp.int32)
    e_r = jnp.repeat(block_exp, MOE_BLK)
    jr = r - pstart[e_r]
    valid = jr < counts[e_r]
    a_r = order[jnp.clip(start[e_r] + jr, 0, N_ASSIGN - 1)]
    k_r = a_r // T_ALL
    t_r = a_r - k_r * T_ALL
    src = jnp.where(valid, t_r, 0).astype(jnp.int32)

    def spare_row(q):
        in_gap = q < TOP_K * SLAB_GAP
        gap_row = (q // SLAB_GAP) * T_PAD + T_ALL + q % SLAB_GAP
        return jnp.where(in_gap, gap_row, TAIL0 + q - TOP_K * SLAB_GAP)

    pad_rank = jnp.cumsum((~valid).astype(jnp.int32)) - 1
    dst = jnp.where(valid, k_r * T_PAD + t_r, spare_row(2 * MOE_BLK + pad_rank)).astype(jnp.int32)
    src_rows = jnp.concatenate(
        [src.reshape(N_BLOCKS, MOE_BLK), jnp.zeros((2, MOE_BLK), jnp.int32)], axis=0)
    warmup = spare_row(jnp.arange(2 * MOE_BLK, dtype=jnp.int32)).astype(jnp.int32).reshape(2, MOE_BLK)
    dst_rows = jnp.concatenate([warmup, dst.reshape(N_BLOCKS, MOE_BLK)], axis=0)
    table = jnp.concatenate([src_rows, dst_rows], axis=1)
    return block_exp, table


def _moe_kernel(bexp_ref, table_hbm, h2_hbm, wgu_ref, bgu_ref, wdn_ref, bdn_ref, out_hbm,
                xbuf0, xbuf1, ybuf0, ybuf1, tbl0, tbl1, gsem, ssem, tsem):
    del bexp_ref
    b = pl.program_id(0)
    nb = pl.num_programs(0)
    xbufs, ybufs, tbls = (xbuf0, xbuf1), (ybuf0, ybuf1), (tbl0, tbl1)

    def table_copy(row, p):
        return pltpu.make_async_copy(table_hbm.at[row], tbls[p], tsem.at[p])

    def gather_row(tok, r, p):
        return pltpu.make_async_copy(h2_hbm.at[pl.ds(tok, 1), :],
                                     xbufs[p].at[pl.ds(r, 1), :], gsem.at[p])

    def scatter_row(row, r, p):
        return pltpu.make_async_copy(ybufs[p].at[pl.ds(r, 1), :],
                                     out_hbm.at[pl.ds(row, 1), :], ssem.at[p])

    def start_gather(tp, p):
        for r in range(MOE_BLK):
            gather_row(tbls[tp][r], r, p).start()

    def start_scatter(tp, p):
        for r in range(MOE_BLK):
            scatter_row(tbls[tp][MOE_BLK + r], r, p).start()

    def wait_gather(p):
        for r in range(MOE_BLK):
            gather_row(0, r, p).wait()

    def wait_scatter(p):
        for r in range(MOE_BLK):
            scatter_row(0, r, p).wait()

    @pl.when(b == 0)
    def _():
        first = table_copy(0, 0)
        first.start()
        first.wait()
        ybuf0[...] = jnp.zeros_like(ybuf0)
        ybuf1[...] = jnp.zeros_like(ybuf1)
        start_gather(0, 0)
        start_scatter(0, 0)
        table_copy(1, 1).start()

    def step(p):
        q = 1 - p
        table_copy(b + 1, q).wait()
        wait_gather(p)
        start_gather(q, q)
        start_scatter(q, q)
        table_copy(b + 2, p).start()

        x = xbufs[p][...].astype(_BF16)
        hgu = jnp.dot(x, wgu_ref[...], preferred_element_type=_F32) + bgu_ref[...]
        gate = jnp.minimum(hgu[:, :D_FF], SWIGLU_LIMIT)
        up = jnp.clip(hgu[:, D_FF:], -SWIGLU_LIMIT, SWIGLU_LIMIT)
        act = gate * jax.nn.sigmoid(SWIGLU_ALPHA * gate) * (up + 1.0)
        y = jnp.dot(act.astype(_BF16), wdn_ref[...], preferred_element_type=_F32) + bdn_ref[...]

        wait_scatter(p)
        ybufs[p][...] = y

        @pl.when(b == nb - 1)
        def _():
            table_copy(b + 2, p).wait()
            start_scatter(p, p)
            wait_scatter(p)
            wait_scatter(q)
            wait_gather(q)

    parity = lax.rem(b, 2)
    pl.when(parity == 0)(lambda: step(0))
    pl.when(parity == 1)(lambda: step(1))


def _moe_experts(block_exp, table, h2_all, wgu, bgu, wdn, bdn):
    grid_spec = pltpu.PrefetchScalarGridSpec(
        num_scalar_prefetch=1,
        grid=(N_BLOCKS,),
        in_specs=[
            pl.BlockSpec(memory_space=pl.ANY),
            pl.BlockSpec(memory_space=pl.ANY),
            pl.BlockSpec((None, D_MODEL, 2 * D_FF), lambda b, be: (be[b], 0, 0)),
            pl.BlockSpec((None, 1, 2 * D_FF), lambda b, be: (be[b], 0, 0)),
            pl.BlockSpec((None, D_FF, D_MODEL), lambda b, be: (be[b], 0, 0)),
            pl.BlockSpec((None, 1, D_MODEL), lambda b, be: (be[b], 0, 0)),
        ],
        out_specs=pl.BlockSpec(memory_space=pl.ANY),
        scratch_shapes=[
            pltpu.VMEM((MOE_BLK, D_MODEL), _F32),
            pltpu.VMEM((MOE_BLK, D_MODEL), _F32),
            pltpu.VMEM((MOE_BLK, D_MODEL), _F32),
            pltpu.VMEM((MOE_BLK, D_MODEL), _F32),
            pltpu.SMEM((2 * MOE_BLK,), jnp.int32),
            pltpu.SMEM((2 * MOE_BLK,), jnp.int32),
            pltpu.SemaphoreType.DMA((2,)),
            pltpu.SemaphoreType.DMA((2,)),
            pltpu.SemaphoreType.DMA((2,)),
        ],
    )
    return pl.pallas_call(
        _moe_kernel,
        grid_spec=grid_spec,
        out_shape=jax.ShapeDtypeStruct((OUT_ROWS, D_MODEL), _F32),
        compiler_params=pltpu.CompilerParams(
            dimension_semantics=("arbitrary",), vmem_limit_bytes=VMEM_LIMIT),
        name="moe_experts",
    )(block_exp, table, h2_all, wgu, bgu, wdn, bdn)


def _combine_kernel(x1_ref, g_ref, o0_ref, o1_ref, o2_ref, o3_ref, gfin_ref, yp_ref, ys_ref):
    i = pl.program_id(0)
    g = g_ref[...]
    moe = (g[:, 0:1] * o0_ref[...] + g[:, 1:2] * o1_ref[...]) + (g[:, 2:3] * o2_ref[...] + g[:, 3:4] * o3_ref[...])
    y = _rms(x1_ref[...] + moe, gfin_ref[...])

    @pl.when(i < COMB_PROMPT_STEPS)
    def _():
        yp_ref[...] = y

    @pl.when(i == COMB_PROMPT_STEPS)
    def _():
        ys_ref[...] = y[:DEC_BATCH, :]


def _combine(x1, gates_tk, out4, g_final):
    slab_blocks = T_PAD // COMB_TILE

    def slab_spec(k):
        return pl.BlockSpec((COMB_TILE, D_MODEL), lambda i, k=k: (k * slab_blocks + i, 0))

    return pl.pallas_call(
        _combine_kernel,
        grid=(COMB_PROMPT_STEPS + 1,),
        in_specs=[pl.BlockSpec((COMB_TILE, D_MODEL), lambda i: (i, 0)),
                  pl.BlockSpec((COMB_TILE, TOP_K), lambda i: (i, 0)),
                  slab_spec(0), slab_spec(1), slab_spec(2), slab_spec(3),
                  _const_spec(g_final.shape)],
        out_specs=[pl.BlockSpec((COMB_TILE, D_MODEL), lambda i: (jnp.minimum(i, COMB_PROMPT_STEPS - 1), 0)),
                   _const_spec((DEC_BATCH, D_MODEL))],
        out_shape=[jax.ShapeDtypeStruct((T_PROMPT, D_MODEL), _F32),
                   jax.ShapeDtypeStruct((DEC_BATCH, D_MODEL), _F32)],
        compiler_params=pltpu.CompilerParams(
            dimension_semantics=("arbitrary",), vmem_limit_bytes=VMEM_LIMIT),
        name="combine",
    )(x1, gates_tk, out4, out4, out4, out4, g_final)


def kernel(x_prompt, x_sample, state_pool, state_conv, meta_tokens, g_mix, w_in, w_pool_grp, pool_scale,
           w_pool_up, w_conv, w_conv_out, w_o, g_ffn, w_router, b_router, w_gu, b_gu, w_down, b_down, g_final):
    assert g_mix.shape[0] == 1, "single-layer step"
    grp = w_pool_grp[0].astype(_BF16)
    zero = jnp.zeros((POOL_GROUP, POOL_GROUP), _BF16)
    wgrp = jnp.stack([jnp.block([[grp[0], zero], [zero, grp[1]]]),
                      jnp.block([[grp[2], zero], [zero, grp[3]]])])
    w = (
        g_mix[0].reshape(1, D_MODEL),
        w_in[0].astype(_BF16),
        wgrp,
        pool_scale[0].reshape(1, D_POOL),
        w_pool_up[0].astype(_BF16),
        w_conv[0],
        w_conv_out[0].astype(_BF16),
        w_o[0].astype(_BF16),
        g_ffn[0].reshape(1, D_MODEL),
        w_router[0].T,
        b_router[0].reshape(N_EXPERTS, 1),
    )
    xs_pad = jnp.pad(x_sample.reshape(DEC_BATCH, D_MODEL), ((0, MIX_TILE - DEC_BATCH), (0, 0)))
    spool_t = jnp.transpose(state_pool[0], (1, 0, 2))
    sconv_t = jnp.transpose(state_conv[0], (1, 0, 2))
    x1, h2, idx, gate, pool_p, conv_p, u_s, cv_s = _mixer(x_prompt, xs_pad, spool_t, sconv_t, meta_tokens, w)

    block_exp, table = _routing_tables(idx)
    out4 = _moe_experts(block_exp, table, h2,
                        w_gu[0].astype(_BF16), b_gu[0].reshape(N_EXPERTS, 1, 2 * D_FF),
                        w_down[0].astype(_BF16), b_down[0].reshape(N_EXPERTS, 1, D_MODEL))

    y_p, y_s = _combine(x1, gate.T, out4, g_final.reshape(1, D_MODEL))

    new_pool_p = pool_p[:, POOL_HALO - POOL_BUF:, :][None]
    new_conv_p = conv_p[:, CONV_HALO - CONV_BUF:, :][None]
    new_pool_s = jnp.concatenate([state_pool[0][:, 1:, :], u_s[:, None, :]], axis=1)[None]
    new_conv_s = jnp.concatenate([state_conv[0][:, 1:, :], cv_s[:, None, :]], axis=1)[None]
    return (y_p.reshape(BATCH, SEQ, D_MODEL), y_s.reshape(DEC_BATCH, 1, D_MODEL),
            new_pool_p, new_conv_p, new_pool_s, new_conv_s)
```

```python
import jax
import jax.numpy as jnp
from jax import lax
from jax.experimental import pallas as pl
from jax.experimental.pallas import tpu as pltpu

D_MODEL = 1024
BATCH = 8
SEQ = 2048
DEC_BATCH = 128
N_META = 16
D_POOL = 512
POOL_GROUP = 128
POOL_WINDOWS = (2, 4, 8, 16)
POOL_BUF = 15
D_CONV = 512
CONV_BUF = 2
N_EXPERTS = 32
TOP_K = 4
D_FF = 1024
SWIGLU_LIMIT = 7.0
SWIGLU_ALPHA = 1.702
RMS_EPS = 1e-5

O_CG = D_POOL
O_BG = O_CG + D_CONV
O_V = O_BG + D_CONV
O_GL = O_V + D_CONV

T_PROMPT = BATCH * SEQ
T_ALL = T_PROMPT + DEC_BATCH
N_ASSIGN = TOP_K * T_ALL

MIX_TILE = 256
MIX_PROMPT_STEPS = T_PROMPT // MIX_TILE
SEQ_TILES = SEQ // MIX_TILE
POOL_HALO = 16
CONV_HALO = 8
MOE_BLK = 256
N_BLOCKS = -(-N_ASSIGN // MOE_BLK) + N_EXPERTS
COMB_TILE = 512
COMB_PROMPT_STEPS = T_PROMPT // COMB_TILE
T_PAD = (T_ALL + COMB_TILE - 1) // COMB_TILE * COMB_TILE
SLAB_GAP = T_PAD - T_ALL
OUT_ROWS = (N_BLOCKS + 2) * MOE_BLK
TAIL0 = TOP_K * T_PAD
assert OUT_ROWS >= TAIL0

ROW_TILE = 8
LANES = D_MODEL // ROW_TILE

VMEM_LIMIT = 56 * 1024 * 1024

_F32 = jnp.float32
_BF16 = jnp.bfloat16


def _rms(x, g):
    ms = jnp.mean(x * x, axis=-1, keepdims=True)
    return x * lax.rsqrt(ms + RMS_EPS) * g


def _bdot(a, w):
    return jnp.dot(a.astype(_BF16), w, preferred_element_type=_F32)


def _store_row_tiles(ref, val):
    rows = val.shape[0]
    for c in range(ROW_TILE):
        ref[pl.ds(c, rows, stride=ROW_TILE), :] = val[:, c * LANES:(c + 1) * LANES]


def _load_row_tiles(ref, rows):
    return jnp.concatenate([ref[pl.ds(c, rows, stride=ROW_TILE), :] for c in range(ROW_TILE)], axis=-1)


def _branches(z, pm, conv, wgrp_ref, pscale_ref, wup_ref, wcout_ref):
    pmb = pm.astype(_BF16)
    half = 2 * POOL_GROUP
    pg = jnp.concatenate(
        [jnp.dot(pmb[:, :half], wgrp_ref[0], preferred_element_type=_F32),
         jnp.dot(pmb[:, half:], wgrp_ref[1], preferred_element_type=_F32)], axis=-1)
    branch_a = _bdot(pg * pscale_ref[...], wup_ref[...])
    branch_b = _bdot(z[:, O_BG:O_V] * conv, wcout_ref[...])
    return branch_a, branch_b


def _merge_and_route(x, z, branch_a, branch_b, wo_ref, gffn_ref, wrt_ref, br_ref):
    gates = jax.nn.sigmoid(z[:, O_GL:])
    merged = gates[:, :D_MODEL] * branch_a + gates[:, D_MODEL:] * branch_b
    x1 = x + _bdot(merged, wo_ref[...])
    h2 = _rms(x1, gffn_ref[...])
    logits = lax.dot_general(wrt_ref[...], h2, (((1,), (1,)), ((), ())),
                             precision=lax.Precision.HIGHEST,
                             preferred_element_type=_F32) + br_ref[...]
    iota = lax.broadcasted_iota(jnp.int32, logits.shape, 0)
    vals, idxs = [], []
    cur = logits
    for _ in range(TOP_K):
        m = jnp.max(cur, axis=0, keepdims=True)
        ik = jnp.min(jnp.where(cur == m, iota, N_EXPERTS), axis=0, keepdims=True)
        vals.append(m)
        idxs.append(ik)
        cur = jnp.where(iota == ik, -jnp.inf, cur)
    exps = [jnp.exp(v - vals[0]) for v in vals]
    denom = (exps[0] + exps[1]) + (exps[2] + exps[3])
    gate = [e / denom for e in exps]
    return x1, h2, idxs, gate


def _mixer_kernel(xp_ref, xs_ref, spool_ref, sconv_ref, meta_ref, gmix_ref, win_ref, wgrp_ref, pscale_ref,
                  wup_ref, wconv_ref, wcout_ref, wo_ref, gffn_ref, wrt_ref, br_ref,
                  x1_ref, h2_ref, idx_ref, gate_ref, pool_out_ref, conv_out_ref, us_ref, cvs_ref,
                  pool_ext, conv_ext, meta_pool, meta_conv, pm_scr, conv_scr):
    s = pl.program_id(0)
    is_prompt = s < MIX_PROMPT_STEPS
    j = lax.rem(s, SEQ_TILES)

    @pl.when(s == 0)
    def _():
        hm = _rms(meta_ref[...], gmix_ref[...]).astype(_BF16)
        zm = jnp.dot(hm, win_ref[:, :O_GL], preferred_element_type=_F32)
        meta_pool[...] = zm[:, :D_POOL]
        meta_conv[...] = zm[:, O_CG:O_BG] * zm[:, O_V:O_GL]

    @pl.when(is_prompt & (j == 0))
    def _():
        pool_ext[0:POOL_HALO, :] = meta_pool[...]
        conv_ext[0:CONV_HALO, :] = meta_conv[N_META - CONV_HALO:, :]

    x = jnp.where(is_prompt, xp_ref[...], xs_ref[...])
    h = _rms(x, gmix_ref[...]).astype(_BF16)
    z = jnp.dot(h, win_ref[...], preferred_element_type=_F32)
    u = z[:, :D_POOL]
    cv = z[:, O_CG:O_BG] * z[:, O_V:O_GL]
    wc = wconv_ref[...]

    @pl.when(is_prompt)
    def _():
        pool_ext[POOL_HALO:, :] = u
        ext = pool_ext[...]
        for g, w in enumerate(POOL_WINDOWS):
            lanes = slice(g * POOL_GROUP, (g + 1) * POOL_GROUP)
            acc = ext[:, lanes]
            sh = 1
            while sh < w:
                acc = acc + pltpu.roll(acc, sh, 0)
                sh *= 2
            pm_scr[:, lanes] = acc[POOL_HALO:, :] * (1.0 / w) - u[:, lanes]
        conv_ext[CONV_HALO:, :] = cv
        cext = conv_ext[...]
        conv_scr[...] = (wc[0:1, :] * pltpu.roll(cext, 2, 0)[CONV_HALO:, :]
                         + wc[1:2, :] * pltpu.roll(cext, 1, 0)[CONV_HALO:, :]) + wc[2:3, :] * cv
        pool_ext[0:POOL_HALO, :] = u[MIX_TILE - POOL_HALO:, :]
        conv_ext[0:CONV_HALO, :] = cv[MIX_TILE - CONV_HALO:, :]

    @pl.when(is_prompt & (j == SEQ_TILES - 1))
    def _():
        pool_out_ref[...] = u[MIX_TILE - POOL_HALO:, :]
        conv_out_ref[...] = cv[MIX_TILE - CONV_HALO:, :]

    @pl.when(jnp.logical_not(is_prompt))
    def _():
        us = u[:DEC_BATCH, :]
        cvs = cv[:DEC_BATCH, :]
        pm_scr[...] = jnp.zeros_like(pm_scr)
        conv_scr[...] = jnp.zeros_like(conv_scr)
        for g, w in enumerate(POOL_WINDOWS):
            lanes = slice(g * POOL_GROUP, (g + 1) * POOL_GROUP)
            acc = us[:, lanes]
            for t in range(POOL_BUF - (w - 1), POOL_BUF):
                acc = acc + spool_ref[t, :, lanes]
            pm_scr[0:DEC_BATCH, lanes] = acc * (1.0 / w) - us[:, lanes]
        conv_scr[0:DEC_BATCH, :] = (wc[0:1, :] * sconv_ref[0] + wc[1:2, :] * sconv_ref[1]) + wc[2:3, :] * cvs
        us_ref[...] = us
        cvs_ref[...] = cvs

    branch_a, branch_b = _branches(z, pm_scr[...], conv_scr[...], wgrp_ref, pscale_ref, wup_ref, wcout_ref)
    x1, h2, idxs, gate = _merge_and_route(x, z, branch_a, branch_b, wo_ref, gffn_ref, wrt_ref, br_ref)
    x1_ref[...] = x1
    _store_row_tiles(h2_ref, h2)
    for k in range(TOP_K):
        idx_ref[k:k + 1, :] = idxs[k]
        gate_ref[k:k + 1, :] = gate[k]


def _const_spec(shape):
    nd = len(shape)
    return pl.BlockSpec(shape, lambda *_: (0,) * nd)


def _mixer(x_prompt, xs_pad, spool_t, sconv_t, meta, w):
    last = MIX_PROMPT_STEPS - 1

    def x_map(s):
        sp = jnp.minimum(s, last)
        return (sp // SEQ_TILES, sp % SEQ_TILES, 0)

    def seq_map(s):
        return (jnp.minimum(s, last) // SEQ_TILES, 0, 0)

    consts = (xs_pad, spool_t, sconv_t, meta) + tuple(w)
    return pl.pallas_call(
        _mixer_kernel,
        grid=(MIX_PROMPT_STEPS + 1,),
        in_specs=[pl.BlockSpec((None, MIX_TILE, D_MODEL), x_map)] + [_const_spec(a.shape) for a in consts],
        out_specs=[
            pl.BlockSpec((MIX_TILE, D_MODEL), lambda s: (s, 0)),
            pl.BlockSpec((MIX_TILE * ROW_TILE, LANES), lambda s: (s, 0)),
            pl.BlockSpec((TOP_K, MIX_TILE), lambda s: (0, s)),
            pl.BlockSpec((TOP_K, MIX_TILE), lambda s: (0, s)),
            pl.BlockSpec((None, POOL_HALO, D_POOL), seq_map),
            pl.BlockSpec((None, CONV_HALO, D_CONV), seq_map),
            _const_spec((DEC_BATCH, D_POOL)),
            _const_spec((DEC_BATCH, D_CONV)),
        ],
        out_shape=[
            jax.ShapeDtypeStruct((T_ALL, D_MODEL), _F32),
            jax.ShapeDtypeStruct((T_ALL * ROW_TILE, LANES), _F32),
            jax.ShapeDtypeStruct((TOP_K, T_ALL), jnp.int32),
            jax.ShapeDtypeStruct((TOP_K, T_ALL), _F32),
            jax.ShapeDtypeStruct((BATCH, POOL_HALO, D_POOL), _F32),
            jax.ShapeDtypeStruct((BATCH, CONV_HALO, D_CONV), _F32),
            jax.ShapeDtypeStruct((DEC_BATCH, D_POOL), _F32),
            jax.ShapeDtypeStruct((DEC_BATCH, D_CONV), _F32),
        ],
        scratch_shapes=[
            pltpu.VMEM((POOL_HALO + MIX_TILE, D_POOL), _F32),
            pltpu.VMEM((CONV_HALO + MIX_TILE, D_CONV), _F32),
            pltpu.VMEM((N_META, D_POOL), _F32),
            pltpu.VMEM((N_META, D_CONV), _F32),
            pltpu.VMEM((MIX_TILE, D_POOL), _F32),
            pltpu.VMEM((MIX_TILE, D_CONV), _F32),
        ],
        compiler_params=pltpu.CompilerParams(
            dimension_semantics=("arbitrary",), vmem_limit_bytes=VMEM_LIMIT),
        name="mixer",
    )(x_prompt, *consts)


def _routing_tables(idx_all):
    e_flat = idx_all.reshape(N_ASSIGN)
    order = jnp.argsort(e_flat, stable=True).astype(jnp.int32)
    experts = jnp.arange(N_EXPERTS, dtype=jnp.int32)
    counts = jnp.sum((e_flat[:, None] == experts[None, :]).astype(jnp.int32), axis=0)
    padded = (counts + MOE_BLK - 1) // MOE_BLK * MOE_BLK
    pend = jnp.cumsum(padded)
    pstart = pend - padded
    start = jnp.cumsum(counts) - counts
    block_start = jnp.arange(N_BLOCKS, dtype=jnp.int32) * MOE_BLK
    block_exp = jnp.minimum(jnp.sum((block_start[:, None] >= pend[None, :]).astype(jnp.int32), axis=1),
                            N_EXPERTS - 1)
    r = jnp.arange(N_BLOCKS * MOE_BLK, dtype=jnp.int32)
    e_r = jnp.repeat(block_exp, MOE_BLK)
    jr = r - pstart[e_r]
    valid = jr < counts[e_r]
    a_r = order[jnp.clip(start[e_r] + jr, 0, N_ASSIGN - 1)]
    k_r = a_r // T_ALL
    t_r = a_r - k_r * T_ALL
    src = jnp.where(valid, t_r, 0).astype(jnp.int32)

    def spare_row(q):
        in_gap = q < TOP_K * SLAB_GAP
        gap_row = (q // SLAB_GAP) * T_PAD + T_ALL + q % SLAB_GAP
        return jnp.where(in_gap, gap_row, TAIL0 + q - TOP_K * SLAB_GAP)

    pad_rank = jnp.cumsum((~valid).astype(jnp.int32)) - 1
    dst = jnp.where(valid, k_r * T_PAD + t_r, spare_row(2 * MOE_BLK + pad_rank)).astype(jnp.int32)
    src_rows = jnp.concatenate(
        [src.reshape(N_BLOCKS, MOE_BLK), jnp.zeros((2, MOE_BLK), jnp.int32)], axis=0)
    warmup = spare_row(jnp.arange(2 * MOE_BLK, dtype=jnp.int32)).astype(jnp.int32).reshape(2, MOE_BLK)
    dst_rows = jnp.concatenate([warmup, dst.reshape(N_BLOCKS, MOE_BLK)], axis=0)
    table = jnp.concatenate([src_rows, dst_rows], axis=1) * ROW_TILE
    return block_exp, table


def _moe_kernel(bexp_ref, table_hbm, h2_hbm, wgu_ref, bgu_ref, wdn_ref, bdn_ref, out_hbm,
                xbuf0, xbuf1, ybuf0, ybuf1, tbl0, tbl1, gsem, ssem, tsem):
    del bexp_ref
    b = pl.program_id(0)
    nb = pl.num_programs(0)
    xbufs, ybufs, tbls = (xbuf0, xbuf1), (ybuf0, ybuf1), (tbl0, tbl1)

    def table_copy(row, p):
        return pltpu.make_async_copy(table_hbm.at[row], tbls[p], tsem.at[p])

    def gather_row(off, r, p):
        return pltpu.make_async_copy(h2_hbm.at[pl.ds(off, ROW_TILE), :],
                                     xbufs[p].at[pl.ds(r * ROW_TILE, ROW_TILE), :], gsem.at[p])

    def scatter_row(off, r, p):
        return pltpu.make_async_copy(ybufs[p].at[pl.ds(r * ROW_TILE, ROW_TILE), :],
                                     out_hbm.at[pl.ds(off, ROW_TILE), :], ssem.at[p])

    def start_gather(tp, p):
        for r in range(MOE_BLK):
            gather_row(pl.multiple_of(tbls[tp][r], ROW_TILE), r, p).start()

    def start_scatter(tp, p):
        for r in range(MOE_BLK):
            scatter_row(pl.multiple_of(tbls[tp][MOE_BLK + r], ROW_TILE), r, p).start()

    def wait_gather(p):
        for r in range(MOE_BLK):
            gather_row(0, r, p).wait()

    def wait_scatter(p):
        for r in range(MOE_BLK):
            scatter_row(0, r, p).wait()

    @pl.when(b == 0)
    def _():
        first = table_copy(0, 0)
        first.start()
        first.wait()
        ybuf0[...] = jnp.zeros_like(ybuf0)
        ybuf1[...] = jnp.zeros_like(ybuf1)
        start_gather(0, 0)
        start_scatter(0, 0)
        table_copy(1, 1).start()

    def step(p):
        q = 1 - p
        table_copy(b + 1, q).wait()
        wait_gather(p)
        start_gather(q, q)
        start_scatter(q, q)
        table_copy(b + 2, p).start()

        x = _load_row_tiles(xbufs[p], MOE_BLK).astype(_BF16)
        hgu = jnp.dot(x, wgu_ref[...], preferred_element_type=_F32) + bgu_ref[...]
        gate = jnp.minimum(hgu[:, :D_FF], SWIGLU_LIMIT)
        up = jnp.clip(hgu[:, D_FF:], -SWIGLU_LIMIT, SWIGLU_LIMIT)
        act = gate * jax.nn.sigmoid(SWIGLU_ALPHA * gate) * (up + 1.0)
        y = jnp.dot(act.astype(_BF16), wdn_ref[...], preferred_element_type=_F32) + bdn_ref[...]

        wait_scatter(p)
        _store_row_tiles(ybufs[p], y)

        @pl.when(b == nb - 1)
        def _():
            table_copy(b + 2, p).wait()
            start_scatter(p, p)
            wait_scatter(p)
            wait_scatter(q)
            wait_gather(q)

    parity = lax.rem(b, 2)
    pl.when(parity == 0)(lambda: step(0))
    pl.when(parity == 1)(lambda: step(1))


def _moe_experts(block_exp, table, h2_all, wgu, bgu, wdn, bdn):
    grid_spec = pltpu.PrefetchScalarGridSpec(
        num_scalar_prefetch=1,
        grid=(N_BLOCKS,),
        in_specs=[
            pl.BlockSpec(memory_space=pl.ANY),
            pl.BlockSpec(memory_space=pl.ANY),
            pl.BlockSpec((None, D_MODEL, 2 * D_FF), lambda b, be: (be[b], 0, 0)),
            pl.BlockSpec((None, 1, 2 * D_FF), lambda b, be: (be[b], 0, 0)),
            pl.BlockSpec((None, D_FF, D_MODEL), lambda b, be: (be[b], 0, 0)),
            pl.BlockSpec((None, 1, D_MODEL), lambda b, be: (be[b], 0, 0)),
        ],
        out_specs=pl.BlockSpec(memory_space=pl.ANY),
        scratch_shapes=[
            pltpu.VMEM((MOE_BLK * ROW_TILE, LANES), _F32),
            pltpu.VMEM((MOE_BLK * ROW_TILE, LANES), _F32),
            pltpu.VMEM((MOE_BLK * ROW_TILE, LANES), _F32),
            pltpu.VMEM((MOE_BLK * ROW_TILE, LANES), _F32),
            pltpu.SMEM((2 * MOE_BLK,), jnp.int32),
            pltpu.SMEM((2 * MOE_BLK,), jnp.int32),
            pltpu.SemaphoreType.DMA((2,)),
            pltpu.SemaphoreType.DMA((2,)),
            pltpu.SemaphoreType.DMA((2,)),
        ],
    )
    return pl.pallas_call(
        _moe_kernel,
        grid_spec=grid_spec,
        out_shape=jax.ShapeDtypeStruct((OUT_ROWS * ROW_TILE, LANES), _F32),
        compiler_params=pltpu.CompilerParams(
            dimension_semantics=("arbitrary",), vmem_limit_bytes=VMEM_LIMIT),
        name="moe_experts",
    )(block_exp, table, h2_all, wgu, bgu, wdn, bdn)


def _combine_kernel(x1_ref, g_ref, o0_ref, o1_ref, o2_ref, o3_ref, gfin_ref, yp_ref, ys_ref):
    i = pl.program_id(0)
    g = g_ref[...]
    o = [_load_row_tiles(r, COMB_TILE) for r in (o0_ref, o1_ref, o2_ref, o3_ref)]
    moe = (g[:, 0:1] * o[0] + g[:, 1:2] * o[1]) + (g[:, 2:3] * o[2] + g[:, 3:4] * o[3])
    y = _rms(x1_ref[...] + moe, gfin_ref[...])

    @pl.when(i < COMB_PROMPT_STEPS)
    def _():
        yp_ref[...] = y

    @pl.when(i == COMB_PROMPT_STEPS)
    def _():
        ys_ref[...] = y[:DEC_BATCH, :]


def _combine(x1, gates_tk, out4, g_final):
    slab_blocks = T_PAD // COMB_TILE

    def slab_spec(k):
        return pl.BlockSpec((COMB_TILE * ROW_TILE, LANES), lambda i, k=k: (k * slab_blocks + i, 0))

    return pl.pallas_call(
        _combine_kernel,
        grid=(COMB_PROMPT_STEPS + 1,),
        in_specs=[pl.BlockSpec((COMB_TILE, D_MODEL), lambda i: (i, 0)),
                  pl.BlockSpec((COMB_TILE, TOP_K), lambda i: (i, 0)),
                  slab_spec(0), slab_spec(1), slab_spec(2), slab_spec(3),
                  _const_spec(g_final.shape)],
        out_specs=[pl.BlockSpec((COMB_TILE, D_MODEL), lambda i: (jnp.minimum(i, COMB_PROMPT_STEPS - 1), 0)),
                   _const_spec((DEC_BATCH, D_MODEL))],
        out_shape=[jax.ShapeDtypeStruct((T_PROMPT, D_MODEL), _F32),
                   jax.ShapeDtypeStruct((DEC_BATCH, D_MODEL), _F32)],
        compiler_params=pltpu.CompilerParams(
            dimension_semantics=("arbitrary",), vmem_limit_bytes=VMEM_LIMIT),
        name="combine",
    )(x1, gates_tk, out4, out4, out4, out4, g_final)


def kernel(x_prompt, x_sample, state_pool, state_conv, meta_tokens, g_mix, w_in, w_pool_grp, pool_scale,
           w_pool_up, w_conv, w_conv_out, w_o, g_ffn, w_router, b_router, w_gu, b_gu, w_down, b_down, g_final):
    assert g_mix.shape[0] == 1, "single-layer step"
    grp = w_pool_grp[0].astype(_BF16)
    zero = jnp.zeros((POOL_GROUP, POOL_GROUP), _BF16)
    wgrp = jnp.stack([jnp.block([[grp[0], zero], [zero, grp[1]]]),
                      jnp.block([[grp[2], zero], [zero, grp[3]]])])
    w = (
        g_mix[0].reshape(1, D_MODEL),
        w_in[0].astype(_BF16),
        wgrp,
        pool_scale[0].reshape(1, D_POOL),
        w_pool_up[0].astype(_BF16),
        w_conv[0],
        w_conv_out[0].astype(_BF16),
        w_o[0].astype(_BF16),
        g_ffn[0].reshape(1, D_MODEL),
        w_router[0].T,
        b_router[0].reshape(N_EXPERTS, 1),
    )
    xs_pad = jnp.pad(x_sample.reshape(DEC_BATCH, D_MODEL), ((0, MIX_TILE - DEC_BATCH), (0, 0)))
    spool_t = jnp.transpose(state_pool[0], (1, 0, 2))
    sconv_t = jnp.transpose(state_conv[0], (1, 0, 2))
    x1, h2, idx, gate, pool_p, conv_p, u_s, cv_s = _mixer(x_prompt, xs_pad, spool_t, sconv_t, meta_tokens, w)

    block_exp, table = _routing_tables(idx)
    out4 = _moe_experts(block_exp, table, h2,
                        w_gu[0].astype(_BF16), b_gu[0].reshape(N_EXPERTS, 1, 2 * D_FF),
                        w_down[0].astype(_BF16), b_down[0].reshape(N_EXPERTS, 1, D_MODEL))

    y_p, y_s = _combine(x1, gate.T, out4, g_final.reshape(1, D_MODEL))

    new_pool_p = pool_p[:, POOL_HALO - POOL_BUF:, :][None]
    new_conv_p = conv_p[:, CONV_HALO - CONV_BUF:, :][None]
    new_pool_s = jnp.concatenate([state_pool[0][:, 1:, :], u_s[:, None, :]], axis=1)[None]
    new_conv_s = jnp.concatenate([state_conv[0][:, 1:, :], cv_s[:, None, :]], axis=1)[None]
    return (y_p.reshape(BATCH, SEQ, D_MODEL), y_s.reshape(DEC_BATCH, 1, D_MODEL),
            new_pool_p, new_conv_p, new_pool_s, new_conv_s)
```

```python
import jax
import jax.numpy as jnp
from jax import lax
from jax.experimental import pallas as pl
from jax.experimental.pallas import tpu as pltpu

D_MODEL = 1024
BATCH = 8
SEQ = 2048
DEC_BATCH = 128
N_META = 16
D_POOL = 512
POOL_GROUP = 128
POOL_WINDOWS = (2, 4, 8, 16)
POOL_BUF = 15
D_CONV = 512
CONV_BUF = 2
N_EXPERTS = 32
TOP_K = 4
D_FF = 1024
SWIGLU_LIMIT = 7.0
SWIGLU_ALPHA = 1.702
RMS_EPS = 1e-5

O_CG = D_POOL
O_BG = O_CG + D_CONV
O_V = O_BG + D_CONV
O_GL = O_V + D_CONV

T_PROMPT = BATCH * SEQ
T_ALL = T_PROMPT + DEC_BATCH
N_ASSIGN = TOP_K * T_ALL

MIX_TILE = 256
MIX_PROMPT_STEPS = T_PROMPT // MIX_TILE
SEQ_TILES = SEQ // MIX_TILE
POOL_HALO = 16
CONV_HALO = 8
MOE_BLK = 256
N_BLOCKS = -(-N_ASSIGN // MOE_BLK) + N_EXPERTS
COMB_TILE = 512
COMB_PROMPT_STEPS = T_PROMPT // COMB_TILE
T_PAD = (T_ALL + COMB_TILE - 1) // COMB_TILE * COMB_TILE
SLAB_GAP = T_PAD - T_ALL
OUT_ROWS = (N_BLOCKS + 2) * MOE_BLK
TAIL0 = TOP_K * T_PAD
assert OUT_ROWS >= TAIL0

ROW_TILE = 8
LANES = D_MODEL // ROW_TILE

VMEM_LIMIT = 56 * 1024 * 1024

_F32 = jnp.float32
_BF16 = jnp.bfloat16


def _rms(x, g):
    ms = jnp.mean(x * x, axis=-1, keepdims=True)
    return x * lax.rsqrt(ms + RMS_EPS) * g


def _bdot(a, w):
    return jnp.dot(a.astype(_BF16), w, preferred_element_type=_F32)


def _store_row_tiles(ref, val):
    rows = val.shape[0]
    for c in range(ROW_TILE):
        ref[pl.ds(c, rows, stride=ROW_TILE), :] = val[:, c * LANES:(c + 1) * LANES]


def _load_row_tiles(ref, rows):
    return jnp.concatenate([ref[pl.ds(c, rows, stride=ROW_TILE), :] for c in range(ROW_TILE)], axis=-1)


def _branches(z, pm, conv, wgrp_ref, pscale_ref, wup_ref, wcout_ref):
    pmb = pm.astype(_BF16)
    half = 2 * POOL_GROUP
    pg = jnp.concatenate(
        [jnp.dot(pmb[:, :half], wgrp_ref[0], preferred_element_type=_F32),
         jnp.dot(pmb[:, half:], wgrp_ref[1], preferred_element_type=_F32)], axis=-1)
    branch_a = _bdot(pg * pscale_ref[...], wup_ref[...])
    branch_b = _bdot(z[:, O_BG:O_V] * conv, wcout_ref[...])
    return branch_a, branch_b


def _merge_and_route(x, z, branch_a, branch_b, wo_ref, gffn_ref, wrt_ref, br_ref):
    gates = jax.nn.sigmoid(z[:, O_GL:])
    merged = gates[:, :D_MODEL] * branch_a + gates[:, D_MODEL:] * branch_b
    x1 = x + _bdot(merged, wo_ref[...])
    h2 = _rms(x1, gffn_ref[...])
    logits = lax.dot_general(wrt_ref[...], h2, (((1,), (1,)), ((), ())),
                             precision=lax.Precision.HIGHEST,
                             preferred_element_type=_F32) + br_ref[...]
    iota = lax.broadcasted_iota(jnp.int32, logits.shape, 0)
    vals, idxs = [], []
    cur = logits
    for _ in range(TOP_K):
        m = jnp.max(cur, axis=0, keepdims=True)
        ik = jnp.min(jnp.where(cur == m, iota, N_EXPERTS), axis=0, keepdims=True)
        vals.append(m)
        idxs.append(ik)
        cur = jnp.where(iota == ik, -jnp.inf, cur)
    exps = [jnp.exp(v - vals[0]) for v in vals]
    denom = (exps[0] + exps[1]) + (exps[2] + exps[3])
    gate = [e / denom for e in exps]
    return x1, h2, idxs, gate


def _mixer_kernel(xp_ref, xs_ref, spool_ref, sconv_ref, meta_ref, gmix_ref, win_ref, wgrp_ref, pscale_ref,
                  wup_ref, wconv_ref, wcout_ref, wo_ref, gffn_ref, wrt_ref, br_ref,
                  x1_ref, h2_ref, idx_ref, gate_ref, pool_out_ref, conv_out_ref, us_ref, cvs_ref,
                  pool_ext, conv_ext, meta_pool, meta_conv, pm_scr, conv_scr):
    s = pl.program_id(0)
    is_prompt = s < MIX_PROMPT_STEPS
    j = lax.rem(s, SEQ_TILES)

    @pl.when(s == 0)
    def _():
        hm = _rms(meta_ref[...], gmix_ref[...]).astype(_BF16)
        zm = jnp.dot(hm, win_ref[:, :O_GL], preferred_element_type=_F32)
        meta_pool[...] = zm[:, :D_POOL]
        meta_conv[...] = zm[:, O_CG:O_BG] * zm[:, O_V:O_GL]

    @pl.when(is_prompt & (j == 0))
    def _():
        pool_ext[0:POOL_HALO, :] = meta_pool[...]
        conv_ext[0:CONV_HALO, :] = meta_conv[N_META - CONV_HALO:, :]

    x = jnp.where(is_prompt, xp_ref[...], xs_ref[...])
    h = _rms(x, gmix_ref[...]).astype(_BF16)
    z = jnp.dot(h, win_ref[...], preferred_element_type=_F32)
    u = z[:, :D_POOL]
    cv = z[:, O_CG:O_BG] * z[:, O_V:O_GL]
    wc = wconv_ref[...]

    @pl.when(is_prompt)
    def _():
        pool_ext[POOL_HALO:, :] = u
        ext = pool_ext[...]
        for g, w in enumerate(POOL_WINDOWS):
            lanes = slice(g * POOL_GROUP, (g + 1) * POOL_GROUP)
            acc = ext[:, lanes]
            sh = 1
            while sh < w:
                acc = acc + pltpu.roll(acc, sh, 0)
                sh *= 2
            pm_scr[:, lanes] = acc[POOL_HALO:, :] * (1.0 / w) - u[:, lanes]
        conv_ext[CONV_HALO:, :] = cv
        cext = conv_ext[...]
        conv_scr[...] = (wc[0:1, :] * pltpu.roll(cext, 2, 0)[CONV_HALO:, :]
                         + wc[1:2, :] * pltpu.roll(cext, 1, 0)[CONV_HALO:, :]) + wc[2:3, :] * cv
        pool_ext[0:POOL_HALO, :] = u[MIX_TILE - POOL_HALO:, :]
        conv_ext[0:CONV_HALO, :] = cv[MIX_TILE - CONV_HALO:, :]

    @pl.when(is_prompt & (j == SEQ_TILES - 1))
    def _():
        pool_out_ref[...] = u[MIX_TILE - POOL_HALO:, :]
        conv_out_ref[...] = cv[MIX_TILE - CONV_HALO:, :]

    @pl.when(jnp.logical_not(is_prompt))
    def _():
        us = u[:DEC_BATCH, :]
        cvs = cv[:DEC_BATCH, :]
        pm_scr[...] = jnp.zeros_like(pm_scr)
        conv_scr[...] = jnp.zeros_like(conv_scr)
        for g, w in enumerate(POOL_WINDOWS):
            lanes = slice(g * POOL_GROUP, (g + 1) * POOL_GROUP)
            acc = us[:, lanes]
            for t in range(POOL_BUF - (w - 1), POOL_BUF):
                acc = acc + spool_ref[t, :, lanes]
            pm_scr[0:DEC_BATCH, lanes] = acc * (1.0 / w) - us[:, lanes]
        conv_scr[0:DEC_BATCH, :] = (wc[0:1, :] * sconv_ref[0] + wc[1:2, :] * sconv_ref[1]) + wc[2:3, :] * cvs
        us_ref[...] = us
        cvs_ref[...] = cvs

    branch_a, branch_b = _branches(z, pm_scr[...], conv_scr[...], wgrp_ref, pscale_ref, wup_ref, wcout_ref)
    x1, h2, idxs, gate = _merge_and_route(x, z, branch_a, branch_b, wo_ref, gffn_ref, wrt_ref, br_ref)
    x1_ref[...] = x1
    _store_row_tiles(h2_ref, h2)
    for k in range(TOP_K):
        idx_ref[k:k + 1, :] = idxs[k]
        gate_ref[k:k + 1, :] = gate[k]


def _const_spec(shape):
    nd = len(shape)
    return pl.BlockSpec(shape, lambda *_: (0,) * nd)


def _mixer(x_prompt, xs_pad, spool_t, sconv_t, meta, w):
    last = MIX_PROMPT_STEPS - 1

    def x_map(s):
        sp = jnp.minimum(s, last)
        return (sp // SEQ_TILES, sp % SEQ_TILES, 0)

    def seq_map(s):
        return (jnp.minimum(s, last) // SEQ_TILES, 0, 0)

    consts = (xs_pad, spool_t, sconv_t, meta) + tuple(w)
    return pl.pallas_call(
        _mixer_kernel,
        grid=(MIX_PROMPT_STEPS + 1,),
        in_specs=[pl.BlockSpec((None, MIX_TILE, D_MODEL), x_map)] + [_const_spec(a.shape) for a in consts],
        out_specs=[
            pl.BlockSpec((MIX_TILE, D_MODEL), lambda s: (s, 0)),
            pl.BlockSpec((MIX_TILE * ROW_TILE, LANES), lambda s: (s, 0)),
            pl.BlockSpec((TOP_K, MIX_TILE), lambda s: (0, s)),
            pl.BlockSpec((TOP_K, MIX_TILE), lambda s: (0, s)),
            pl.BlockSpec((None, POOL_HALO, D_POOL), seq_map),
            pl.BlockSpec((None, CONV_HALO, D_CONV), seq_map),
            _const_spec((DEC_BATCH, D_POOL)),
            _const_spec((DEC_BATCH, D_CONV)),
        ],
        out_shape=[
            jax.ShapeDtypeStruct((T_ALL, D_MODEL), _F32),
            jax.ShapeDtypeStruct((T_ALL * ROW_TILE, LANES), _F32),
            jax.ShapeDtypeStruct((TOP_K, T_ALL), jnp.int32),
            jax.ShapeDtypeStruct((TOP_K, T_ALL), _F32),
            jax.ShapeDtypeStruct((BATCH, POOL_HALO, D_POOL), _F32),
            jax.ShapeDtypeStruct((BATCH, CONV_HALO, D_CONV), _F32),
            jax.ShapeDtypeStruct((DEC_BATCH, D_POOL), _F32),
            jax.ShapeDtypeStruct((DEC_BATCH, D_CONV), _F32),
        ],
        scratch_shapes=[
            pltpu.VMEM((POOL_HALO + MIX_TILE, D_POOL), _F32),
            pltpu.VMEM((CONV_HALO + MIX_TILE, D_CONV), _F32),
            pltpu.VMEM((N_META, D_POOL), _F32),
            pltpu.VMEM((N_META, D_CONV), _F32),
            pltpu.VMEM((MIX_TILE, D_POOL), _F32),
            pltpu.VMEM((MIX_TILE, D_CONV), _F32),
        ],
        compiler_params=pltpu.CompilerParams(
            dimension_semantics=("arbitrary",), vmem_limit_bytes=VMEM_LIMIT),
        name="mixer",
    )(x_prompt, *consts)


def _routing_tables(idx_all):
    e_flat = idx_all.reshape(N_ASSIGN)
    order = jnp.argsort(e_flat, stable=True).astype(jnp.int32)
    experts = jnp.arange(N_EXPERTS, dtype=jnp.int32)
    counts = jnp.sum((e_flat[:, None] == experts[None, :]).astype(jnp.int32), axis=0)
    padded = (counts + MOE_BLK - 1) // MOE_BLK * MOE_BLK
    pend = jnp.cumsum(padded)
    pstart = pend - padded
    start = jnp.cumsum(counts) - counts
    block_start = jnp.arange(N_BLOCKS, dtype=jnp.int32) * MOE_BLK
    block_exp = jnp.minimum(jnp.sum((block_start[:, None] >= pend[None, :]).astype(jnp.int32), axis=1),
                            N_EXPERTS - 1)
    r = jnp.arange(N_BLOCKS * MOE_BLK, dtype=jnp.int32)
    e_r = jnp.repeat(block_exp, MOE_BLK)
    jr = r - pstart[e_r]
    valid = jr < counts[e_r]
    a_r = order[jnp.clip(start[e_r] + jr, 0, N_ASSIGN - 1)]
    k_r = a_r // T_ALL
    t_r = a_r - k_r * T_ALL
    src = jnp.where(valid, t_r, 0).astype(jnp.int32)

    def spare_row(q):
        in_gap = q < TOP_K * SLAB_GAP
        gap_row = (q // SLAB_GAP) * T_PAD + T_ALL + q % SLAB_GAP
        return jnp.where(in_gap, gap_row, TAIL0 + q - TOP_K * SLAB_GAP)

    pad_rank = jnp.cumsum((~valid).astype(jnp.int32)) - 1
    dst = jnp.where(valid, k_r * T_PAD + t_r, spare_row(2 * MOE_BLK + pad_rank)).astype(jnp.int32)
    src_rows = jnp.concatenate(
        [src.reshape(N_BLOCKS, MOE_BLK), jnp.zeros((2, MOE_BLK), jnp.int32)], axis=0)
    warmup = spare_row(jnp.arange(2 * MOE_BLK, dtype=jnp.int32)).astype(jnp.int32).reshape(2, MOE_BLK)
    dst_rows = jnp.concatenate([warmup, dst.reshape(N_BLOCKS, MOE_BLK)], axis=0)
    table = jnp.concatenate([src_rows, dst_rows], axis=1) * ROW_TILE
    return block_exp, table


def _moe_kernel(bexp_ref, table_hbm, h2_hbm, wgu_ref, bgu_ref, wdn_ref, bdn_ref, out_hbm,
                xbuf0, xbuf1, ybuf0, ybuf1, tbl0, tbl1, gsem, ssem, tsem):
    del bexp_ref
    b = pl.program_id(0)
    nb = pl.num_programs(0)
    xbufs, ybufs, tbls = (xbuf0, xbuf1), (ybuf0, ybuf1), (tbl0, tbl1)

    def table_copy(row, p):
        return pltpu.make_async_copy(table_hbm.at[row], tbls[p], tsem.at[p])

    def gather_row(off, r, p):
        return pltpu.make_async_copy(h2_hbm.at[pl.ds(off, ROW_TILE), :],
                                     xbufs[p].at[pl.ds(r * ROW_TILE, ROW_TILE), :], gsem.at[p])

    def scatter_row(off, r, p):
        return pltpu.make_async_copy(ybufs[p].at[pl.ds(r * ROW_TILE, ROW_TILE), :],
                                     out_hbm.at[pl.ds(off, ROW_TILE), :], ssem.at[p])

    def start_gather(tp, p):
        for r in range(MOE_BLK):
            gather_row(pl.multiple_of(tbls[tp][r], ROW_TILE), r, p).start(priority=r % 2)

    def start_scatter(tp, p):
        for r in range(MOE_BLK):
            scatter_row(pl.multiple_of(tbls[tp][MOE_BLK + r], ROW_TILE), r, p).start(priority=r % 2)

    def wait_gather(p):
        for r in range(MOE_BLK):
            gather_row(0, r, p).wait()

    def wait_scatter(p):
        for r in range(MOE_BLK):
            scatter_row(0, r, p).wait()

    @pl.when(b == 0)
    def _():
        first = table_copy(0, 0)
        first.start()
        first.wait()
        ybuf0[...] = jnp.zeros_like(ybuf0)
        ybuf1[...] = jnp.zeros_like(ybuf1)
        start_gather(0, 0)
        start_scatter(0, 0)
        table_copy(1, 1).start()

    def step(p):
        q = 1 - p
        table_copy(b + 1, q).wait()
        wait_gather(p)
        start_gather(q, q)
        start_scatter(q, q)
        table_copy(b + 2, p).start()

        x = _load_row_tiles(xbufs[p], MOE_BLK).astype(_BF16)
        hgu = jnp.dot(x, wgu_ref[...], preferred_element_type=_F32) + bgu_ref[...]
        gate = jnp.minimum(hgu[:, :D_FF], SWIGLU_LIMIT)
        up = jnp.clip(hgu[:, D_FF:], -SWIGLU_LIMIT, SWIGLU_LIMIT)
        act = gate * jax.nn.sigmoid(SWIGLU_ALPHA * gate) * (up + 1.0)
        y = jnp.dot(act.astype(_BF16), wdn_ref[...], preferred_element_type=_F32) + bdn_ref[...]

        wait_scatter(p)
        _store_row_tiles(ybufs[p], y)

        @pl.when(b == nb - 1)
        def _():
            table_copy(b + 2, p).wait()
            start_scatter(p, p)
            wait_scatter(p)
            wait_scatter(q)
            wait_gather(q)

    parity = lax.rem(b, 2)
    pl.when(parity == 0)(lambda: step(0))
    pl.when(parity == 1)(lambda: step(1))


def _moe_experts(block_exp, table, h2_all, wgu, bgu, wdn, bdn):
    grid_spec = pltpu.PrefetchScalarGridSpec(
        num_scalar_prefetch=1,
        grid=(N_BLOCKS,),
        in_specs=[
            pl.BlockSpec(memory_space=pl.ANY),
            pl.BlockSpec(memory_space=pl.ANY),
            pl.BlockSpec((None, D_MODEL, 2 * D_FF), lambda b, be: (be[b], 0, 0)),
            pl.BlockSpec((None, 1, 2 * D_FF), lambda b, be: (be[b], 0, 0)),
            pl.BlockSpec((None, D_FF, D_MODEL), lambda b, be: (be[b], 0, 0)),
            pl.BlockSpec((None, 1, D_MODEL), lambda b, be: (be[b], 0, 0)),
        ],
        out_specs=pl.BlockSpec(memory_space=pl.ANY),
        scratch_shapes=[
            pltpu.VMEM((MOE_BLK * ROW_TILE, LANES), _F32),
            pltpu.VMEM((MOE_BLK * ROW_TILE, LANES), _F32),
            pltpu.VMEM((MOE_BLK * ROW_TILE, LANES), _F32),
            pltpu.VMEM((MOE_BLK * ROW_TILE, LANES), _F32),
            pltpu.SMEM((2 * MOE_BLK,), jnp.int32),
            pltpu.SMEM((2 * MOE_BLK,), jnp.int32),
            pltpu.SemaphoreType.DMA((2,)),
            pltpu.SemaphoreType.DMA((2,)),
            pltpu.SemaphoreType.DMA((2,)),
        ],
    )
    return pl.pallas_call(
        _moe_kernel,
        grid_spec=grid_spec,
        out_shape=jax.ShapeDtypeStruct((OUT_ROWS * ROW_TILE, LANES), _F32),
        compiler_params=pltpu.CompilerParams(
            dimension_semantics=("arbitrary",), vmem_limit_bytes=VMEM_LIMIT),
        name="moe_experts",
    )(block_exp, table, h2_all, wgu, bgu, wdn, bdn)


def _combine_kernel(x1_ref, g_ref, o0_ref, o1_ref, o2_ref, o3_ref, gfin_ref, yp_ref, ys_ref):
    i = pl.program_id(0)
    g = g_ref[...]
    o = [_load_row_tiles(r, COMB_TILE) for r in (o0_ref, o1_ref, o2_ref, o3_ref)]
    moe = (g[:, 0:1] * o[0] + g[:, 1:2] * o[1]) + (g[:, 2:3] * o[2] + g[:, 3:4] * o[3])
    y = _rms(x1_ref[...] + moe, gfin_ref[...])

    @pl.when(i < COMB_PROMPT_STEPS)
    def _():
        yp_ref[...] = y

    @pl.when(i == COMB_PROMPT_STEPS)
    def _():
        ys_ref[...] = y[:DEC_BATCH, :]


def _combine(x1, gates_tk, out4, g_final):
    slab_blocks = T_PAD // COMB_TILE

    def slab_spec(k):
        return pl.BlockSpec((COMB_TILE * ROW_TILE, LANES), lambda i, k=k: (k * slab_blocks + i, 0))

    return pl.pallas_call(
        _combine_kernel,
        grid=(COMB_PROMPT_STEPS + 1,),
        in_specs=[pl.BlockSpec((COMB_TILE, D_MODEL), lambda i: (i, 0)),
                  pl.BlockSpec((COMB_TILE, TOP_K), lambda i: (i, 0)),
                  slab_spec(0), slab_spec(1), slab_spec(2), slab_spec(3),
                  _const_spec(g_final.shape)],
        out_specs=[pl.BlockSpec((COMB_TILE, D_MODEL), lambda i: (jnp.minimum(i, COMB_PROMPT_STEPS - 1), 0)),
                   _const_spec((DEC_BATCH, D_MODEL))],
        out_shape=[jax.ShapeDtypeStruct((T_PROMPT, D_MODEL), _F32),
                   jax.ShapeDtypeStruct((DEC_BATCH, D_MODEL), _F32)],
        compiler_params=pltpu.CompilerParams(
            dimension_semantics=("arbitrary",), vmem_limit_bytes=VMEM_LIMIT),
        name="combine",
    )(x1, gates_tk, out4, out4, out4, out4, g_final)


def kernel(x_prompt, x_sample, state_pool, state_conv, meta_tokens, g_mix, w_in, w_pool_grp, pool_scale,
           w_pool_up, w_conv, w_conv_out, w_o, g_ffn, w_router, b_router, w_gu, b_gu, w_down, b_down, g_final):
    assert g_mix.shape[0] == 1, "single-layer step"
    grp = w_pool_grp[0].astype(_BF16)
    zero = jnp.zeros((POOL_GROUP, POOL_GROUP), _BF16)
    wgrp = jnp.stack([jnp.block([[grp[0], zero], [zero, grp[1]]]),
                      jnp.block([[grp[2], zero], [zero, grp[3]]])])
    w = (
        g_mix[0].reshape(1, D_MODEL),
        w_in[0].astype(_BF16),
        wgrp,
        pool_scale[0].reshape(1, D_POOL),
        w_pool_up[0].astype(_BF16),
        w_conv[0],
        w_conv_out[0].astype(_BF16),
        w_o[0].astype(_BF16),
        g_ffn[0].reshape(1, D_MODEL),
        w_router[0].T,
        b_router[0].reshape(N_EXPERTS, 1),
    )
    xs_pad = jnp.pad(x_sample.reshape(DEC_BATCH, D_MODEL), ((0, MIX_TILE - DEC_BATCH), (0, 0)))
    spool_t = jnp.transpose(state_pool[0], (1, 0, 2))
    sconv_t = jnp.transpose(state_conv[0], (1, 0, 2))
    x1, h2, idx, gate, pool_p, conv_p, u_s, cv_s = _mixer(x_prompt, xs_pad, spool_t, sconv_t, meta_tokens, w)

    block_exp, table = _routing_tables(idx)
    out4 = _moe_experts(block_exp, table, h2,
                        w_gu[0].astype(_BF16), b_gu[0].reshape(N_EXPERTS, 1, 2 * D_FF),
                        w_down[0].astype(_BF16), b_down[0].reshape(N_EXPERTS, 1, D_MODEL))

    y_p, y_s = _combine(x1, gate.T, out4, g_final.reshape(1, D_MODEL))

    new_pool_p = pool_p[:, POOL_HALO - POOL_BUF:, :][None]
    new_conv_p = conv_p[:, CONV_HALO - CONV_BUF:, :][None]
    new_pool_s = jnp.concatenate([state_pool[0][:, 1:, :], u_s[:, None, :]], axis=1)[None]
    new_conv_s = jnp.concatenate([state_conv[0][:, 1:, :], cv_s[:, None, :]], axis=1)[None]
    return (y_p.reshape(BATCH, SEQ, D_MODEL), y_s.reshape(DEC_BATCH, 1, D_MODEL),
            new_pool_p, new_conv_p, new_pool_s, new_conv_s)
```

```python
import jax
import jax.numpy as jnp
from jax import lax
from jax.experimental import pallas as pl
from jax.experimental.pallas import tpu as pltpu

D_MODEL = 1024
BATCH = 8
SEQ = 2048
DEC_BATCH = 128
N_META = 16
D_POOL = 512
POOL_GROUP = 128
POOL_WINDOWS = (2, 4, 8, 16)
POOL_BUF = 15
D_CONV = 512
CONV_BUF = 2
N_EXPERTS = 32
TOP_K = 4
D_FF = 1024
SWIGLU_LIMIT = 7.0
SWIGLU_ALPHA = 1.702
RMS_EPS = 1e-5

O_CG = D_POOL
O_BG = O_CG + D_CONV
O_V = O_BG + D_CONV
O_GL = O_V + D_CONV

T_PROMPT = BATCH * SEQ
T_ALL = T_PROMPT + DEC_BATCH
N_ASSIGN = TOP_K * T_ALL

MIX_TILE = 256
MIX_PROMPT_STEPS = T_PROMPT // MIX_TILE
SEQ_TILES = SEQ // MIX_TILE
POOL_HALO = 16
CONV_HALO = 8
DISP_STEPS = MIX_PROMPT_STEPS + 1
MOE_BLK = 256
N_BLOCKS = -(-N_ASSIGN // MOE_BLK) + N_EXPERTS
N_PAD_ROWS = N_BLOCKS * MOE_BLK - N_ASSIGN
PAD_CHUNK = 64
assert N_PAD_ROWS % PAD_CHUNK == 0
COMB_TILE = 512
COMB_PROMPT_STEPS = T_PROMPT // COMB_TILE
T_PAD = (T_ALL + COMB_TILE - 1) // COMB_TILE * COMB_TILE
SLAB_GAP = T_PAD - T_ALL
OUT_ROWS = (N_BLOCKS + 2) * MOE_BLK
TAIL0 = TOP_K * T_PAD
assert OUT_ROWS >= TAIL0

ROW_TILE = 8
LANES = D_MODEL // ROW_TILE

VMEM_LIMIT = 56 * 1024 * 1024

_F32 = jnp.float32
_BF16 = jnp.bfloat16


def _rms(x, g):
    ms = jnp.mean(x * x, axis=-1, keepdims=True)
    return x * lax.rsqrt(ms + RMS_EPS) * g


def _bdot(a, w):
    return jnp.dot(a.astype(_BF16), w, preferred_element_type=_F32)


def _store_row_tiles(ref, val):
    rows = val.shape[0]
    for c in range(ROW_TILE):
        ref[pl.ds(c, rows, stride=ROW_TILE), :] = val[:, c * LANES:(c + 1) * LANES]


def _load_row_tiles(ref, rows):
    return jnp.concatenate([ref[pl.ds(c, rows, stride=ROW_TILE), :] for c in range(ROW_TILE)], axis=-1)


def _branches(z, pm, conv, wgrp_ref, pscale_ref, wup_ref, wcout_ref):
    pmb = pm.astype(_BF16)
    half = 2 * POOL_GROUP
    pg = jnp.concatenate(
        [jnp.dot(pmb[:, :half], wgrp_ref[0], preferred_element_type=_F32),
         jnp.dot(pmb[:, half:], wgrp_ref[1], preferred_element_type=_F32)], axis=-1)
    branch_a = _bdot(pg * pscale_ref[...], wup_ref[...])
    branch_b = _bdot(z[:, O_BG:O_V] * conv, wcout_ref[...])
    return branch_a, branch_b


def _merge_and_route(x, z, branch_a, branch_b, wo_ref, gffn_ref, wrt_ref, br_ref, base_ref, n_live):
    gates = jax.nn.sigmoid(z[:, O_GL:])
    merged = gates[:, :D_MODEL] * branch_a + gates[:, D_MODEL:] * branch_b
    x1 = x + _bdot(merged, wo_ref[...])
    h2 = _rms(x1, gffn_ref[...])
    logits = lax.dot_general(wrt_ref[...], h2, (((1,), (1,)), ((), ())),
                             precision=lax.Precision.HIGHEST,
                             preferred_element_type=_F32) + br_ref[...]
    iota = lax.broadcasted_iota(jnp.int32, logits.shape, 0)
    vals, idxs = [], []
    cur = logits
    for _ in range(TOP_K):
        m = jnp.max(cur, axis=0, keepdims=True)
        ik = jnp.min(jnp.where(cur == m, iota, N_EXPERTS), axis=0, keepdims=True)
        vals.append(m)
        idxs.append(ik)
        cur = jnp.where(iota == ik, -jnp.inf, cur)
    exps = [jnp.exp(v - vals[0]) for v in vals]
    denom = (exps[0] + exps[1]) + (exps[2] + exps[3])
    gate = [e / denom for e in exps]
    rows = logits.shape[1]
    live = lax.broadcasted_iota(jnp.int32, logits.shape, 1) < n_live
    onehots = [jnp.where(live, jnp.where(iota == ik, 1.0, 0.0), 0.0) for ik in idxs]
    member = (onehots[0] + onehots[1]) + (onehots[2] + onehots[3])
    earlier = (lax.broadcasted_iota(jnp.int32, (rows, rows), 0)
               < lax.broadcasted_iota(jnp.int32, (rows, rows), 1))
    before = jnp.dot(member.astype(_BF16), jnp.where(earlier, 1.0, 0.0).astype(_BF16),
                     preferred_element_type=_F32)
    pos = base_ref[...] + before
    ranks = [jnp.sum(oh * pos, axis=0, keepdims=True).astype(jnp.int32) for oh in onehots]
    base_ref[...] = base_ref[...] + jnp.sum(member, axis=1, keepdims=True)
    return x1, h2, idxs, gate, ranks


def _mixer_kernel(xp_ref, xs_ref, spool_ref, sconv_ref, meta_ref, gmix_ref, win_ref, wgrp_ref, pscale_ref,
                  wup_ref, wconv_ref, wcout_ref, wo_ref, gffn_ref, wrt_ref, br_ref,
                  x1_ref, h2_ref, idx_ref, gate_ref, rank_ref, cnt_ref, pool_out_ref, conv_out_ref, us_ref, cvs_ref,
                  pool_ext, conv_ext, meta_pool, meta_conv, pm_scr, conv_scr, base_scr):
    s = pl.program_id(0)
    is_prompt = s < MIX_PROMPT_STEPS
    j = lax.rem(s, SEQ_TILES)

    @pl.when(s == 0)
    def _():
        hm = _rms(meta_ref[...], gmix_ref[...]).astype(_BF16)
        zm = jnp.dot(hm, win_ref[:, :O_GL], preferred_element_type=_F32)
        meta_pool[...] = zm[:, :D_POOL]
        meta_conv[...] = zm[:, O_CG:O_BG] * zm[:, O_V:O_GL]
        base_scr[...] = jnp.zeros_like(base_scr)

    @pl.when(is_prompt & (j == 0))
    def _():
        pool_ext[0:POOL_HALO, :] = meta_pool[...]
        conv_ext[0:CONV_HALO, :] = meta_conv[N_META - CONV_HALO:, :]

    x = jnp.where(is_prompt, xp_ref[...], xs_ref[...])
    h = _rms(x, gmix_ref[...]).astype(_BF16)
    z = jnp.dot(h, win_ref[...], preferred_element_type=_F32)
    u = z[:, :D_POOL]
    cv = z[:, O_CG:O_BG] * z[:, O_V:O_GL]
    wc = wconv_ref[...]

    @pl.when(is_prompt)
    def _():
        pool_ext[POOL_HALO:, :] = u
        ext = pool_ext[...]
        for g, w in enumerate(POOL_WINDOWS):
            lanes = slice(g * POOL_GROUP, (g + 1) * POOL_GROUP)
            acc = ext[:, lanes]
            sh = 1
            while sh < w:
                acc = acc + pltpu.roll(acc, sh, 0)
                sh *= 2
            pm_scr[:, lanes] = acc[POOL_HALO:, :] * (1.0 / w) - u[:, lanes]
        conv_ext[CONV_HALO:, :] = cv
        cext = conv_ext[...]
        conv_scr[...] = (wc[0:1, :] * pltpu.roll(cext, 2, 0)[CONV_HALO:, :]
                         + wc[1:2, :] * pltpu.roll(cext, 1, 0)[CONV_HALO:, :]) + wc[2:3, :] * cv
        pool_ext[0:POOL_HALO, :] = u[MIX_TILE - POOL_HALO:, :]
        conv_ext[0:CONV_HALO, :] = cv[MIX_TILE - CONV_HALO:, :]

    @pl.when(is_prompt & (j == SEQ_TILES - 1))
    def _():
        pool_out_ref[...] = u[MIX_TILE - POOL_HALO:, :]
        conv_out_ref[...] = cv[MIX_TILE - CONV_HALO:, :]

    @pl.when(jnp.logical_not(is_prompt))
    def _():
        us = u[:DEC_BATCH, :]
        cvs = cv[:DEC_BATCH, :]
        pm_scr[...] = jnp.zeros_like(pm_scr)
        conv_scr[...] = jnp.zeros_like(conv_scr)
        for g, w in enumerate(POOL_WINDOWS):
            lanes = slice(g * POOL_GROUP, (g + 1) * POOL_GROUP)
            acc = us[:, lanes]
            for t in range(POOL_BUF - (w - 1), POOL_BUF):
                acc = acc + spool_ref[t, :, lanes]
            pm_scr[0:DEC_BATCH, lanes] = acc * (1.0 / w) - us[:, lanes]
        conv_scr[0:DEC_BATCH, :] = (wc[0:1, :] * sconv_ref[0] + wc[1:2, :] * sconv_ref[1]) + wc[2:3, :] * cvs
        us_ref[...] = us
        cvs_ref[...] = cvs

    branch_a, branch_b = _branches(z, pm_scr[...], conv_scr[...], wgrp_ref, pscale_ref, wup_ref, wcout_ref)
    n_live = jnp.where(is_prompt, MIX_TILE, DEC_BATCH)
    x1, h2, idxs, gate, ranks = _merge_and_route(x, z, branch_a, branch_b, wo_ref, gffn_ref, wrt_ref, br_ref,
                                                 base_scr, n_live)
    x1_ref[...] = x1
    _store_row_tiles(h2_ref, h2)
    for k in range(TOP_K):
        idx_ref[k:k + 1, :] = idxs[k]
        gate_ref[k:k + 1, :] = gate[k]
        rank_ref[k:k + 1, :] = ranks[k]

    @pl.when(jnp.logical_not(is_prompt))
    def _():
        cnt_ref[...] = jnp.broadcast_to(base_scr[...], cnt_ref.shape)


def _const_spec(shape):
    nd = len(shape)
    return pl.BlockSpec(shape, lambda *_: (0,) * nd)


def _mixer(x_prompt, xs_pad, spool_t, sconv_t, meta, w):
    last = MIX_PROMPT_STEPS - 1

    def x_map(s):
        sp = jnp.minimum(s, last)
        return (sp // SEQ_TILES, sp % SEQ_TILES, 0)

    def seq_map(s):
        return (jnp.minimum(s, last) // SEQ_TILES, 0, 0)

    consts = (xs_pad, spool_t, sconv_t, meta) + tuple(w)
    return pl.pallas_call(
        _mixer_kernel,
        grid=(MIX_PROMPT_STEPS + 1,),
        in_specs=[pl.BlockSpec((None, MIX_TILE, D_MODEL), x_map)] + [_const_spec(a.shape) for a in consts],
        out_specs=[
            pl.BlockSpec((MIX_TILE, D_MODEL), lambda s: (s, 0)),
            pl.BlockSpec((MIX_TILE * ROW_TILE, LANES), lambda s: (s, 0)),
            pl.BlockSpec((TOP_K, MIX_TILE), lambda s: (0, s)),
            pl.BlockSpec((TOP_K, MIX_TILE), lambda s: (0, s)),
            pl.BlockSpec((TOP_K, MIX_TILE), lambda s: (0, s)),
            _const_spec((N_EXPERTS, 128)),
            pl.BlockSpec((None, POOL_HALO, D_POOL), seq_map),
            pl.BlockSpec((None, CONV_HALO, D_CONV), seq_map),
            _const_spec((DEC_BATCH, D_POOL)),
            _const_spec((DEC_BATCH, D_CONV)),
        ],
        out_shape=[
            jax.ShapeDtypeStruct((T_ALL, D_MODEL), _F32),
            jax.ShapeDtypeStruct((T_ALL * ROW_TILE, LANES), _F32),
            jax.ShapeDtypeStruct((TOP_K, T_ALL), jnp.int32),
            jax.ShapeDtypeStruct((TOP_K, T_ALL), _F32),
            jax.ShapeDtypeStruct((TOP_K, T_ALL), jnp.int32),
            jax.ShapeDtypeStruct((N_EXPERTS, 128), _F32),
            jax.ShapeDtypeStruct((BATCH, POOL_HALO, D_POOL), _F32),
            jax.ShapeDtypeStruct((BATCH, CONV_HALO, D_CONV), _F32),
            jax.ShapeDtypeStruct((DEC_BATCH, D_POOL), _F32),
            jax.ShapeDtypeStruct((DEC_BATCH, D_CONV), _F32),
        ],
        scratch_shapes=[
            pltpu.VMEM((POOL_HALO + MIX_TILE, D_POOL), _F32),
            pltpu.VMEM((CONV_HALO + MIX_TILE, D_CONV), _F32),
            pltpu.VMEM((N_META, D_POOL), _F32),
            pltpu.VMEM((N_META, D_CONV), _F32),
            pltpu.VMEM((MIX_TILE, D_POOL), _F32),
            pltpu.VMEM((MIX_TILE, D_CONV), _F32),
            pltpu.VMEM((N_EXPERTS, 1), _F32),
        ],
        compiler_params=pltpu.CompilerParams(
            dimension_semantics=("arbitrary",), vmem_limit_bytes=VMEM_LIMIT),
        name="mixer",
    )(x_prompt, *consts)


def _routing_tables(idx, rank, cnt):
    counts = cnt[:, 0].astype(jnp.int32)
    padded = (counts + MOE_BLK - 1) // MOE_BLK * MOE_BLK
    pend = jnp.cumsum(padded)
    pstart = pend - padded
    start = jnp.cumsum(counts) - counts

    dest = pstart[idx] + rank
    dest = jnp.pad(dest, ((0, 0), (0, DISP_STEPS * MIX_TILE - T_ALL)))
    disp = dest.reshape(TOP_K, DISP_STEPS, MIX_TILE).transpose(1, 0, 2).reshape(DISP_STEPS, TOP_K * MIX_TILE)

    n_pad = padded - counts
    pad_cum = jnp.cumsum(n_pad)
    q = jnp.arange(N_PAD_ROWS, dtype=jnp.int32)
    e_q = jnp.sum((q[:, None] >= pad_cum[None, :]).astype(jnp.int32), axis=1)
    first_pad = jnp.concatenate([pstart + counts, pend[-1:]])[e_q]
    pads_before = jnp.concatenate([pad_cum - n_pad, pad_cum[-1:]])[e_q]
    pad_rows = first_pad + q - pads_before

    e_flat = idx.T.reshape(N_ASSIGN)
    order = jnp.argsort(e_flat, stable=True).astype(jnp.int32)
    block_start = jnp.arange(N_BLOCKS, dtype=jnp.int32) * MOE_BLK
    block_exp = jnp.minimum(jnp.sum((block_start[:, None] >= pend[None, :]).astype(jnp.int32), axis=1),
                            N_EXPERTS - 1)
    r = jnp.arange(N_BLOCKS * MOE_BLK, dtype=jnp.int32)
    e_r = jnp.repeat(block_exp, MOE_BLK)
    jr = r - pstart[e_r]
    valid = jr < counts[e_r]
    a_r = order[jnp.clip(start[e_r] + jr, 0, N_ASSIGN - 1)]
    t_r = a_r // TOP_K
    k_r = a_r - t_r * TOP_K

    def spare_row(q):
        in_gap = q < TOP_K * SLAB_GAP
        gap_row = (q // SLAB_GAP) * T_PAD + T_ALL + q % SLAB_GAP
        return jnp.where(in_gap, gap_row, TAIL0 + q - TOP_K * SLAB_GAP)

    pad_rank = jnp.cumsum((~valid).astype(jnp.int32)) - 1
    dst = jnp.where(valid, k_r * T_PAD + t_r, spare_row(2 * MOE_BLK + pad_rank)).astype(jnp.int32)
    warmup = spare_row(jnp.arange(2 * MOE_BLK, dtype=jnp.int32)).astype(jnp.int32).reshape(2, MOE_BLK)
    dst_rows = jnp.concatenate([warmup, dst.reshape(N_BLOCKS, MOE_BLK)], axis=0)
    return block_exp, disp * ROW_TILE, pad_rows * ROW_TILE, dst_rows * ROW_TILE


def _dispatch_kernel(disp_hbm, pads_hbm, h2_ref, xs_hbm, tbl0, tbl1, pad_tbl, zero_tile, rsem, tsem, psem):
    s = pl.program_id(0)
    ns = pl.num_programs(0)
    tbls = (tbl0, tbl1)

    def table_copy(row, p):
        return pltpu.make_async_copy(disp_hbm.at[row], tbls[p], tsem.at[p])

    def row_copy(off, r, k):
        return pltpu.make_async_copy(h2_ref.at[pl.ds(r * ROW_TILE, ROW_TILE), :],
                                     xs_hbm.at[pl.ds(off, ROW_TILE), :], rsem.at[k])

    def pad_copy(off):
        return pltpu.make_async_copy(zero_tile, xs_hbm.at[pl.ds(off, ROW_TILE), :], psem.at[0])

    def scatter_rows(p, n_rows):
        for r in range(n_rows):
            for k in range(TOP_K):
                off = pl.multiple_of(tbls[p][k * MIX_TILE + r], ROW_TILE)
                row_copy(off, r, k).start(priority=(r + k) % 2)
        for r in range(n_rows):
            for k in range(TOP_K):
                row_copy(0, r, k).wait()

    def fill_pads():
        zero_tile[...] = jnp.zeros_like(zero_tile)
        load = pltpu.make_async_copy(pads_hbm, pad_tbl, psem.at[1])
        load.start()
        load.wait()

        def chunk(i, carry):
            for j in range(PAD_CHUNK):
                pad_copy(pl.multiple_of(pad_tbl[i * PAD_CHUNK + j], ROW_TILE)).start(priority=j % 2)

            @pl.when(i > 0)
            def _():
                for j in range(PAD_CHUNK):
                    pad_copy(0).wait()
            return carry

        lax.fori_loop(0, N_PAD_ROWS // PAD_CHUNK, chunk, 0)
        for j in range(PAD_CHUNK):
            pad_copy(0).wait()

    @pl.when(s == 0)
    def _():
        table_copy(0, 0).start()

    def step(p):
        table_copy(s, p).wait()

        @pl.when(s < ns - 1)
        def _():
            table_copy(s + 1, 1 - p).start()
            scatter_rows(p, MIX_TILE)

        @pl.when(s == ns - 1)
        def _():
            scatter_rows(p, DEC_BATCH)
            fill_pads()

    parity = lax.rem(s, 2)
    pl.when(parity == 0)(lambda: step(0))
    pl.when(parity == 1)(lambda: step(1))


def _dispatch(disp, pad_rows, h2):
    return pl.pallas_call(
        _dispatch_kernel,
        grid=(DISP_STEPS,),
        in_specs=[pl.BlockSpec(memory_space=pl.ANY),
                  pl.BlockSpec(memory_space=pl.ANY),
                  pl.BlockSpec((MIX_TILE * ROW_TILE, LANES), lambda s: (s, 0))],
        out_specs=pl.BlockSpec(memory_space=pl.ANY),
        out_shape=jax.ShapeDtypeStruct((N_BLOCKS * MOE_BLK * ROW_TILE, LANES), _F32),
        scratch_shapes=[
            pltpu.SMEM((TOP_K * MIX_TILE,), jnp.int32),
            pltpu.SMEM((TOP_K * MIX_TILE,), jnp.int32),
            pltpu.SMEM((N_PAD_ROWS,), jnp.int32),
            pltpu.VMEM((ROW_TILE, LANES), _F32),
            pltpu.SemaphoreType.DMA((TOP_K,)),
            pltpu.SemaphoreType.DMA((2,)),
            pltpu.SemaphoreType.DMA((2,)),
        ],
        compiler_params=pltpu.CompilerParams(
            dimension_semantics=("arbitrary",), vmem_limit_bytes=VMEM_LIMIT),
        name="dispatch",
    )(disp, pad_rows, h2)


def _moe_kernel(bexp_ref, table_hbm, x_ref, wgu_ref, bgu_ref, wdn_ref, bdn_ref, out_hbm,
                ybuf0, ybuf1, tbl0, tbl1, ssem, tsem):
    del bexp_ref
    b = pl.program_id(0)
    nb = pl.num_programs(0)
    ybufs, tbls = (ybuf0, ybuf1), (tbl0, tbl1)

    def table_copy(row, p):
        return pltpu.make_async_copy(table_hbm.at[row], tbls[p], tsem.at[p])

    def scatter_row(off, r, p):
        return pltpu.make_async_copy(ybufs[p].at[pl.ds(r * ROW_TILE, ROW_TILE), :],
                                     out_hbm.at[pl.ds(off, ROW_TILE), :], ssem.at[p])

    def start_scatter(tp, p):
        for r in range(MOE_BLK):
            scatter_row(pl.multiple_of(tbls[tp][r], ROW_TILE), r, p).start(priority=r % 2)

    def wait_scatter(p):
        for r in range(MOE_BLK):
            scatter_row(0, r, p).wait()

    @pl.when(b == 0)
    def _():
        first = table_copy(0, 0)
        first.start()
        first.wait()
        ybuf0[...] = jnp.zeros_like(ybuf0)
        ybuf1[...] = jnp.zeros_like(ybuf1)
        start_scatter(0, 0)
        table_copy(1, 1).start()

    def step(p):
        q = 1 - p
        table_copy(b + 1, q).wait()
        start_scatter(q, q)
        table_copy(b + 2, p).start()

        x = _load_row_tiles(x_ref, MOE_BLK).astype(_BF16)
        hgu = jnp.dot(x, wgu_ref[...], preferred_element_type=_F32) + bgu_ref[...]
        gate = jnp.minimum(hgu[:, :D_FF], SWIGLU_LIMIT)
        up = jnp.clip(hgu[:, D_FF:], -SWIGLU_LIMIT, SWIGLU_LIMIT)
        act = gate * jax.nn.sigmoid(SWIGLU_ALPHA * gate) * (up + 1.0)
        y = jnp.dot(act.astype(_BF16), wdn_ref[...], preferred_element_type=_F32) + bdn_ref[...]

        wait_scatter(p)
        _store_row_tiles(ybufs[p], y)

        @pl.when(b == nb - 1)
        def _():
            table_copy(b + 2, p).wait()
            start_scatter(p, p)
            wait_scatter(p)
            wait_scatter(q)

    parity = lax.rem(b, 2)
    pl.when(parity == 0)(lambda: step(0))
    pl.when(parity == 1)(lambda: step(1))


def _moe_experts(block_exp, table, xs, wgu, bgu, wdn, bdn):
    grid_spec = pltpu.PrefetchScalarGridSpec(
        num_scalar_prefetch=1,
        grid=(N_BLOCKS,),
        in_specs=[
            pl.BlockSpec(memory_space=pl.ANY),
            pl.BlockSpec((MOE_BLK * ROW_TILE, LANES), lambda b, be: (b, 0)),
            pl.BlockSpec((None, D_MODEL, 2 * D_FF), lambda b, be: (be[b], 0, 0)),
            pl.BlockSpec((None, 1, 2 * D_FF), lambda b, be: (be[b], 0, 0)),
            pl.BlockSpec((None, D_FF, D_MODEL), lambda b, be: (be[b], 0, 0)),
            pl.BlockSpec((None, 1, D_MODEL), lambda b, be: (be[b], 0, 0)),
        ],
        out_specs=pl.BlockSpec(memory_space=pl.ANY),
        scratch_shapes=[
            pltpu.VMEM((MOE_BLK * ROW_TILE, LANES), _F32),
            pltpu.VMEM((MOE_BLK * ROW_TILE, LANES), _F32),
            pltpu.SMEM((MOE_BLK,), jnp.int32),
            pltpu.SMEM((MOE_BLK,), jnp.int32),
            pltpu.SemaphoreType.DMA((2,)),
            pltpu.SemaphoreType.DMA((2,)),
        ],
    )
    return pl.pallas_call(
        _moe_kernel,
        grid_spec=grid_spec,
        out_shape=jax.ShapeDtypeStruct((OUT_ROWS * ROW_TILE, LANES), _F32),
        compiler_params=pltpu.CompilerParams(
            dimension_semantics=("arbitrary",), vmem_limit_bytes=VMEM_LIMIT),
        name="moe_experts",
    )(block_exp, table, xs, wgu, bgu, wdn, bdn)


def _combine_kernel(x1_ref, g_ref, o0_ref, o1_ref, o2_ref, o3_ref, gfin_ref, yp_ref, ys_ref):
    i = pl.program_id(0)
    g = g_ref[...]
    o = [_load_row_tiles(r, COMB_TILE) for r in (o0_ref, o1_ref, o2_ref, o3_ref)]
    moe = (g[:, 0:1] * o[0] + g[:, 1:2] * o[1]) + (g[:, 2:3] * o[2] + g[:, 3:4] * o[3])
    y = _rms(x1_ref[...] + moe, gfin_ref[...])

    @pl.when(i < COMB_PROMPT_STEPS)
    def _():
        yp_ref[...] = y

    @pl.when(i == COMB_PROMPT_STEPS)
    def _():
        ys_ref[...] = y[:DEC_BATCH, :]


def _combine(x1, gates_tk, out4, g_final):
    slab_blocks = T_PAD // COMB_TILE

    def slab_spec(k):
        return pl.BlockSpec((COMB_TILE * ROW_TILE, LANES), lambda i, k=k: (k * slab_blocks + i, 0))

    return pl.pallas_call(
        _combine_kernel,
        grid=(COMB_PROMPT_STEPS + 1,),
        in_specs=[pl.BlockSpec((COMB_TILE, D_MODEL), lambda i: (i, 0)),
                  pl.BlockSpec((COMB_TILE, TOP_K), lambda i: (i, 0)),
                  slab_spec(0), slab_spec(1), slab_spec(2), slab_spec(3),
                  _const_spec(g_final.shape)],
        out_specs=[pl.BlockSpec((COMB_TILE, D_MODEL), lambda i: (jnp.minimum(i, COMB_PROMPT_STEPS - 1), 0)),
                   _const_spec((DEC_BATCH, D_MODEL))],
        out_shape=[jax.ShapeDtypeStruct((T_PROMPT, D_MODEL), _F32),
                   jax.ShapeDtypeStruct((DEC_BATCH, D_MODEL), _F32)],
        compiler_params=pltpu.CompilerParams(
            dimension_semantics=("arbitrary",), vmem_limit_bytes=VMEM_LIMIT),
        name="combine",
    )(x1, gates_tk, out4, out4, out4, out4, g_final)


def kernel(x_prompt, x_sample, state_pool, state_conv, meta_tokens, g_mix, w_in, w_pool_grp, pool_scale,
           w_pool_up, w_conv, w_conv_out, w_o, g_ffn, w_router, b_router, w_gu, b_gu, w_down, b_down, g_final):
    assert g_mix.shape[0] == 1, "single-layer step"
    grp = w_pool_grp[0].astype(_BF16)
    zero = jnp.zeros((POOL_GROUP, POOL_GROUP), _BF16)
    wgrp = jnp.stack([jnp.block([[grp[0], zero], [zero, grp[1]]]),
                      jnp.block([[grp[2], zero], [zero, grp[3]]])])
    w = (
        g_mix[0].reshape(1, D_MODEL),
        w_in[0].astype(_BF16),
        wgrp,
        pool_scale[0].reshape(1, D_POOL),
        w_pool_up[0].astype(_BF16),
        w_conv[0],
        w_conv_out[0].astype(_BF16),
        w_o[0].astype(_BF16),
        g_ffn[0].reshape(1, D_MODEL),
        w_router[0].T,
        b_router[0].reshape(N_EXPERTS, 1),
    )
    xs_pad = jnp.pad(x_sample.reshape(DEC_BATCH, D_MODEL), ((0, MIX_TILE - DEC_BATCH), (0, 0)))
    spool_t = jnp.transpose(state_pool[0], (1, 0, 2))
    sconv_t = jnp.transpose(state_conv[0], (1, 0, 2))
    x1, h2, idx, gate, rank, cnt, pool_p, conv_p, u_s, cv_s = _mixer(
        x_prompt, xs_pad, spool_t, sconv_t, meta_tokens, w)

    block_exp, disp, pad_rows, dst_rows = _routing_tables(idx, rank, cnt)
    xs = _dispatch(disp, pad_rows, h2)
    out4 = _moe_experts(block_exp, dst_rows, xs,
                        w_gu[0].astype(_BF16), b_gu[0].reshape(N_EXPERTS, 1, 2 * D_FF),
                        w_down[0].astype(_BF16), b_down[0].reshape(N_EXPERTS, 1, D_MODEL))

    y_p, y_s = _combine(x1, gate.T, out4, g_final.reshape(1, D_MODEL))

    new_pool_p = pool_p[:, POOL_HALO - POOL_BUF:, :][None]
    new_conv_p = conv_p[:, CONV_HALO - CONV_BUF:, :][None]
    new_pool_s = jnp.concatenate([state_pool[0][:, 1:, :], u_s[:, None, :]], axis=1)[None]
    new_conv_s = jnp.concatenate([state_conv[0][:, 1:, :], cv_s[:, None, :]], axis=1)[None]
    return (y_p.reshape(BATCH, SEQ, D_MODEL), y_s.reshape(DEC_BATCH, 1, D_MODEL),
            new_pool_p, new_conv_p, new_pool_s, new_conv_s)
```

```python
import jax
import jax.numpy as jnp
from jax import lax
from jax.experimental import pallas as pl
from jax.experimental.pallas import tpu as pltpu

D_MODEL = 1024
BATCH = 8
SEQ = 2048
DEC_BATCH = 128
N_META = 16
D_POOL = 512
POOL_GROUP = 128
POOL_WINDOWS = (2, 4, 8, 16)
POOL_BUF = 15
D_CONV = 512
CONV_BUF = 2
N_EXPERTS = 32
TOP_K = 4
D_FF = 1024
SWIGLU_LIMIT = 7.0
SWIGLU_ALPHA = 1.702
RMS_EPS = 1e-5

O_CG = D_POOL
O_BG = O_CG + D_CONV
O_V = O_BG + D_CONV
O_GL = O_V + D_CONV

T_PROMPT = BATCH * SEQ
T_ALL = T_PROMPT + DEC_BATCH
N_ASSIGN = TOP_K * T_ALL

MIX_TILE = 256
MIX_PROMPT_STEPS = T_PROMPT // MIX_TILE
SEQ_TILES = SEQ // MIX_TILE
POOL_HALO = 16
CONV_HALO = 8
DISP_STEPS = MIX_PROMPT_STEPS + 1
MOE_BLK = 256
N_BLOCKS = -(-N_ASSIGN // MOE_BLK) + N_EXPERTS
N_PAD_ROWS = N_BLOCKS * MOE_BLK - N_ASSIGN
PAD_CHUNK = 64
assert N_PAD_ROWS % PAD_CHUNK == 0
COMB_TILE = 512
COMB_PROMPT_STEPS = T_PROMPT // COMB_TILE
T_PAD = (T_ALL + COMB_TILE - 1) // COMB_TILE * COMB_TILE
SLAB_GAP = T_PAD - T_ALL
OUT_ROWS = (N_BLOCKS + 2) * MOE_BLK
TAIL0 = TOP_K * T_PAD
assert OUT_ROWS >= TAIL0

ROW_TILE = 8
LANES = D_MODEL // ROW_TILE

VMEM_LIMIT = 56 * 1024 * 1024

_F32 = jnp.float32
_BF16 = jnp.bfloat16


def _rms(x, g):
    ms = jnp.mean(x * x, axis=-1, keepdims=True)
    return x * lax.rsqrt(ms + RMS_EPS) * g


def _bdot(a, w):
    return jnp.dot(a.astype(_BF16), w, preferred_element_type=_F32)


def _store_row_tiles(ref, val):
    rows = val.shape[0]
    for c in range(ROW_TILE):
        ref[pl.ds(c, rows, stride=ROW_TILE), :] = val[:, c * LANES:(c + 1) * LANES]


def _load_row_tiles(ref, rows):
    return jnp.concatenate([ref[pl.ds(c, rows, stride=ROW_TILE), :] for c in range(ROW_TILE)], axis=-1)


def _branches(z, pm, conv, wgrp_ref, pscale_ref, wup_ref, wcout_ref):
    pmb = pm.astype(_BF16)
    half = 2 * POOL_GROUP
    pg = jnp.concatenate(
        [jnp.dot(pmb[:, :half], wgrp_ref[0], preferred_element_type=_F32),
         jnp.dot(pmb[:, half:], wgrp_ref[1], preferred_element_type=_F32)], axis=-1)
    branch_a = _bdot(pg * pscale_ref[...], wup_ref[...])
    branch_b = _bdot(z[:, O_BG:O_V] * conv, wcout_ref[...])
    return branch_a, branch_b


def _merge_and_route(x, z, branch_a, branch_b, wo_ref, gffn_ref, wrt_ref, br_ref, base_ref, n_live):
    gates = jax.nn.sigmoid(z[:, O_GL:])
    merged = gates[:, :D_MODEL] * branch_a + gates[:, D_MODEL:] * branch_b
    x1 = x + _bdot(merged, wo_ref[...])
    h2 = _rms(x1, gffn_ref[...])
    logits = lax.dot_general(wrt_ref[...], h2, (((1,), (1,)), ((), ())),
                             precision=lax.Precision.HIGHEST,
                             preferred_element_type=_F32) + br_ref[...]
    iota = lax.broadcasted_iota(jnp.int32, logits.shape, 0)
    vals, idxs = [], []
    cur = logits
    for _ in range(TOP_K):
        m = jnp.max(cur, axis=0, keepdims=True)
        ik = jnp.min(jnp.where(cur == m, iota, N_EXPERTS), axis=0, keepdims=True)
        vals.append(m)
        idxs.append(ik)
        cur = jnp.where(iota == ik, -jnp.inf, cur)
    exps = [jnp.exp(v - vals[0]) for v in vals]
    denom = (exps[0] + exps[1]) + (exps[2] + exps[3])
    gate = [e / denom for e in exps]
    rows = logits.shape[1]
    live = lax.broadcasted_iota(jnp.int32, logits.shape, 1) < n_live
    onehots = [jnp.where(live, jnp.where(iota == ik, 1.0, 0.0), 0.0) for ik in idxs]
    member = (onehots[0] + onehots[1]) + (onehots[2] + onehots[3])
    earlier = (lax.broadcasted_iota(jnp.int32, (rows, rows), 0)
               < lax.broadcasted_iota(jnp.int32, (rows, rows), 1))
    before = jnp.dot(member.astype(_BF16), jnp.where(earlier, 1.0, 0.0).astype(_BF16),
                     preferred_element_type=_F32)
    pos = base_ref[...] + before
    ranks = [jnp.sum(oh * pos, axis=0, keepdims=True).astype(jnp.int32) for oh in onehots]
    base_ref[...] = base_ref[...] + jnp.sum(member, axis=1, keepdims=True)
    return x1, h2, idxs, gate, ranks


def _mixer_kernel(xp_ref, xs_ref, spool_ref, sconv_ref, meta_ref, gmix_ref, win_ref, wgrp_ref, pscale_ref,
                  wup_ref, wconv_ref, wcout_ref, wo_ref, gffn_ref, wrt_ref, br_ref,
                  x1_ref, h2_ref, idx_ref, gate_ref, rank_ref, cnt_ref, pool_out_ref, conv_out_ref, us_ref, cvs_ref,
                  pool_ext, conv_ext, meta_pool, meta_conv, pm_scr, conv_scr, base_scr):
    s = pl.program_id(0)
    is_prompt = s < MIX_PROMPT_STEPS
    j = lax.rem(s, SEQ_TILES)

    @pl.when(s == 0)
    def _():
        hm = _rms(meta_ref[...], gmix_ref[...]).astype(_BF16)
        zm = jnp.dot(hm, win_ref[:, :O_GL], preferred_element_type=_F32)
        meta_pool[...] = zm[:, :D_POOL]
        meta_conv[...] = zm[:, O_CG:O_BG] * zm[:, O_V:O_GL]
        base_scr[...] = jnp.zeros_like(base_scr)

    @pl.when(is_prompt & (j == 0))
    def _():
        pool_ext[0:POOL_HALO, :] = meta_pool[...]
        conv_ext[0:CONV_HALO, :] = meta_conv[N_META - CONV_HALO:, :]

    x = jnp.where(is_prompt, xp_ref[...], xs_ref[...])
    h = _rms(x, gmix_ref[...]).astype(_BF16)
    z = jnp.dot(h, win_ref[...], preferred_element_type=_F32)
    u = z[:, :D_POOL]
    cv = z[:, O_CG:O_BG] * z[:, O_V:O_GL]
    wc = wconv_ref[...]

    @pl.when(is_prompt)
    def _():
        pool_ext[POOL_HALO:, :] = u
        ext = pool_ext[...]
        for g, w in enumerate(POOL_WINDOWS):
            lanes = slice(g * POOL_GROUP, (g + 1) * POOL_GROUP)
            acc = ext[:, lanes]
            sh = 1
            while sh < w:
                acc = acc + pltpu.roll(acc, sh, 0)
                sh *= 2
            pm_scr[:, lanes] = acc[POOL_HALO:, :] * (1.0 / w) - u[:, lanes]
        conv_ext[CONV_HALO:, :] = cv
        cext = conv_ext[...]
        conv_scr[...] = (wc[0:1, :] * pltpu.roll(cext, 2, 0)[CONV_HALO:, :]
                         + wc[1:2, :] * pltpu.roll(cext, 1, 0)[CONV_HALO:, :]) + wc[2:3, :] * cv
        pool_ext[0:POOL_HALO, :] = u[MIX_TILE - POOL_HALO:, :]
        conv_ext[0:CONV_HALO, :] = cv[MIX_TILE - CONV_HALO:, :]

    @pl.when(is_prompt & (j == SEQ_TILES - 1))
    def _():
        pool_out_ref[...] = u[MIX_TILE - POOL_HALO:, :]
        conv_out_ref[...] = cv[MIX_TILE - CONV_HALO:, :]

    @pl.when(jnp.logical_not(is_prompt))
    def _():
        us = u[:DEC_BATCH, :]
        cvs = cv[:DEC_BATCH, :]
        pm_scr[...] = jnp.zeros_like(pm_scr)
        conv_scr[...] = jnp.zeros_like(conv_scr)
        for g, w in enumerate(POOL_WINDOWS):
            lanes = slice(g * POOL_GROUP, (g + 1) * POOL_GROUP)
            acc = us[:, lanes]
            for t in range(POOL_BUF - (w - 1), POOL_BUF):
                acc = acc + spool_ref[t, :, lanes]
            pm_scr[0:DEC_BATCH, lanes] = acc * (1.0 / w) - us[:, lanes]
        conv_scr[0:DEC_BATCH, :] = (wc[0:1, :] * sconv_ref[0] + wc[1:2, :] * sconv_ref[1]) + wc[2:3, :] * cvs
        us_ref[...] = us
        cvs_ref[...] = cvs

    branch_a, branch_b = _branches(z, pm_scr[...], conv_scr[...], wgrp_ref, pscale_ref, wup_ref, wcout_ref)
    n_live = jnp.where(is_prompt, MIX_TILE, DEC_BATCH)
    x1, h2, idxs, gate, ranks = _merge_and_route(x, z, branch_a, branch_b, wo_ref, gffn_ref, wrt_ref, br_ref,
                                                 base_scr, n_live)
    x1_ref[...] = x1
    _store_row_tiles(h2_ref, h2)
    for k in range(TOP_K):
        idx_ref[k:k + 1, :] = idxs[k]
        gate_ref[k:k + 1, :] = gate[k]
        rank_ref[k:k + 1, :] = ranks[k]

    @pl.when(jnp.logical_not(is_prompt))
    def _():
        cnt_ref[...] = jnp.broadcast_to(base_scr[...], cnt_ref.shape)


def _const_spec(shape):
    nd = len(shape)
    return pl.BlockSpec(shape, lambda *_: (0,) * nd)


def _mixer(x_prompt, xs_pad, spool_t, sconv_t, meta, w):
    last = MIX_PROMPT_STEPS - 1

    def x_map(s):
        sp = jnp.minimum(s, last)
        return (sp // SEQ_TILES, sp % SEQ_TILES, 0)

    def seq_map(s):
        return (jnp.minimum(s, last) // SEQ_TILES, 0, 0)

    consts = (xs_pad, spool_t, sconv_t, meta) + tuple(w)
    return pl.pallas_call(
        _mixer_kernel,
        grid=(MIX_PROMPT_STEPS + 1,),
        in_specs=[pl.BlockSpec((None, MIX_TILE, D_MODEL), x_map)] + [_const_spec(a.shape) for a in consts],
        out_specs=[
            pl.BlockSpec((MIX_TILE, D_MODEL), lambda s: (s, 0)),
            pl.BlockSpec((MIX_TILE * ROW_TILE, LANES), lambda s: (s, 0)),
            pl.BlockSpec((TOP_K, MIX_TILE), lambda s: (0, s)),
            pl.BlockSpec((TOP_K, MIX_TILE), lambda s: (0, s)),
            pl.BlockSpec((TOP_K, MIX_TILE), lambda s: (0, s)),
            _const_spec((N_EXPERTS, 128)),
            pl.BlockSpec((None, POOL_HALO, D_POOL), seq_map),
            pl.BlockSpec((None, CONV_HALO, D_CONV), seq_map),
            _const_spec((DEC_BATCH, D_POOL)),
            _const_spec((DEC_BATCH, D_CONV)),
        ],
        out_shape=[
            jax.ShapeDtypeStruct((T_ALL, D_MODEL), _F32),
            jax.ShapeDtypeStruct((T_ALL * ROW_TILE, LANES), _F32),
            jax.ShapeDtypeStruct((TOP_K, T_ALL), jnp.int32),
            jax.ShapeDtypeStruct((TOP_K, T_ALL), _F32),
            jax.ShapeDtypeStruct((TOP_K, T_ALL), jnp.int32),
            jax.ShapeDtypeStruct((N_EXPERTS, 128), _F32),
            jax.ShapeDtypeStruct((BATCH, POOL_HALO, D_POOL), _F32),
            jax.ShapeDtypeStruct((BATCH, CONV_HALO, D_CONV), _F32),
            jax.ShapeDtypeStruct((DEC_BATCH, D_POOL), _F32),
            jax.ShapeDtypeStruct((DEC_BATCH, D_CONV), _F32),
        ],
        scratch_shapes=[
            pltpu.VMEM((POOL_HALO + MIX_TILE, D_POOL), _F32),
            pltpu.VMEM((CONV_HALO + MIX_TILE, D_CONV), _F32),
            pltpu.VMEM((N_META, D_POOL), _F32),
            pltpu.VMEM((N_META, D_CONV), _F32),
            pltpu.VMEM((MIX_TILE, D_POOL), _F32),
            pltpu.VMEM((MIX_TILE, D_CONV), _F32),
            pltpu.VMEM((N_EXPERTS, 1), _F32),
        ],
        compiler_params=pltpu.CompilerParams(
            dimension_semantics=("arbitrary",), vmem_limit_bytes=VMEM_LIMIT),
        name="mixer",
    )(x_prompt, *consts)


def _routing_tables(idx, rank, cnt):
    def lookup(table, keys):
        ids = jnp.arange(table.shape[0], dtype=jnp.int32)
        return jnp.sum(jnp.where(keys[..., None] == ids, table, 0), axis=-1)

    counts = cnt[:, 0].astype(jnp.int32)
    padded = (counts + MOE_BLK - 1) // MOE_BLK * MOE_BLK
    pend = jnp.cumsum(padded)
    pstart = pend - padded
    start = jnp.cumsum(counts) - counts

    dest = lookup(pstart, idx) + rank
    dest = jnp.pad(dest, ((0, 0), (0, DISP_STEPS * MIX_TILE - T_ALL)))
    disp = dest.reshape(TOP_K, DISP_STEPS, MIX_TILE).transpose(1, 0, 2).reshape(DISP_STEPS, TOP_K * MIX_TILE)

    n_pad = padded - counts
    pad_cum = jnp.cumsum(n_pad)
    q = jnp.arange(N_PAD_ROWS, dtype=jnp.int32)
    e_q = jnp.sum((q[:, None] >= pad_cum[None, :]).astype(jnp.int32), axis=1)
    first_pad = lookup(jnp.concatenate([pstart + counts, pend[-1:]]), e_q)
    pads_before = lookup(jnp.concatenate([pad_cum - n_pad, pad_cum[-1:]]), e_q)
    pad_rows = first_pad + q - pads_before

    e_flat = idx.T.reshape(N_ASSIGN)
    order = jnp.argsort(e_flat, stable=True).astype(jnp.int32)
    block_start = jnp.arange(N_BLOCKS, dtype=jnp.int32) * MOE_BLK
    block_exp = jnp.minimum(jnp.sum((block_start[:, None] >= pend[None, :]).astype(jnp.int32), axis=1),
                            N_EXPERTS - 1)
    in_block = jnp.arange(MOE_BLK, dtype=jnp.int32)[None, :]
    jr = (block_start - lookup(pstart, block_exp))[:, None] + in_block
    valid = (jr < lookup(counts, block_exp)[:, None]).reshape(N_BLOCKS * MOE_BLK)
    sorted_pos = (lookup(start, block_exp)[:, None] + jr).reshape(N_BLOCKS * MOE_BLK)
    a_r = order[jnp.clip(sorted_pos, 0, N_ASSIGN - 1)]
    t_r = a_r // TOP_K
    k_r = a_r - t_r * TOP_K

    def spare_row(q):
        in_gap = q < TOP_K * SLAB_GAP
        gap_row = (q // SLAB_GAP) * T_PAD + T_ALL + q % SLAB_GAP
        return jnp.where(in_gap, gap_row, TAIL0 + q - TOP_K * SLAB_GAP)

    pad_rank = jnp.cumsum((~valid).astype(jnp.int32)) - 1
    dst = jnp.where(valid, k_r * T_PAD + t_r, spare_row(2 * MOE_BLK + pad_rank)).astype(jnp.int32)
    warmup = spare_row(jnp.arange(2 * MOE_BLK, dtype=jnp.int32)).astype(jnp.int32).reshape(2, MOE_BLK)
    dst_rows = jnp.concatenate([warmup, dst.reshape(N_BLOCKS, MOE_BLK)], axis=0)
    blocks = jnp.arange(N_BLOCKS, dtype=jnp.int32)
    first = jnp.concatenate([jnp.ones((1,), jnp.bool_), block_exp[1:] != block_exp[:-1]])
    run_slot = (jnp.cumsum(first.astype(jnp.int32)) - 1) % 2
    later_first = first[None, :] & (blocks[None, :] > blocks[:, None])
    next_first = jnp.min(jnp.where(later_first, blocks[None, :], N_BLOCKS), axis=1)
    has_next = next_first < N_BLOCKS
    next_exp = lookup(block_exp, jnp.minimum(next_first, N_BLOCKS - 1))
    n_used = jnp.broadcast_to(pend[-1] // MOE_BLK, (N_BLOCKS,))
    plan = jnp.stack([block_exp, first.astype(jnp.int32), run_slot, next_exp, has_next.astype(jnp.int32),
                      n_used]).astype(jnp.int32)
    return plan, disp * ROW_TILE, pad_rows * ROW_TILE, dst_rows * ROW_TILE


def _dispatch_kernel(disp_hbm, pads_hbm, h2_ref, xs_hbm, tbl0, tbl1, pad_tbl, zero_tile, rsem, tsem, psem):
    s = pl.program_id(0)
    ns = pl.num_programs(0)
    tbls = (tbl0, tbl1)

    def table_copy(row, p):
        return pltpu.make_async_copy(disp_hbm.at[row], tbls[p], tsem.at[p])

    def row_copy(off, r, k):
        return pltpu.make_async_copy(h2_ref.at[pl.ds(r * ROW_TILE, ROW_TILE), :],
                                     xs_hbm.at[pl.ds(off, ROW_TILE), :], rsem.at[k])

    def pad_copy(off):
        return pltpu.make_async_copy(zero_tile, xs_hbm.at[pl.ds(off, ROW_TILE), :], psem.at[0])

    def scatter_rows(p, n_rows):
        for r in range(n_rows):
            for k in range(TOP_K):
                off = pl.multiple_of(tbls[p][k * MIX_TILE + r], ROW_TILE)
                row_copy(off, r, k).start(priority=(r + k) % 2)
        for r in range(n_rows):
            for k in range(TOP_K):
                row_copy(0, r, k).wait()

    def fill_pads():
        zero_tile[...] = jnp.zeros_like(zero_tile)
        load = pltpu.make_async_copy(pads_hbm, pad_tbl, psem.at[1])
        load.start()
        load.wait()

        def chunk(i, carry):
            for j in range(PAD_CHUNK):
                pad_copy(pl.multiple_of(pad_tbl[i * PAD_CHUNK + j], ROW_TILE)).start(priority=j % 2)

            @pl.when(i > 0)
            def _():
                for j in range(PAD_CHUNK):
                    pad_copy(0).wait()
            return carry

        lax.fori_loop(0, N_PAD_ROWS // PAD_CHUNK, chunk, 0)
        for j in range(PAD_CHUNK):
            pad_copy(0).wait()

    @pl.when(s == 0)
    def _():
        table_copy(0, 0).start()

    def step(p):
        table_copy(s, p).wait()

        @pl.when(s < ns - 1)
        def _():
            table_copy(s + 1, 1 - p).start()
            scatter_rows(p, MIX_TILE)

        @pl.when(s == ns - 1)
        def _():
            scatter_rows(p, DEC_BATCH)
            fill_pads()

    parity = lax.rem(s, 2)
    pl.when(parity == 0)(lambda: step(0))
    pl.when(parity == 1)(lambda: step(1))


def _dispatch(disp, pad_rows, h2):
    return pl.pallas_call(
        _dispatch_kernel,
        grid=(DISP_STEPS,),
        in_specs=[pl.BlockSpec(memory_space=pl.ANY),
                  pl.BlockSpec(memory_space=pl.ANY),
                  pl.BlockSpec((MIX_TILE * ROW_TILE, LANES), lambda s: (s, 0))],
        out_specs=pl.BlockSpec(memory_space=pl.ANY),
        out_shape=jax.ShapeDtypeStruct((N_BLOCKS * MOE_BLK * ROW_TILE, LANES), _F32),
        scratch_shapes=[
            pltpu.SMEM((TOP_K * MIX_TILE,), jnp.int32),
            pltpu.SMEM((TOP_K * MIX_TILE,), jnp.int32),
            pltpu.SMEM((N_PAD_ROWS,), jnp.int32),
            pltpu.VMEM((ROW_TILE, LANES), _F32),
            pltpu.SemaphoreType.DMA((TOP_K,)),
            pltpu.SemaphoreType.DMA((2,)),
            pltpu.SemaphoreType.DMA((2,)),
        ],
        compiler_params=pltpu.CompilerParams(
            dimension_semantics=("arbitrary",), vmem_limit_bytes=VMEM_LIMIT),
        name="dispatch",
    )(disp, pad_rows, h2)


PLAN_EXPERT, PLAN_FIRST, PLAN_SLOT, PLAN_NEXT, PLAN_HAS_NEXT, PLAN_USED = range(6)


def _moe_kernel(plan_ref, table_hbm, x_ref, wgu_hbm, bgu_ref, wdn_hbm, bdn_ref, out_hbm,
                ybuf0, ybuf1, tbl0, tbl1, gu_stage0, gu_stage1, dn_stage0, dn_stage1, wgu_bf, wdn_bf,
                ssem, tsem, wsem):
    b = pl.program_id(0)
    nb = pl.num_programs(0)
    ybufs, tbls = (ybuf0, ybuf1), (tbl0, tbl1)
    gu_stages, dn_stages = (gu_stage0, gu_stage1), (dn_stage0, dn_stage1)
    n_used = plan_ref[PLAN_USED, 0]

    def table_copy(row, p):
        return pltpu.make_async_copy(table_hbm.at[row], tbls[p], tsem.at[p])

    def weight_copies(e, w):
        return (pltpu.make_async_copy(wgu_hbm.at[e], gu_stages[w], wsem.at[2 * w]),
                pltpu.make_async_copy(wdn_hbm.at[e], dn_stages[w], wsem.at[2 * w + 1]))

    def scatter_row(off, r, p):
        return pltpu.make_async_copy(ybufs[p].at[pl.ds(r * ROW_TILE, ROW_TILE), :],
                                     out_hbm.at[pl.ds(off, ROW_TILE), :], ssem.at[p])

    def start_scatter(tp, p):
        for r in range(MOE_BLK):
            scatter_row(pl.multiple_of(tbls[tp][r], ROW_TILE), r, p).start(priority=r % 2)

    def wait_scatter(p):
        for r in range(MOE_BLK):
            scatter_row(0, r, p).wait()

    @pl.when(b == 0)
    def _():
        first = table_copy(0, 0)
        first.start()
        first.wait()
        ybuf0[...] = jnp.zeros_like(ybuf0)
        ybuf1[...] = jnp.zeros_like(ybuf1)
        start_scatter(0, 0)
        table_copy(1, 1).start()
        for c in weight_copies(plan_ref[PLAN_EXPERT, 0], 0):
            c.start()

    def new_expert(w):
        for c in weight_copies(0, w):
            c.wait()

        @pl.when(plan_ref[PLAN_HAS_NEXT, b] == 1)
        def _():
            for c in weight_copies(plan_ref[PLAN_NEXT, b], 1 - w):
                c.start()

        wgu_bf[...] = gu_stages[w][...].astype(_BF16)
        wdn_bf[...] = dn_stages[w][...].astype(_BF16)

    is_first = plan_ref[PLAN_FIRST, b] == 1
    wslot = plan_ref[PLAN_SLOT, b]
    pl.when(is_first & (wslot == 0))(lambda: new_expert(0))
    pl.when(is_first & (wslot == 1))(lambda: new_expert(1))

    def step(p):
        q = 1 - p
        table_copy(b + 1, q).wait()

        @pl.when(b < n_used)
        def _():
            start_scatter(q, q)
            table_copy(b + 2, p).start()
            x = _load_row_tiles(x_ref, MOE_BLK).astype(_BF16)
            hgu = jnp.dot(x, wgu_bf[...], preferred_element_type=_F32) + bgu_ref[...]
            gate = jnp.minimum(hgu[:, :D_FF], SWIGLU_LIMIT)
            up = jnp.clip(hgu[:, D_FF:], -SWIGLU_LIMIT, SWIGLU_LIMIT)
            act = gate * jax.nn.sigmoid(SWIGLU_ALPHA * gate) * (up + 1.0)
            y = jnp.dot(act.astype(_BF16), wdn_bf[...], preferred_element_type=_F32) + bdn_ref[...]
            wait_scatter(p)
            _store_row_tiles(ybufs[p], y)

        @pl.when(b >= n_used)
        def _():
            start_scatter(q, q)
            table_copy(b + 2, p).start()
            wait_scatter(p)
            ybufs[p][...] = jnp.zeros_like(ybufs[p])

        @pl.when(b == nb - 1)
        def _():
            table_copy(b + 2, p).wait()
            start_scatter(p, p)
            wait_scatter(p)
            wait_scatter(q)

    parity = lax.rem(b, 2)
    pl.when(parity == 0)(lambda: step(0))
    pl.when(parity == 1)(lambda: step(1))


def _moe_experts(plan, table, xs, wgu, bgu, wdn, bdn):
    grid_spec = pltpu.PrefetchScalarGridSpec(
        num_scalar_prefetch=1,
        grid=(N_BLOCKS,),
        in_specs=[
            pl.BlockSpec(memory_space=pl.ANY),
            pl.BlockSpec((MOE_BLK * ROW_TILE, LANES), lambda b, plan: (b, 0)),
            pl.BlockSpec(memory_space=pl.ANY),
            pl.BlockSpec((None, 1, 2 * D_FF), lambda b, plan: (plan[PLAN_EXPERT, b], 0, 0)),
            pl.BlockSpec(memory_space=pl.ANY),
            pl.BlockSpec((None, 1, D_MODEL), lambda b, plan: (plan[PLAN_EXPERT, b], 0, 0)),
        ],
        out_specs=pl.BlockSpec(memory_space=pl.ANY),
        scratch_shapes=[
            pltpu.VMEM((MOE_BLK * ROW_TILE, LANES), _F32),
            pltpu.VMEM((MOE_BLK * ROW_TILE, LANES), _F32),
            pltpu.SMEM((MOE_BLK,), jnp.int32),
            pltpu.SMEM((MOE_BLK,), jnp.int32),
            pltpu.VMEM((D_MODEL, 2 * D_FF), _F32),
            pltpu.VMEM((D_MODEL, 2 * D_FF), _F32),
            pltpu.VMEM((D_FF, D_MODEL), _F32),
            pltpu.VMEM((D_FF, D_MODEL), _F32),
            pltpu.VMEM((D_MODEL, 2 * D_FF), _BF16),
            pltpu.VMEM((D_FF, D_MODEL), _BF16),
            pltpu.SemaphoreType.DMA((2,)),
            pltpu.SemaphoreType.DMA((2,)),
            pltpu.SemaphoreType.DMA((4,)),
        ],
    )
    return pl.pallas_call(
        _moe_kernel,
        grid_spec=grid_spec,
        out_shape=jax.ShapeDtypeStruct((OUT_ROWS * ROW_TILE, LANES), _F32),
        compiler_params=pltpu.CompilerParams(
            dimension_semantics=("arbitrary",), vmem_limit_bytes=VMEM_LIMIT),
        name="moe_experts",
    )(plan, table, xs, wgu, bgu, wdn, bdn)


def _combine_kernel(x1_ref, g_ref, o0_ref, o1_ref, o2_ref, o3_ref, gfin_ref, yp_ref, ys_ref):
    i = pl.program_id(0)
    g = g_ref[...]
    o = [_load_row_tiles(r, COMB_TILE) for r in (o0_ref, o1_ref, o2_ref, o3_ref)]
    moe = (g[:, 0:1] * o[0] + g[:, 1:2] * o[1]) + (g[:, 2:3] * o[2] + g[:, 3:4] * o[3])
    y = _rms(x1_ref[...] + moe, gfin_ref[...])

    @pl.when(i < COMB_PROMPT_STEPS)
    def _():
        yp_ref[...] = y

    @pl.when(i == COMB_PROMPT_STEPS)
    def _():
        ys_ref[...] = y[:DEC_BATCH, :]


def _combine(x1, gates_tk, out4, g_final):
    slab_blocks = T_PAD // COMB_TILE

    def slab_spec(k):
        return pl.BlockSpec((COMB_TILE * ROW_TILE, LANES), lambda i, k=k: (k * slab_blocks + i, 0))

    return pl.pallas_call(
        _combine_kernel,
        grid=(COMB_PROMPT_STEPS + 1,),
        in_specs=[pl.BlockSpec((COMB_TILE, D_MODEL), lambda i: (i, 0)),
                  pl.BlockSpec((COMB_TILE, TOP_K), lambda i: (i, 0)),
                  slab_spec(0), slab_spec(1), slab_spec(2), slab_spec(3),
                  _const_spec(g_final.shape)],
        out_specs=[pl.BlockSpec((COMB_TILE, D_MODEL), lambda i: (jnp.minimum(i, COMB_PROMPT_STEPS - 1), 0)),
                   _const_spec((DEC_BATCH, D_MODEL))],
        out_shape=[jax.ShapeDtypeStruct((T_PROMPT, D_MODEL), _F32),
                   jax.ShapeDtypeStruct((DEC_BATCH, D_MODEL), _F32)],
        compiler_params=pltpu.CompilerParams(
            dimension_semantics=("arbitrary",), vmem_limit_bytes=VMEM_LIMIT),
        name="combine",
    )(x1, gates_tk, out4, out4, out4, out4, g_final)


def kernel(x_prompt, x_sample, state_pool, state_conv, meta_tokens, g_mix, w_in, w_pool_grp, pool_scale,
           w_pool_up, w_conv, w_conv_out, w_o, g_ffn, w_router, b_router, w_gu, b_gu, w_down, b_down, g_final):
    assert g_mix.shape[0] == 1, "single-layer step"
    grp = w_pool_grp[0].astype(_BF16)
    zero = jnp.zeros((POOL_GROUP, POOL_GROUP), _BF16)
    wgrp = jnp.stack([jnp.block([[grp[0], zero], [zero, grp[1]]]),
                      jnp.block([[grp[2], zero], [zero, grp[3]]])])
    w = (
        g_mix[0].reshape(1, D_MODEL),
        w_in[0].astype(_BF16),
        wgrp,
        pool_scale[0].reshape(1, D_POOL),
        w_pool_up[0].astype(_BF16),
        w_conv[0],
        w_conv_out[0].astype(_BF16),
        w_o[0].astype(_BF16),
        g_ffn[0].reshape(1, D_MODEL),
        w_router[0].T,
        b_router[0].reshape(N_EXPERTS, 1),
    )
    xs_pad = jnp.pad(x_sample.reshape(DEC_BATCH, D_MODEL), ((0, MIX_TILE - DEC_BATCH), (0, 0)))
    spool_t = jnp.transpose(state_pool[0], (1, 0, 2))
    sconv_t = jnp.transpose(state_conv[0], (1, 0, 2))
    x1, h2, idx, gate, rank, cnt, pool_p, conv_p, u_s, cv_s = _mixer(
        x_prompt, xs_pad, spool_t, sconv_t, meta_tokens, w)

    plan, disp, pad_rows, dst_rows = _routing_tables(idx, rank, cnt)
    xs = _dispatch(disp, pad_rows, h2)
    out4 = _moe_experts(plan, dst_rows, xs,
                        w_gu.reshape(N_EXPERTS, D_MODEL, 2 * D_FF), b_gu.reshape(N_EXPERTS, 1, 2 * D_FF),
                        w_down.reshape(N_EXPERTS, D_FF, D_MODEL), b_down.reshape(N_EXPERTS, 1, D_MODEL))

    y_p, y_s = _combine(x1, gate.T, out4, g_final.reshape(1, D_MODEL))

    new_pool_p = pool_p[:, POOL_HALO - POOL_BUF:, :][None]
    new_conv_p = conv_p[:, CONV_HALO - CONV_BUF:, :][None]
    new_pool_s = jnp.concatenate([state_pool[0][:, 1:, :], u_s[:, None, :]], axis=1)[None]
    new_conv_s = jnp.concatenate([state_conv[0][:, 1:, :], cv_s[:, None, :]], axis=1)[None]
    return (y_p.reshape(BATCH, SEQ, D_MODEL), y_s.reshape(DEC_BATCH, 1, D_MODEL),
            new_pool_p, new_conv_p, new_pool_s, new_conv_s)
```

```python
import jax
import jax.numpy as jnp
from jax import lax
from jax.experimental import pallas as pl
from jax.experimental.pallas import tpu as pltpu

D_MODEL = 1024
BATCH = 8
SEQ = 2048
DEC_BATCH = 128
N_META = 16
D_POOL = 512
POOL_GROUP = 128
POOL_WINDOWS = (2, 4, 8, 16)
POOL_BUF = 15
D_CONV = 512
CONV_BUF = 2
N_EXPERTS = 32
TOP_K = 4
D_FF = 1024
SWIGLU_LIMIT = 7.0
SWIGLU_ALPHA = 1.702
RMS_EPS = 1e-5

O_CG = D_POOL
O_BG = O_CG + D_CONV
O_V = O_BG + D_CONV
O_GL = O_V + D_CONV

T_PROMPT = BATCH * SEQ
T_ALL = T_PROMPT + DEC_BATCH
N_ASSIGN = TOP_K * T_ALL

MIX_TILE = 512
MIX_PROMPT_STEPS = T_PROMPT // MIX_TILE
SEQ_TILES = SEQ // MIX_TILE
POOL_HALO = 16
CONV_HALO = 8
DISP_TILE = 256
DISP_STEPS = -(-T_ALL // DISP_TILE)
DISP_LAST_ROWS = T_ALL - (DISP_STEPS - 1) * DISP_TILE
MOE_BLK = 256
N_BLOCKS = -(-N_ASSIGN // MOE_BLK) + N_EXPERTS
N_PAD_ROWS = N_BLOCKS * MOE_BLK - N_ASSIGN
PAD_CHUNK = 64
assert N_PAD_ROWS % PAD_CHUNK == 0
FF_CHUNKS = 4
FF_CHUNK = D_FF // FF_CHUNKS
COMB_TILE = 512
COMB_PROMPT_STEPS = T_PROMPT // COMB_TILE
T_PAD = (T_ALL + COMB_TILE - 1) // COMB_TILE * COMB_TILE
SLAB_GAP = T_PAD - T_ALL
OUT_ROWS = (N_BLOCKS + 2) * MOE_BLK
TAIL0 = TOP_K * T_PAD
assert OUT_ROWS >= TAIL0

ROW_TILE = 8
LANES = D_MODEL // ROW_TILE

VMEM_LIMIT = 56 * 1024 * 1024

_F32 = jnp.float32
_BF16 = jnp.bfloat16


def _rms(x, g):
    ms = jnp.mean(x * x, axis=-1, keepdims=True)
    return x * lax.rsqrt(ms + RMS_EPS) * g


def _bdot(a, w):
    return jnp.dot(a.astype(_BF16), w, preferred_element_type=_F32)


def _store_row_tiles(ref, val):
    rows = val.shape[0]
    for c in range(ROW_TILE):
        ref[pl.ds(c, rows, stride=ROW_TILE), :] = val[:, c * LANES:(c + 1) * LANES]


def _load_row_tiles(ref, rows):
    return jnp.concatenate([ref[pl.ds(c, rows, stride=ROW_TILE), :] for c in range(ROW_TILE)], axis=-1)


def _branches(z, pm, conv, wgrp_ref, pscale_ref, wup_ref, wcout_ref):
    pmb = pm.astype(_BF16)
    half = 2 * POOL_GROUP
    pg = jnp.concatenate(
        [jnp.dot(pmb[:, :half], wgrp_ref[0], preferred_element_type=_F32),
         jnp.dot(pmb[:, half:], wgrp_ref[1], preferred_element_type=_F32)], axis=-1)
    branch_a = _bdot(pg * pscale_ref[...], wup_ref[...])
    branch_b = _bdot(z[:, O_BG:O_V] * conv, wcout_ref[...])
    return branch_a, branch_b


def _merge_and_route(x, z, branch_a, branch_b, wo_ref, gffn_ref, wrt_ref, br_ref, base_ref, n_live):
    gates = jax.nn.sigmoid(z[:, O_GL:])
    merged = gates[:, :D_MODEL] * branch_a + gates[:, D_MODEL:] * branch_b
    x1 = x + _bdot(merged, wo_ref[...])
    h2 = _rms(x1, gffn_ref[...])
    logits = lax.dot_general(wrt_ref[...], h2, (((1,), (1,)), ((), ())),
                             precision=lax.Precision.HIGHEST,
                             preferred_element_type=_F32) + br_ref[...]
    iota = lax.broadcasted_iota(jnp.int32, logits.shape, 0)
    vals, idxs = [], []
    cur = logits
    for _ in range(TOP_K):
        m = jnp.max(cur, axis=0, keepdims=True)
        ik = jnp.min(jnp.where(cur == m, iota, N_EXPERTS), axis=0, keepdims=True)
        vals.append(m)
        idxs.append(ik)
        cur = jnp.where(iota == ik, -jnp.inf, cur)
    exps = [jnp.exp(v - vals[0]) for v in vals]
    denom = (exps[0] + exps[1]) + (exps[2] + exps[3])
    gate = [e / denom for e in exps]
    rows = logits.shape[1]
    live = lax.broadcasted_iota(jnp.int32, logits.shape, 1) < n_live
    onehots = [jnp.where(live, jnp.where(iota == ik, 1.0, 0.0), 0.0) for ik in idxs]
    member = (onehots[0] + onehots[1]) + (onehots[2] + onehots[3])
    earlier = (lax.broadcasted_iota(jnp.int32, (rows, rows), 0)
               < lax.broadcasted_iota(jnp.int32, (rows, rows), 1))
    before = jnp.dot(member.astype(_BF16), jnp.where(earlier, 1.0, 0.0).astype(_BF16),
                     preferred_element_type=_F32)
    pos = base_ref[...] + before
    ranks = [jnp.sum(oh * pos, axis=0, keepdims=True).astype(jnp.int32) for oh in onehots]
    base_ref[...] = base_ref[...] + jnp.sum(member, axis=1, keepdims=True)
    return x1, h2, idxs, gate, ranks


def _mixer_kernel(xp_ref, xs_ref, spool_ref, sconv_ref, meta_ref, gmix_ref, win_ref, wgrp_ref, pscale_ref,
                  wup_ref, wconv_ref, wcout_ref, wo_ref, gffn_ref, wrt_ref, br_ref,
                  x1_ref, h2_ref, idx_ref, gate_ref, rank_ref, cnt_ref, pool_out_ref, conv_out_ref, us_ref, cvs_ref,
                  pool_ext, conv_ext, meta_pool, meta_conv, pm_scr, conv_scr, base_scr):
    s = pl.program_id(0)
    is_prompt = s < MIX_PROMPT_STEPS
    j = lax.rem(s, SEQ_TILES)

    @pl.when(s == 0)
    def _():
        hm = _rms(meta_ref[...], gmix_ref[...]).astype(_BF16)
        zm = jnp.dot(hm, win_ref[:, :O_GL], preferred_element_type=_F32)
        meta_pool[...] = zm[:, :D_POOL]
        meta_conv[...] = zm[:, O_CG:O_BG] * zm[:, O_V:O_GL]
        base_scr[...] = jnp.zeros_like(base_scr)

    @pl.when(is_prompt & (j == 0))
    def _():
        pool_ext[0:POOL_HALO, :] = meta_pool[...]
        conv_ext[0:CONV_HALO, :] = meta_conv[N_META - CONV_HALO:, :]

    x = jnp.where(is_prompt, xp_ref[...], xs_ref[...])
    h = _rms(x, gmix_ref[...]).astype(_BF16)
    z = jnp.dot(h, win_ref[...], preferred_element_type=_F32)
    u = z[:, :D_POOL]
    cv = z[:, O_CG:O_BG] * z[:, O_V:O_GL]
    wc = wconv_ref[...]

    @pl.when(is_prompt)
    def _():
        pool_ext[POOL_HALO:, :] = u
        ext = pool_ext[...]
        for g, w in enumerate(POOL_WINDOWS):
            lanes = slice(g * POOL_GROUP, (g + 1) * POOL_GROUP)
            acc = ext[:, lanes]
            sh = 1
            while sh < w:
                acc = acc + pltpu.roll(acc, sh, 0)
                sh *= 2
            pm_scr[:, lanes] = acc[POOL_HALO:, :] * (1.0 / w) - u[:, lanes]
        conv_ext[CONV_HALO:, :] = cv
        cext = conv_ext[...]
        conv_scr[...] = (wc[0:1, :] * pltpu.roll(cext, 2, 0)[CONV_HALO:, :]
                         + wc[1:2, :] * pltpu.roll(cext, 1, 0)[CONV_HALO:, :]) + wc[2:3, :] * cv
        pool_ext[0:POOL_HALO, :] = u[MIX_TILE - POOL_HALO:, :]
        conv_ext[0:CONV_HALO, :] = cv[MIX_TILE - CONV_HALO:, :]

    @pl.when(is_prompt & (j == SEQ_TILES - 1))
    def _():
        pool_out_ref[...] = u[MIX_TILE - POOL_HALO:, :]
        conv_out_ref[...] = cv[MIX_TILE - CONV_HALO:, :]

    @pl.when(jnp.logical_not(is_prompt))
    def _():
        us = u[:DEC_BATCH, :]
        cvs = cv[:DEC_BATCH, :]
        pm_scr[...] = jnp.zeros_like(pm_scr)
        conv_scr[...] = jnp.zeros_like(conv_scr)
        for g, w in enumerate(POOL_WINDOWS):
            lanes = slice(g * POOL_GROUP, (g + 1) * POOL_GROUP)
            acc = us[:, lanes]
            for t in range(POOL_BUF - (w - 1), POOL_BUF):
                acc = acc + spool_ref[t, :, lanes]
            pm_scr[0:DEC_BATCH, lanes] = acc * (1.0 / w) - us[:, lanes]
        conv_scr[0:DEC_BATCH, :] = (wc[0:1, :] * sconv_ref[0] + wc[1:2, :] * sconv_ref[1]) + wc[2:3, :] * cvs
        us_ref[...] = us
        cvs_ref[...] = cvs

    branch_a, branch_b = _branches(z, pm_scr[...], conv_scr[...], wgrp_ref, pscale_ref, wup_ref, wcout_ref)
    n_live = jnp.where(is_prompt, MIX_TILE, DEC_BATCH)
    x1, h2, idxs, gate, ranks = _merge_and_route(x, z, branch_a, branch_b, wo_ref, gffn_ref, wrt_ref, br_ref,
                                                 base_scr, n_live)
    x1_ref[...] = x1
    _store_row_tiles(h2_ref, h2)
    for k in range(TOP_K):
        idx_ref[k:k + 1, :] = idxs[k]
        gate_ref[k:k + 1, :] = gate[k]
        rank_ref[k:k + 1, :] = ranks[k]

    @pl.when(jnp.logical_not(is_prompt))
    def _():
        cnt_ref[...] = jnp.broadcast_to(base_scr[...], cnt_ref.shape)


def _const_spec(shape):
    nd = len(shape)
    return pl.BlockSpec(shape, lambda *_: (0,) * nd)


def _resident_spec(shape):
    nd = len(shape)
    return pl.BlockSpec(shape, lambda *_: (0,) * nd, pipeline_mode=pl.Buffered(1))


def _mixer(x_prompt, xs_pad, spool_t, sconv_t, meta, w):
    last = MIX_PROMPT_STEPS - 1

    def x_map(s):
        sp = jnp.minimum(s, last)
        return (sp // SEQ_TILES, sp % SEQ_TILES, 0)

    def seq_map(s):
        return (jnp.minimum(s, last) // SEQ_TILES, 0, 0)

    consts = (xs_pad, spool_t, sconv_t, meta) + tuple(w)
    return pl.pallas_call(
        _mixer_kernel,
        grid=(MIX_PROMPT_STEPS + 1,),
        in_specs=[pl.BlockSpec((None, MIX_TILE, D_MODEL), x_map)] + [_resident_spec(a.shape) for a in consts],
        out_specs=[
            pl.BlockSpec((MIX_TILE, D_MODEL), lambda s: (s, 0)),
            pl.BlockSpec((MIX_TILE * ROW_TILE, LANES), lambda s: (s, 0)),
            pl.BlockSpec((TOP_K, MIX_TILE), lambda s: (0, s)),
            pl.BlockSpec((TOP_K, MIX_TILE), lambda s: (0, s)),
            pl.BlockSpec((TOP_K, MIX_TILE), lambda s: (0, s)),
            _const_spec((N_EXPERTS, 128)),
            pl.BlockSpec((None, POOL_HALO, D_POOL), seq_map),
            pl.BlockSpec((None, CONV_HALO, D_CONV), seq_map),
            _const_spec((DEC_BATCH, D_POOL)),
            _const_spec((DEC_BATCH, D_CONV)),
        ],
        out_shape=[
            jax.ShapeDtypeStruct((T_ALL, D_MODEL), _F32),
            jax.ShapeDtypeStruct((T_ALL * ROW_TILE, LANES), _F32),
            jax.ShapeDtypeStruct((TOP_K, T_ALL), jnp.int32),
            jax.ShapeDtypeStruct((TOP_K, T_ALL), _F32),
            jax.ShapeDtypeStruct((TOP_K, T_ALL), jnp.int32),
            jax.ShapeDtypeStruct((N_EXPERTS, 128), _F32),
            jax.ShapeDtypeStruct((BATCH, POOL_HALO, D_POOL), _F32),
            jax.ShapeDtypeStruct((BATCH, CONV_HALO, D_CONV), _F32),
            jax.ShapeDtypeStruct((DEC_BATCH, D_POOL), _F32),
            jax.ShapeDtypeStruct((DEC_BATCH, D_CONV), _F32),
        ],
        scratch_shapes=[
            pltpu.VMEM((POOL_HALO + MIX_TILE, D_POOL), _F32),
            pltpu.VMEM((CONV_HALO + MIX_TILE, D_CONV), _F32),
            pltpu.VMEM((N_META, D_POOL), _F32),
            pltpu.VMEM((N_META, D_CONV), _F32),
            pltpu.VMEM((MIX_TILE, D_POOL), _F32),
            pltpu.VMEM((MIX_TILE, D_CONV), _F32),
            pltpu.VMEM((N_EXPERTS, 1), _F32),
        ],
        compiler_params=pltpu.CompilerParams(
            dimension_semantics=("arbitrary",), vmem_limit_bytes=VMEM_LIMIT),
        name="mixer",
    )(x_prompt, *consts)


def _routing_tables(idx, rank, cnt):
    def lookup(table, keys):
        ids = jnp.arange(table.shape[0], dtype=jnp.int32)
        return jnp.sum(jnp.where(keys[..., None] == ids, table, 0), axis=-1)

    counts = cnt[:, 0].astype(jnp.int32)
    padded = (counts + MOE_BLK - 1) // MOE_BLK * MOE_BLK
    pend = jnp.cumsum(padded)
    pstart = pend - padded
    start = jnp.cumsum(counts) - counts

    dest = lookup(pstart, idx) + rank
    dest = jnp.pad(dest, ((0, 0), (0, DISP_STEPS * DISP_TILE - T_ALL)))
    disp = dest.reshape(TOP_K, DISP_STEPS, DISP_TILE).transpose(1, 0, 2).reshape(DISP_STEPS, TOP_K * DISP_TILE)

    n_pad = padded - counts
    pad_cum = jnp.cumsum(n_pad)
    q = jnp.arange(N_PAD_ROWS, dtype=jnp.int32)
    e_q = jnp.sum((q[:, None] >= pad_cum[None, :]).astype(jnp.int32), axis=1)
    first_pad = lookup(jnp.concatenate([pstart + counts, pend[-1:]]), e_q)
    pads_before = lookup(jnp.concatenate([pad_cum - n_pad, pad_cum[-1:]]), e_q)
    pad_rows = first_pad + q - pads_before

    e_flat = idx.T.reshape(N_ASSIGN)
    order = jnp.argsort(e_flat, stable=True).astype(jnp.int32)
    block_start = jnp.arange(N_BLOCKS, dtype=jnp.int32) * MOE_BLK
    block_exp = jnp.minimum(jnp.sum((block_start[:, None] >= pend[None, :]).astype(jnp.int32), axis=1),
                            N_EXPERTS - 1)
    in_block = jnp.arange(MOE_BLK, dtype=jnp.int32)[None, :]
    jr = (block_start - lookup(pstart, block_exp))[:, None] + in_block
    valid = (jr < lookup(counts, block_exp)[:, None]).reshape(N_BLOCKS * MOE_BLK)
    sorted_pos = (lookup(start, block_exp)[:, None] + jr).reshape(N_BLOCKS * MOE_BLK)
    a_r = order[jnp.clip(sorted_pos, 0, N_ASSIGN - 1)]
    t_r = a_r // TOP_K
    k_r = a_r - t_r * TOP_K

    def spare_row(q):
        in_gap = q < TOP_K * SLAB_GAP
        gap_row = (q // SLAB_GAP) * T_PAD + T_ALL + q % SLAB_GAP
        return jnp.where(in_gap, gap_row, TAIL0 + q - TOP_K * SLAB_GAP)

    pad_rank = jnp.cumsum((~valid).astype(jnp.int32)) - 1
    dst = jnp.where(valid, k_r * T_PAD + t_r, spare_row(2 * MOE_BLK + pad_rank)).astype(jnp.int32)
    warmup = spare_row(jnp.arange(2 * MOE_BLK, dtype=jnp.int32)).astype(jnp.int32).reshape(2, MOE_BLK)
    dst_rows = jnp.concatenate([warmup, dst.reshape(N_BLOCKS, MOE_BLK)], axis=0)
    blocks = jnp.arange(N_BLOCKS, dtype=jnp.int32)
    first = jnp.concatenate([jnp.ones((1,), jnp.bool_), block_exp[1:] != block_exp[:-1]])
    run_slot = (jnp.cumsum(first.astype(jnp.int32)) - 1) % 2
    later_first = first[None, :] & (blocks[None, :] > blocks[:, None])
    next_first = jnp.min(jnp.where(later_first, blocks[None, :], N_BLOCKS), axis=1)
    has_next = next_first < N_BLOCKS
    next_exp = lookup(block_exp, jnp.minimum(next_first, N_BLOCKS - 1))
    n_used = jnp.broadcast_to(pend[-1] // MOE_BLK, (N_BLOCKS,))
    plan = jnp.stack([block_exp, first.astype(jnp.int32), run_slot, next_exp, has_next.astype(jnp.int32),
                      n_used]).astype(jnp.int32)
    return plan, disp * ROW_TILE, pad_rows * ROW_TILE, dst_rows * ROW_TILE


def _dispatch_kernel(disp_hbm, pads_hbm, h2_ref, xs_hbm, tbl0, tbl1, pad_tbl, zero_tile, rsem, tsem, psem):
    s = pl.program_id(0)
    ns = pl.num_programs(0)
    tbls = (tbl0, tbl1)

    def table_copy(row, p):
        return pltpu.make_async_copy(disp_hbm.at[row], tbls[p], tsem.at[p])

    def row_copy(off, r, k):
        return pltpu.make_async_copy(h2_ref.at[pl.ds(r * ROW_TILE, ROW_TILE), :],
                                     xs_hbm.at[pl.ds(off, ROW_TILE), :], rsem.at[k])

    def pad_copy(off):
        return pltpu.make_async_copy(zero_tile, xs_hbm.at[pl.ds(off, ROW_TILE), :], psem.at[0])

    def scatter_rows(p, n_rows):
        for r in range(n_rows):
            for k in range(TOP_K):
                off = pl.multiple_of(tbls[p][k * DISP_TILE + r], ROW_TILE)
                row_copy(off, r, k).start(priority=(r + k) % 2)
        for r in range(n_rows):
            for k in range(TOP_K):
                row_copy(0, r, k).wait()

    def fill_pads():
        zero_tile[...] = jnp.zeros_like(zero_tile)
        load = pltpu.make_async_copy(pads_hbm, pad_tbl, psem.at[1])
        load.start()
        load.wait()

        def chunk(i, carry):
            for j in range(PAD_CHUNK):
                pad_copy(pl.multiple_of(pad_tbl[i * PAD_CHUNK + j], ROW_TILE)).start(priority=j % 2)

            @pl.when(i > 0)
            def _():
                for j in range(PAD_CHUNK):
                    pad_copy(0).wait()
            return carry

        lax.fori_loop(0, N_PAD_ROWS // PAD_CHUNK, chunk, 0)
        for j in range(PAD_CHUNK):
            pad_copy(0).wait()

    @pl.when(s == 0)
    def _():
        table_copy(0, 0).start()

    def step(p):
        table_copy(s, p).wait()

        @pl.when(s < ns - 1)
        def _():
            table_copy(s + 1, 1 - p).start()
            scatter_rows(p, DISP_TILE)

        @pl.when(s == ns - 1)
        def _():
            scatter_rows(p, DISP_LAST_ROWS)
            fill_pads()

    parity = lax.rem(s, 2)
    pl.when(parity == 0)(lambda: step(0))
    pl.when(parity == 1)(lambda: step(1))


def _dispatch(disp, pad_rows, h2):
    return pl.pallas_call(
        _dispatch_kernel,
        grid=(DISP_STEPS,),
        in_specs=[pl.BlockSpec(memory_space=pl.ANY),
                  pl.BlockSpec(memory_space=pl.ANY),
                  pl.BlockSpec((DISP_TILE * ROW_TILE, LANES), lambda s: (s, 0))],
        out_specs=pl.BlockSpec(memory_space=pl.ANY),
        out_shape=jax.ShapeDtypeStruct((N_BLOCKS * MOE_BLK * ROW_TILE, LANES), _F32),
        scratch_shapes=[
            pltpu.SMEM((TOP_K * DISP_TILE,), jnp.int32),
            pltpu.SMEM((TOP_K * DISP_TILE,), jnp.int32),
            pltpu.SMEM((N_PAD_ROWS,), jnp.int32),
            pltpu.VMEM((ROW_TILE, LANES), _F32),
            pltpu.SemaphoreType.DMA((TOP_K,)),
            pltpu.SemaphoreType.DMA((2,)),
            pltpu.SemaphoreType.DMA((2,)),
        ],
        compiler_params=pltpu.CompilerParams(
            dimension_semantics=("arbitrary",), vmem_limit_bytes=VMEM_LIMIT),
        name="dispatch",
    )(disp, pad_rows, h2)


PLAN_EXPERT, PLAN_FIRST, PLAN_SLOT, PLAN_NEXT, PLAN_HAS_NEXT, PLAN_USED = range(6)


def _moe_kernel(plan_ref, table_hbm, x_ref, wgu_hbm, bgu_ref, wdn_hbm, bdn_ref, out_hbm,
                ybuf0, ybuf1, tbl0, tbl1, gu_stage0, gu_stage1, dn_stage0, dn_stage1, wgu_bf, wdn_bf,
                ssem, tsem, wsem):
    b = pl.program_id(0)
    nb = pl.num_programs(0)
    ybufs, tbls = (ybuf0, ybuf1), (tbl0, tbl1)
    gu_stages, dn_stages = (gu_stage0, gu_stage1), (dn_stage0, dn_stage1)
    n_used = plan_ref[PLAN_USED, 0]

    def table_copy(row, p):
        return pltpu.make_async_copy(table_hbm.at[row], tbls[p], tsem.at[p])

    def weight_copies(e, w):
        return (pltpu.make_async_copy(wgu_hbm.at[e], gu_stages[w], wsem.at[2 * w]),
                pltpu.make_async_copy(wdn_hbm.at[e], dn_stages[w], wsem.at[2 * w + 1]))

    def scatter_row(off, r, p):
        return pltpu.make_async_copy(ybufs[p].at[pl.ds(r * ROW_TILE, ROW_TILE), :],
                                     out_hbm.at[pl.ds(off, ROW_TILE), :], ssem.at[p])

    rows_per_chunk = MOE_BLK // FF_CHUNKS

    def start_scatter(tp, p, lo, hi):
        for r in range(lo, hi):
            scatter_row(pl.multiple_of(tbls[tp][r], ROW_TILE), r, p).start(priority=r % 2)

    def wait_scatter(p):
        for r in range(MOE_BLK):
            scatter_row(0, r, p).wait()

    @pl.when(b == 0)
    def _():
        first = table_copy(0, 0)
        first.start()
        first.wait()
        ybuf0[...] = jnp.zeros_like(ybuf0)
        ybuf1[...] = jnp.zeros_like(ybuf1)
        start_scatter(0, 0, 0, MOE_BLK)
        table_copy(1, 1).start()
        for c in weight_copies(plan_ref[PLAN_EXPERT, 0], 0):
            c.start()

    def new_expert(w):
        for c in weight_copies(0, w):
            c.wait()

        @pl.when(plan_ref[PLAN_HAS_NEXT, b] == 1)
        def _():
            for c in weight_copies(plan_ref[PLAN_NEXT, b], 1 - w):
                c.start()

        wgu_bf[...] = gu_stages[w][...].astype(_BF16)
        wdn_bf[...] = dn_stages[w][...].astype(_BF16)

    is_first = plan_ref[PLAN_FIRST, b] == 1
    wslot = plan_ref[PLAN_SLOT, b]
    pl.when(is_first & (wslot == 0))(lambda: new_expert(0))
    pl.when(is_first & (wslot == 1))(lambda: new_expert(1))

    def step(p):
        q = 1 - p
        table_copy(b + 1, q).wait()

        @pl.when(b < n_used)
        def _():
            x = _load_row_tiles(x_ref, MOE_BLK).astype(_BF16)
            y = None
            for c in range(FF_CHUNKS):
                start_scatter(q, q, c * rows_per_chunk, (c + 1) * rows_per_chunk)
                gcols = slice(c * FF_CHUNK, (c + 1) * FF_CHUNK)
                ucols = slice(D_FF + c * FF_CHUNK, D_FF + (c + 1) * FF_CHUNK)
                gate = jnp.dot(x, wgu_bf[:, gcols], preferred_element_type=_F32) + bgu_ref[:, gcols]
                up = jnp.dot(x, wgu_bf[:, ucols], preferred_element_type=_F32) + bgu_ref[:, ucols]
                gate = jnp.minimum(gate, SWIGLU_LIMIT)
                up = jnp.clip(up, -SWIGLU_LIMIT, SWIGLU_LIMIT)
                act = gate * jax.nn.sigmoid(SWIGLU_ALPHA * gate) * (up + 1.0)
                part = jnp.dot(act.astype(_BF16), wdn_bf[gcols, :], preferred_element_type=_F32)
                y = part if y is None else y + part
            table_copy(b + 2, p).start()
            wait_scatter(p)
            _store_row_tiles(ybufs[p], y + bdn_ref[...])

        @pl.when(b >= n_used)
        def _():
            start_scatter(q, q, 0, MOE_BLK)
            table_copy(b + 2, p).start()
            wait_scatter(p)
            ybufs[p][...] = jnp.zeros_like(ybufs[p])

        @pl.when(b == nb - 1)
        def _():
            table_copy(b + 2, p).wait()
            start_scatter(p, p, 0, MOE_BLK)
            wait_scatter(p)
            wait_scatter(q)

    parity = lax.rem(b, 2)
    pl.when(parity == 0)(lambda: step(0))
    pl.when(parity == 1)(lambda: step(1))


def _moe_experts(plan, table, xs, wgu, bgu, wdn, bdn):
    grid_spec = pltpu.PrefetchScalarGridSpec(
        num_scalar_prefetch=1,
        grid=(N_BLOCKS,),
        in_specs=[
            pl.BlockSpec(memory_space=pl.ANY),
            pl.BlockSpec((MOE_BLK * ROW_TILE, LANES), lambda b, plan: (b, 0)),
            pl.BlockSpec(memory_space=pl.ANY),
            pl.BlockSpec((None, 1, 2 * D_FF), lambda b, plan: (plan[PLAN_EXPERT, b], 0, 0)),
            pl.BlockSpec(memory_space=pl.ANY),
            pl.BlockSpec((None, 1, D_MODEL), lambda b, plan: (plan[PLAN_EXPERT, b], 0, 0)),
        ],
        out_specs=pl.BlockSpec(memory_space=pl.ANY),
        scratch_shapes=[
            pltpu.VMEM((MOE_BLK * ROW_TILE, LANES), _F32),
            pltpu.VMEM((MOE_BLK * ROW_TILE, LANES), _F32),
            pltpu.SMEM((MOE_BLK,), jnp.int32),
            pltpu.SMEM((MOE_BLK,), jnp.int32),
            pltpu.VMEM((D_MODEL, 2 * D_FF), _F32),
            pltpu.VMEM((D_MODEL, 2 * D_FF), _F32),
            pltpu.VMEM((D_FF, D_MODEL), _F32),
            pltpu.VMEM((D_FF, D_MODEL), _F32),
            pltpu.VMEM((D_MODEL, 2 * D_FF), _BF16),
            pltpu.VMEM((D_FF, D_MODEL), _BF16),
            pltpu.SemaphoreType.DMA((2,)),
            pltpu.SemaphoreType.DMA((2,)),
            pltpu.SemaphoreType.DMA((4,)),
        ],
    )
    return pl.pallas_call(
        _moe_kernel,
        grid_spec=grid_spec,
        out_shape=jax.ShapeDtypeStruct((OUT_ROWS * ROW_TILE, LANES), _F32),
        compiler_params=pltpu.CompilerParams(
            dimension_semantics=("arbitrary",), vmem_limit_bytes=VMEM_LIMIT),
        name="moe_experts",
    )(plan, table, xs, wgu, bgu, wdn, bdn)


def _combine_kernel(x1_ref, g_ref, o0_ref, o1_ref, o2_ref, o3_ref, gfin_ref, yp_ref, ys_ref):
    i = pl.program_id(0)
    g = g_ref[...]
    o = [_load_row_tiles(r, COMB_TILE) for r in (o0_ref, o1_ref, o2_ref, o3_ref)]
    moe = (g[:, 0:1] * o[0] + g[:, 1:2] * o[1]) + (g[:, 2:3] * o[2] + g[:, 3:4] * o[3])
    y = _rms(x1_ref[...] + moe, gfin_ref[...])

    @pl.when(i < COMB_PROMPT_STEPS)
    def _():
        yp_ref[...] = y

    @pl.when(i == COMB_PROMPT_STEPS)
    def _():
        ys_ref[...] = y[:DEC_BATCH, :]


def _combine(x1, gates_tk, out4, g_final):
    slab_blocks = T_PAD // COMB_TILE

    def slab_spec(k):
        return pl.BlockSpec((COMB_TILE * ROW_TILE, LANES), lambda i, k=k: (k * slab_blocks + i, 0))

    return pl.pallas_call(
        _combine_kernel,
        grid=(COMB_PROMPT_STEPS + 1,),
        in_specs=[pl.BlockSpec((COMB_TILE, D_MODEL), lambda i: (i, 0)),
                  pl.BlockSpec((COMB_TILE, TOP_K), lambda i: (i, 0)),
                  slab_spec(0), slab_spec(1), slab_spec(2), slab_spec(3),
                  _const_spec(g_final.shape)],
        out_specs=[pl.BlockSpec((COMB_TILE, D_MODEL), lambda i: (jnp.minimum(i, COMB_PROMPT_STEPS - 1), 0)),
                   _const_spec((DEC_BATCH, D_MODEL))],
        out_shape=[jax.ShapeDtypeStruct((T_PROMPT, D_MODEL), _F32),
                   jax.ShapeDtypeStruct((DEC_BATCH, D_MODEL), _F32)],
        compiler_params=pltpu.CompilerParams(
            dimension_semantics=("arbitrary",), vmem_limit_bytes=VMEM_LIMIT),
        name="combine",
    )(x1, gates_tk, out4, out4, out4, out4, g_final)


def kernel(x_prompt, x_sample, state_pool, state_conv, meta_tokens, g_mix, w_in, w_pool_grp, pool_scale,
           w_pool_up, w_conv, w_conv_out, w_o, g_ffn, w_router, b_router, w_gu, b_gu, w_down, b_down, g_final):
    assert g_mix.shape[0] == 1, "single-layer step"
    grp = w_pool_grp[0].astype(_BF16)
    zero = jnp.zeros((POOL_GROUP, POOL_GROUP), _BF16)
    wgrp = jnp.stack([jnp.block([[grp[0], zero], [zero, grp[1]]]),
                      jnp.block([[grp[2], zero], [zero, grp[3]]])])
    w = (
        g_mix[0].reshape(1, D_MODEL),
        w_in[0].astype(_BF16),
        wgrp,
        pool_scale[0].reshape(1, D_POOL),
        w_pool_up[0].astype(_BF16),
        w_conv[0],
        w_conv_out[0].astype(_BF16),
        w_o[0].astype(_BF16),
        g_ffn[0].reshape(1, D_MODEL),
        w_router[0].T,
        b_router[0].reshape(N_EXPERTS, 1),
    )
    xs_pad = jnp.pad(x_sample.reshape(DEC_BATCH, D_MODEL), ((0, MIX_TILE - DEC_BATCH), (0, 0)))
    spool_t = jnp.transpose(state_pool[0], (1, 0, 2))
    sconv_t = jnp.transpose(state_conv[0], (1, 0, 2))
    x1, h2, idx, gate, rank, cnt, pool_p, conv_p, u_s, cv_s = _mixer(
        x_prompt, xs_pad, spool_t, sconv_t, meta_tokens, w)

    plan, disp, pad_rows, dst_rows = _routing_tables(idx, rank, cnt)
    xs = _dispatch(disp, pad_rows, h2)
    out4 = _moe_experts(plan, dst_rows, xs,
                        w_gu.reshape(N_EXPERTS, D_MODEL, 2 * D_FF), b_gu.reshape(N_EXPERTS, 1, 2 * D_FF),
                        w_down.reshape(N_EXPERTS, D_FF, D_MODEL), b_down.reshape(N_EXPERTS, 1, D_MODEL))

    y_p, y_s = _combine(x1, gate.T, out4, g_final.reshape(1, D_MODEL))

    new_pool_p = pool_p[:, POOL_HALO - POOL_BUF:, :][None]
    new_conv_p = conv_p[:, CONV_HALO - CONV_BUF:, :][None]
    new_pool_s = jnp.concatenate([state_pool[0][:, 1:, :], u_s[:, None, :]], axis=1)[None]
    new_conv_s = jnp.concatenate([state_conv[0][:, 1:, :], cv_s[:, None, :]], axis=1)[None]
    return (y_p.reshape(BATCH, SEQ, D_MODEL), y_s.reshape(DEC_BATCH, 1, D_MODEL),
            new_pool_p, new_conv_p, new_pool_s, new_conv_s)
```

```python
import jax
import jax.numpy as jnp
from jax import lax
from jax.experimental import pallas as pl
from jax.experimental.pallas import tpu as pltpu

D_MODEL = 1024
BATCH = 8
SEQ = 2048
DEC_BATCH = 128
N_META = 16
D_POOL = 512
POOL_GROUP = 128
POOL_WINDOWS = (2, 4, 8, 16)
POOL_BUF = 15
D_CONV = 512
CONV_BUF = 2
N_EXPERTS = 32
TOP_K = 4
D_FF = 1024
SWIGLU_LIMIT = 7.0
SWIGLU_ALPHA = 1.702
RMS_EPS = 1e-5

O_CG = D_POOL
O_BG = O_CG + D_CONV
O_V = O_BG + D_CONV
O_GL = O_V + D_CONV

T_PROMPT = BATCH * SEQ
T_ALL = T_PROMPT + DEC_BATCH
N_ASSIGN = TOP_K * T_ALL

MIX_TILE = 512
MIX_PROMPT_STEPS = T_PROMPT // MIX_TILE
SEQ_TILES = SEQ // MIX_TILE
POOL_HALO = 16
CONV_HALO = 8
DISP_TILE = 256
DISP_STEPS = -(-T_ALL // DISP_TILE)
DISP_LAST_ROWS = T_ALL - (DISP_STEPS - 1) * DISP_TILE
MOE_BLK = 256
N_BLOCKS = -(-N_ASSIGN // MOE_BLK) + N_EXPERTS
N_PAD_ROWS = N_BLOCKS * MOE_BLK - N_ASSIGN
PAD_CHUNK = 64
assert N_PAD_ROWS % PAD_CHUNK == 0
ROW_DMA_PRIORITY = 0
WEIGHT_DMA_PRIORITY = 1
COMB_TILE = 512
COMB_PROMPT_STEPS = T_PROMPT // COMB_TILE
T_PAD = (T_ALL + COMB_TILE - 1) // COMB_TILE * COMB_TILE
SLAB_GAP = T_PAD - T_ALL
OUT_ROWS = (N_BLOCKS + 2) * MOE_BLK
TAIL0 = TOP_K * T_PAD
assert OUT_ROWS >= TAIL0

ROW_TILE = 8
LANES = D_MODEL // ROW_TILE

VMEM_LIMIT = 56 * 1024 * 1024

_F32 = jnp.float32
_BF16 = jnp.bfloat16


def _rms(x, g):
    ms = jnp.mean(x * x, axis=-1, keepdims=True)
    return x * lax.rsqrt(ms + RMS_EPS) * g


def _bdot(a, w):
    return jnp.dot(a.astype(_BF16), w, preferred_element_type=_F32)


def _store_row_tiles(ref, val):
    rows = val.shape[0]
    for c in range(ROW_TILE):
        ref[pl.ds(c, rows, stride=ROW_TILE), :] = val[:, c * LANES:(c + 1) * LANES]


def _load_row_tiles(ref, rows):
    return jnp.concatenate([ref[pl.ds(c, rows, stride=ROW_TILE), :] for c in range(ROW_TILE)], axis=-1)


def _branches(z, pm, conv, wgrp_ref, pscale_ref, wup_ref, wcout_ref):
    pmb = pm.astype(_BF16)
    half = 2 * POOL_GROUP
    pg = jnp.concatenate(
        [jnp.dot(pmb[:, :half], wgrp_ref[0], preferred_element_type=_F32),
         jnp.dot(pmb[:, half:], wgrp_ref[1], preferred_element_type=_F32)], axis=-1)
    branch_a = _bdot(pg * pscale_ref[...], wup_ref[...])
    branch_b = _bdot(z[:, O_BG:O_V] * conv, wcout_ref[...])
    return branch_a, branch_b


def _merge_and_route(x, z, branch_a, branch_b, wo_ref, gffn_ref, wrt_ref, br_ref, base_ref, n_live):
    gates = jax.nn.sigmoid(z[:, O_GL:])
    merged = gates[:, :D_MODEL] * branch_a + gates[:, D_MODEL:] * branch_b
    x1 = x + _bdot(merged, wo_ref[...])
    h2 = _rms(x1, gffn_ref[...])
    logits = lax.dot_general(wrt_ref[...], h2, (((1,), (1,)), ((), ())),
                             precision=lax.Precision.HIGHEST,
                             preferred_element_type=_F32) + br_ref[...]
    iota = lax.broadcasted_iota(jnp.int32, logits.shape, 0)
    vals, idxs = [], []
    cur = logits
    for _ in range(TOP_K):
        m = jnp.max(cur, axis=0, keepdims=True)
        ik = jnp.min(jnp.where(cur == m, iota, N_EXPERTS), axis=0, keepdims=True)
        vals.append(m)
        idxs.append(ik)
        cur = jnp.where(iota == ik, -jnp.inf, cur)
    exps = [jnp.exp(v - vals[0]) for v in vals]
    denom = (exps[0] + exps[1]) + (exps[2] + exps[3])
    gate = [e / denom for e in exps]
    rows = logits.shape[1]
    live = lax.broadcasted_iota(jnp.int32, logits.shape, 1) < n_live
    onehots = [jnp.where(live, jnp.where(iota == ik, 1.0, 0.0), 0.0) for ik in idxs]
    member = (onehots[0] + onehots[1]) + (onehots[2] + onehots[3])
    earlier = (lax.broadcasted_iota(jnp.int32, (rows, rows), 0)
               < lax.broadcasted_iota(jnp.int32, (rows, rows), 1))
    before = jnp.dot(member.astype(_BF16), jnp.where(earlier, 1.0, 0.0).astype(_BF16),
                     preferred_element_type=_F32)
    pos = base_ref[...] + before
    ranks = [jnp.sum(oh * pos, axis=0, keepdims=True).astype(jnp.int32) for oh in onehots]
    base_ref[...] = base_ref[...] + jnp.sum(member, axis=1, keepdims=True)
    return x1, h2, idxs, gate, ranks


def _mixer_kernel(xp_ref, xs_ref, spool_ref, sconv_ref, meta_ref, gmix_ref, win_ref, wgrp_ref, pscale_ref,
                  wup_ref, wconv_ref, wcout_ref, wo_ref, gffn_ref, wrt_ref, br_ref,
                  x1_ref, h2_ref, idx_ref, gate_ref, rank_ref, cnt_ref, pool_out_ref, conv_out_ref, us_ref, cvs_ref,
                  pool_ext, conv_ext, meta_pool, meta_conv, pm_scr, conv_scr, base_scr):
    s = pl.program_id(0)
    is_prompt = s < MIX_PROMPT_STEPS
    j = lax.rem(s, SEQ_TILES)

    @pl.when(s == 0)
    def _():
        hm = _rms(meta_ref[...], gmix_ref[...]).astype(_BF16)
        zm = jnp.dot(hm, win_ref[:, :O_GL], preferred_element_type=_F32)
        meta_pool[...] = zm[:, :D_POOL]
        meta_conv[...] = zm[:, O_CG:O_BG] * zm[:, O_V:O_GL]
        base_scr[...] = jnp.zeros_like(base_scr)

    @pl.when(is_prompt & (j == 0))
    def _():
        pool_ext[0:POOL_HALO, :] = meta_pool[...]
        conv_ext[0:CONV_HALO, :] = meta_conv[N_META - CONV_HALO:, :]

    x = jnp.where(is_prompt, xp_ref[...], xs_ref[...])
    h = _rms(x, gmix_ref[...]).astype(_BF16)
    z = jnp.dot(h, win_ref[...], preferred_element_type=_F32)
    u = z[:, :D_POOL]
    cv = z[:, O_CG:O_BG] * z[:, O_V:O_GL]
    wc = wconv_ref[...]

    @pl.when(is_prompt)
    def _():
        pool_ext[POOL_HALO:, :] = u
        ext = pool_ext[...]
        for g, w in enumerate(POOL_WINDOWS):
            lanes = slice(g * POOL_GROUP, (g + 1) * POOL_GROUP)
            acc = ext[:, lanes]
            sh = 1
            while sh < w:
                acc = acc + pltpu.roll(acc, sh, 0)
                sh *= 2
            pm_scr[:, lanes] = acc[POOL_HALO:, :] * (1.0 / w) - u[:, lanes]
        conv_ext[CONV_HALO:, :] = cv
        cext = conv_ext[...]
        conv_scr[...] = (wc[0:1, :] * pltpu.roll(cext, 2, 0)[CONV_HALO:, :]
                         + wc[1:2, :] * pltpu.roll(cext, 1, 0)[CONV_HALO:, :]) + wc[2:3, :] * cv
        pool_ext[0:POOL_HALO, :] = u[MIX_TILE - POOL_HALO:, :]
        conv_ext[0:CONV_HALO, :] = cv[MIX_TILE - CONV_HALO:, :]

    @pl.when(is_prompt & (j == SEQ_TILES - 1))
    def _():
        pool_out_ref[...] = u[MIX_TILE - POOL_HALO:, :]
        conv_out_ref[...] = cv[MIX_TILE - CONV_HALO:, :]

    @pl.when(jnp.logical_not(is_prompt))
    def _():
        us = u[:DEC_BATCH, :]
        cvs = cv[:DEC_BATCH, :]
        pm_scr[...] = jnp.zeros_like(pm_scr)
        conv_scr[...] = jnp.zeros_like(conv_scr)
        for g, w in enumerate(POOL_WINDOWS):
            lanes = slice(g * POOL_GROUP, (g + 1) * POOL_GROUP)
            acc = us[:, lanes]
            for t in range(POOL_BUF - (w - 1), POOL_BUF):
                acc = acc + spool_ref[t, :, lanes]
            pm_scr[0:DEC_BATCH, lanes] = acc * (1.0 / w) - us[:, lanes]
        conv_scr[0:DEC_BATCH, :] = (wc[0:1, :] * sconv_ref[0] + wc[1:2, :] * sconv_ref[1]) + wc[2:3, :] * cvs
        us_ref[...] = us
        cvs_ref[...] = cvs

    branch_a, branch_b = _branches(z, pm_scr[...], conv_scr[...], wgrp_ref, pscale_ref, wup_ref, wcout_ref)
    n_live = jnp.where(is_prompt, MIX_TILE, DEC_BATCH)
    x1, h2, idxs, gate, ranks = _merge_and_route(x, z, branch_a, branch_b, wo_ref, gffn_ref, wrt_ref, br_ref,
                                                 base_scr, n_live)
    x1_ref[...] = x1
    _store_row_tiles(h2_ref, h2)
    for k in range(TOP_K):
        idx_ref[k:k + 1, :] = idxs[k]
        gate_ref[k:k + 1, :] = gate[k]
        rank_ref[k:k + 1, :] = ranks[k]

    @pl.when(jnp.logical_not(is_prompt))
    def _():
        cnt_ref[...] = jnp.broadcast_to(base_scr[...], cnt_ref.shape)


def _const_spec(shape):
    nd = len(shape)
    return pl.BlockSpec(shape, lambda *_: (0,) * nd)


def _resident_spec(shape):
    nd = len(shape)
    return pl.BlockSpec(shape, lambda *_: (0,) * nd, pipeline_mode=pl.Buffered(1))


def _mixer(x_prompt, xs_pad, spool_t, sconv_t, meta, w):
    last = MIX_PROMPT_STEPS - 1

    def x_map(s):
        sp = jnp.minimum(s, last)
        return (sp // SEQ_TILES, sp % SEQ_TILES, 0)

    def seq_map(s):
        return (jnp.minimum(s, last) // SEQ_TILES, 0, 0)

    consts = (xs_pad, spool_t, sconv_t, meta) + tuple(w)
    return pl.pallas_call(
        _mixer_kernel,
        grid=(MIX_PROMPT_STEPS + 1,),
        in_specs=[pl.BlockSpec((None, MIX_TILE, D_MODEL), x_map)] + [_resident_spec(a.shape) for a in consts],
        out_specs=[
            pl.BlockSpec((MIX_TILE, D_MODEL), lambda s: (s, 0)),
            pl.BlockSpec((MIX_TILE * ROW_TILE, LANES), lambda s: (s, 0)),
            pl.BlockSpec((TOP_K, MIX_TILE), lambda s: (0, s)),
            pl.BlockSpec((TOP_K, MIX_TILE), lambda s: (0, s)),
            pl.BlockSpec((TOP_K, MIX_TILE), lambda s: (0, s)),
            _const_spec((N_EXPERTS, 128)),
            pl.BlockSpec((None, POOL_HALO, D_POOL), seq_map),
            pl.BlockSpec((None, CONV_HALO, D_CONV), seq_map),
            _const_spec((DEC_BATCH, D_POOL)),
            _const_spec((DEC_BATCH, D_CONV)),
        ],
        out_shape=[
            jax.ShapeDtypeStruct((T_ALL, D_MODEL), _F32),
            jax.ShapeDtypeStruct((T_ALL * ROW_TILE, LANES), _F32),
            jax.ShapeDtypeStruct((TOP_K, T_ALL), jnp.int32),
            jax.ShapeDtypeStruct((TOP_K, T_ALL), _F32),
            jax.ShapeDtypeStruct((TOP_K, T_ALL), jnp.int32),
            jax.ShapeDtypeStruct((N_EXPERTS, 128), _F32),
            jax.ShapeDtypeStruct((BATCH, POOL_HALO, D_POOL), _F32),
            jax.ShapeDtypeStruct((BATCH, CONV_HALO, D_CONV), _F32),
            jax.ShapeDtypeStruct((DEC_BATCH, D_POOL), _F32),
            jax.ShapeDtypeStruct((DEC_BATCH, D_CONV), _F32),
        ],
        scratch_shapes=[
            pltpu.VMEM((POOL_HALO + MIX_TILE, D_POOL), _F32),
            pltpu.VMEM((CONV_HALO + MIX_TILE, D_CONV), _F32),
            pltpu.VMEM((N_META, D_POOL), _F32),
            pltpu.VMEM((N_META, D_CONV), _F32),
            pltpu.VMEM((MIX_TILE, D_POOL), _F32),
            pltpu.VMEM((MIX_TILE, D_CONV), _F32),
            pltpu.VMEM((N_EXPERTS, 1), _F32),
        ],
        compiler_params=pltpu.CompilerParams(
            dimension_semantics=("arbitrary",), vmem_limit_bytes=VMEM_LIMIT),
        name="mixer",
    )(x_prompt, *consts)


def _routing_tables(idx, rank, cnt):
    def lookup(table, keys):
        ids = jnp.arange(table.shape[0], dtype=jnp.int32)
        return jnp.sum(jnp.where(keys[..., None] == ids, table, 0), axis=-1)

    counts = cnt[:, 0].astype(jnp.int32)
    padded = (counts + MOE_BLK - 1) // MOE_BLK * MOE_BLK
    pend = jnp.cumsum(padded)
    pstart = pend - padded
    start = jnp.cumsum(counts) - counts

    dest = lookup(pstart, idx) + rank
    dest = jnp.pad(dest, ((0, 0), (0, DISP_STEPS * DISP_TILE - T_ALL)))
    disp = dest.reshape(TOP_K, DISP_STEPS, DISP_TILE).transpose(1, 0, 2).reshape(DISP_STEPS, TOP_K * DISP_TILE)

    n_pad = padded - counts
    pad_cum = jnp.cumsum(n_pad)
    q = jnp.arange(N_PAD_ROWS, dtype=jnp.int32)
    e_q = jnp.sum((q[:, None] >= pad_cum[None, :]).astype(jnp.int32), axis=1)
    first_pad = lookup(jnp.concatenate([pstart + counts, pend[-1:]]), e_q)
    pads_before = lookup(jnp.concatenate([pad_cum - n_pad, pad_cum[-1:]]), e_q)
    pad_rows = first_pad + q - pads_before

    e_flat = idx.T.reshape(N_ASSIGN)
    order = jnp.argsort(e_flat, stable=True).astype(jnp.int32)
    block_start = jnp.arange(N_BLOCKS, dtype=jnp.int32) * MOE_BLK
    block_exp = jnp.minimum(jnp.sum((block_start[:, None] >= pend[None, :]).astype(jnp.int32), axis=1),
                            N_EXPERTS - 1)
    in_block = jnp.arange(MOE_BLK, dtype=jnp.int32)[None, :]
    jr = (block_start - lookup(pstart, block_exp))[:, None] + in_block
    valid = (jr < lookup(counts, block_exp)[:, None]).reshape(N_BLOCKS * MOE_BLK)
    sorted_pos = (lookup(start, block_exp)[:, None] + jr).reshape(N_BLOCKS * MOE_BLK)
    a_r = order[jnp.clip(sorted_pos, 0, N_ASSIGN - 1)]
    t_r = a_r // TOP_K
    k_r = a_r - t_r * TOP_K

    def spare_row(q):
        in_gap = q < TOP_K * SLAB_GAP
        gap_row = (q // SLAB_GAP) * T_PAD + T_ALL + q % SLAB_GAP
        return jnp.where(in_gap, gap_row, TAIL0 + q - TOP_K * SLAB_GAP)

    pad_rank = jnp.cumsum((~valid).astype(jnp.int32)) - 1
    dst = jnp.where(valid, k_r * T_PAD + t_r, spare_row(2 * MOE_BLK + pad_rank)).astype(jnp.int32)
    warmup = spare_row(jnp.arange(2 * MOE_BLK, dtype=jnp.int32)).astype(jnp.int32).reshape(2, MOE_BLK)
    dst_rows = jnp.concatenate([warmup, dst.reshape(N_BLOCKS, MOE_BLK)], axis=0)
    blocks = jnp.arange(N_BLOCKS, dtype=jnp.int32)
    first = jnp.concatenate([jnp.ones((1,), jnp.bool_), block_exp[1:] != block_exp[:-1]])
    run_slot = (jnp.cumsum(first.astype(jnp.int32)) - 1) % 2
    later_first = first[None, :] & (blocks[None, :] > blocks[:, None])
    next_first = jnp.min(jnp.where(later_first, blocks[None, :], N_BLOCKS), axis=1)
    has_next = next_first < N_BLOCKS
    next_exp = lookup(block_exp, jnp.minimum(next_first, N_BLOCKS - 1))
    n_used = jnp.broadcast_to(pend[-1] // MOE_BLK, (N_BLOCKS,))
    plan = jnp.stack([block_exp, first.astype(jnp.int32), run_slot, next_exp, has_next.astype(jnp.int32),
                      n_used]).astype(jnp.int32)
    return plan, disp * ROW_TILE, pad_rows * ROW_TILE, dst_rows * ROW_TILE


def _dispatch_kernel(disp_hbm, pads_hbm, h2_ref, xs_hbm, tbl0, tbl1, pad_tbl, zero_tile, rsem, tsem, psem):
    s = pl.program_id(0)
    ns = pl.num_programs(0)
    tbls = (tbl0, tbl1)

    def table_copy(row, p):
        return pltpu.make_async_copy(disp_hbm.at[row], tbls[p], tsem.at[p])

    def row_copy(off, r, k):
        return pltpu.make_async_copy(h2_ref.at[pl.ds(r * ROW_TILE, ROW_TILE), :],
                                     xs_hbm.at[pl.ds(off, ROW_TILE), :], rsem.at[k])

    def pad_copy(off):
        return pltpu.make_async_copy(zero_tile, xs_hbm.at[pl.ds(off, ROW_TILE), :], psem.at[0])

    def scatter_rows(p, n_rows):
        for r in range(n_rows):
            for k in range(TOP_K):
                off = pl.multiple_of(tbls[p][k * DISP_TILE + r], ROW_TILE)
                row_copy(off, r, k).start(priority=(r + k) % 2)
        for r in range(n_rows):
            for k in range(TOP_K):
                row_copy(0, r, k).wait()

    def fill_pads():
        zero_tile[...] = jnp.zeros_like(zero_tile)
        load = pltpu.make_async_copy(pads_hbm, pad_tbl, psem.at[1])
        load.start()
        load.wait()

        def chunk(i, carry):
            for j in range(PAD_CHUNK):
                pad_copy(pl.multiple_of(pad_tbl[i * PAD_CHUNK + j], ROW_TILE)).start(priority=j % 2)

            @pl.when(i > 0)
            def _():
                for j in range(PAD_CHUNK):
                    pad_copy(0).wait()
            return carry

        lax.fori_loop(0, N_PAD_ROWS // PAD_CHUNK, chunk, 0)
        for j in range(PAD_CHUNK):
            pad_copy(0).wait()

    @pl.when(s == 0)
    def _():
        table_copy(0, 0).start()

    def step(p):
        table_copy(s, p).wait()

        @pl.when(s < ns - 1)
        def _():
            table_copy(s + 1, 1 - p).start()
            scatter_rows(p, DISP_TILE)

        @pl.when(s == ns - 1)
        def _():
            scatter_rows(p, DISP_LAST_ROWS)
            fill_pads()

    parity = lax.rem(s, 2)
    pl.when(parity == 0)(lambda: step(0))
    pl.when(parity == 1)(lambda: step(1))


def _dispatch(disp, pad_rows, h2):
    return pl.pallas_call(
        _dispatch_kernel,
        grid=(DISP_STEPS,),
        in_specs=[pl.BlockSpec(memory_space=pl.ANY),
                  pl.BlockSpec(memory_space=pl.ANY),
                  pl.BlockSpec((DISP_TILE * ROW_TILE, LANES), lambda s: (s, 0))],
        out_specs=pl.BlockSpec(memory_space=pl.ANY),
        out_shape=jax.ShapeDtypeStruct((N_BLOCKS * MOE_BLK * ROW_TILE, LANES), _F32),
        scratch_shapes=[
            pltpu.SMEM((TOP_K * DISP_TILE,), jnp.int32),
            pltpu.SMEM((TOP_K * DISP_TILE,), jnp.int32),
            pltpu.SMEM((N_PAD_ROWS,), jnp.int32),
            pltpu.VMEM((ROW_TILE, LANES), _F32),
            pltpu.SemaphoreType.DMA((TOP_K,)),
            pltpu.SemaphoreType.DMA((2,)),
            pltpu.SemaphoreType.DMA((2,)),
        ],
        compiler_params=pltpu.CompilerParams(
            dimension_semantics=("arbitrary",), vmem_limit_bytes=VMEM_LIMIT),
        name="dispatch",
    )(disp, pad_rows, h2)


PLAN_EXPERT, PLAN_FIRST, PLAN_SLOT, PLAN_NEXT, PLAN_HAS_NEXT, PLAN_USED = range(6)


def _moe_kernel(plan_ref, table_hbm, x_ref, wgu_hbm, bgu_ref, wdn_hbm, bdn_ref, out_hbm,
                ybuf0, ybuf1, tbl0, tbl1, gu_stage0, gu_stage1, dn_stage0, dn_stage1, wgu_bf, wdn_bf,
                ssem, tsem, wsem):
    b = pl.program_id(0)
    nb = pl.num_programs(0)
    ybufs, tbls = (ybuf0, ybuf1), (tbl0, tbl1)
    gu_stages, dn_stages = (gu_stage0, gu_stage1), (dn_stage0, dn_stage1)
    n_used = plan_ref[PLAN_USED, 0]

    def table_copy(row, p):
        return pltpu.make_async_copy(table_hbm.at[row], tbls[p], tsem.at[p])

    def weight_copies(e, w):
        return (pltpu.make_async_copy(wgu_hbm.at[e], gu_stages[w], wsem.at[2 * w]),
                pltpu.make_async_copy(wdn_hbm.at[e], dn_stages[w], wsem.at[2 * w + 1]))

    def scatter_row(off, r, p):
        return pltpu.make_async_copy(ybufs[p].at[pl.ds(r * ROW_TILE, ROW_TILE), :],
                                     out_hbm.at[pl.ds(off, ROW_TILE), :], ssem.at[p])

    def start_scatter(tp, p):
        for r in range(MOE_BLK):
            scatter_row(pl.multiple_of(tbls[tp][r], ROW_TILE), r, p).start(priority=ROW_DMA_PRIORITY)

    def wait_scatter(p):
        for r in range(MOE_BLK):
            scatter_row(0, r, p).wait()

    @pl.when(b == 0)
    def _():
        first = table_copy(0, 0)
        first.start()
        first.wait()
        ybuf0[...] = jnp.zeros_like(ybuf0)
        ybuf1[...] = jnp.zeros_like(ybuf1)
        start_scatter(0, 0)
        table_copy(1, 1).start()
        for c in weight_copies(plan_ref[PLAN_EXPERT, 0], 0):
            c.start(priority=WEIGHT_DMA_PRIORITY)

    def new_expert(w):
        for c in weight_copies(0, w):
            c.wait()

        @pl.when(plan_ref[PLAN_HAS_NEXT, b] == 1)
        def _():
            for c in weight_copies(plan_ref[PLAN_NEXT, b], 1 - w):
                c.start(priority=WEIGHT_DMA_PRIORITY)

        wgu_bf[...] = gu_stages[w][...].astype(_BF16)
        wdn_bf[...] = dn_stages[w][...].astype(_BF16)

    is_first = plan_ref[PLAN_FIRST, b] == 1
    wslot = plan_ref[PLAN_SLOT, b]
    pl.when(is_first & (wslot == 0))(lambda: new_expert(0))
    pl.when(is_first & (wslot == 1))(lambda: new_expert(1))

    def step(p):
        q = 1 - p
        table_copy(b + 1, q).wait()

        @pl.when(b < n_used)
        def _():
            start_scatter(q, q)
            table_copy(b + 2, p).start()
            x = _load_row_tiles(x_ref, MOE_BLK).astype(_BF16)
            hgu = jnp.dot(x, wgu_bf[...], preferred_element_type=_F32) + bgu_ref[...]
            gate = jnp.minimum(hgu[:, :D_FF], SWIGLU_LIMIT)
            up = jnp.clip(hgu[:, D_FF:], -SWIGLU_LIMIT, SWIGLU_LIMIT)
            act = gate * jax.nn.sigmoid(SWIGLU_ALPHA * gate) * (up + 1.0)
            y = jnp.dot(act.astype(_BF16), wdn_bf[...], preferred_element_type=_F32) + bdn_ref[...]
            wait_scatter(p)
            _store_row_tiles(ybufs[p], y)

        @pl.when(b >= n_used)
        def _():
            start_scatter(q, q)
            table_copy(b + 2, p).start()
            wait_scatter(p)
            ybufs[p][...] = jnp.zeros_like(ybufs[p])

        @pl.when(b == nb - 1)
        def _():
            table_copy(b + 2, p).wait()
            start_scatter(p, p)
            wait_scatter(p)
            wait_scatter(q)

    parity = lax.rem(b, 2)
    pl.when(parity == 0)(lambda: step(0))
    pl.when(parity == 1)(lambda: step(1))


def _moe_experts(plan, table, xs, wgu, bgu, wdn, bdn):
    grid_spec = pltpu.PrefetchScalarGridSpec(
        num_scalar_prefetch=1,
        grid=(N_BLOCKS,),
        in_specs=[
            pl.BlockSpec(memory_space=pl.ANY),
            pl.BlockSpec((MOE_BLK * ROW_TILE, LANES), lambda b, plan: (b, 0)),
            pl.BlockSpec(memory_space=pl.ANY),
            pl.BlockSpec((None, 1, 2 * D_FF), lambda b, plan: (plan[PLAN_EXPERT, b], 0, 0)),
            pl.BlockSpec(memory_space=pl.ANY),
            pl.BlockSpec((None, 1, D_MODEL), lambda b, plan: (plan[PLAN_EXPERT, b], 0, 0)),
        ],
        out_specs=pl.BlockSpec(memory_space=pl.ANY),
        scratch_shapes=[
            pltpu.VMEM((MOE_BLK * ROW_TILE, LANES), _F32),
            pltpu.VMEM((MOE_BLK * ROW_TILE, LANES), _F32),
            pltpu.SMEM((MOE_BLK,), jnp.int32),
            pltpu.SMEM((MOE_BLK,), jnp.int32),
            pltpu.VMEM((D_MODEL, 2 * D_FF), _F32),
            pltpu.VMEM((D_MODEL, 2 * D_FF), _F32),
            pltpu.VMEM((D_FF, D_MODEL), _F32),
            pltpu.VMEM((D_FF, D_MODEL), _F32),
            pltpu.VMEM((D_MODEL, 2 * D_FF), _BF16),
            pltpu.VMEM((D_FF, D_MODEL), _BF16),
            pltpu.SemaphoreType.DMA((2,)),
            pltpu.SemaphoreType.DMA((2,)),
            pltpu.SemaphoreType.DMA((4,)),
        ],
    )
    return pl.pallas_call(
        _moe_kernel,
        grid_spec=grid_spec,
        out_shape=jax.ShapeDtypeStruct((OUT_ROWS * ROW_TILE, LANES), _F32),
        compiler_params=pltpu.CompilerParams(
            dimension_semantics=("arbitrary",), vmem_limit_bytes=VMEM_LIMIT),
        name="moe_experts",
    )(plan, table, xs, wgu, bgu, wdn, bdn)


def _combine_kernel(x1_ref, g_ref, o0_ref, o1_ref, o2_ref, o3_ref, gfin_ref, yp_ref, ys_ref):
    i = pl.program_id(0)
    g = g_ref[...]
    o = [_load_row_tiles(r, COMB_TILE) for r in (o0_ref, o1_ref, o2_ref, o3_ref)]
    moe = (g[:, 0:1] * o[0] + g[:, 1:2] * o[1]) + (g[:, 2:3] * o[2] + g[:, 3:4] * o[3])
    y = _rms(x1_ref[...] + moe, gfin_ref[...])

    @pl.when(i < COMB_PROMPT_STEPS)
    def _():
        yp_ref[...] = y

    @pl.when(i == COMB_PROMPT_STEPS)
    def _():
        ys_ref[...] = y[:DEC_BATCH, :]


def _combine(x1, gates_tk, out4, g_final):
    slab_blocks = T_PAD // COMB_TILE

    def slab_spec(k):
        return pl.BlockSpec((COMB_TILE * ROW_TILE, LANES), lambda i, k=k: (k * slab_blocks + i, 0))

    return pl.pallas_call(
        _combine_kernel,
        grid=(COMB_PROMPT_STEPS + 1,),
        in_specs=[pl.BlockSpec((COMB_TILE, D_MODEL), lambda i: (i, 0)),
                  pl.BlockSpec((COMB_TILE, TOP_K), lambda i: (i, 0)),
                  slab_spec(0), slab_spec(1), slab_spec(2), slab_spec(3),
                  _const_spec(g_final.shape)],
        out_specs=[pl.BlockSpec((COMB_TILE, D_MODEL), lambda i: (jnp.minimum(i, COMB_PROMPT_STEPS - 1), 0)),
                   _const_spec((DEC_BATCH, D_MODEL))],
        out_shape=[jax.ShapeDtypeStruct((T_PROMPT, D_MODEL), _F32),
                   jax.ShapeDtypeStruct((DEC_BATCH, D_MODEL), _F32)],
        compiler_params=pltpu.CompilerParams(
            dimension_semantics=("arbitrary",), vmem_limit_bytes=VMEM_LIMIT),
        name="combine",
    )(x1, gates_tk, out4, out4, out4, out4, g_final)


def kernel(x_prompt, x_sample, state_pool, state_conv, meta_tokens, g_mix, w_in, w_pool_grp, pool_scale,
           w_pool_up, w_conv, w_conv_out, w_o, g_ffn, w_router, b_router, w_gu, b_gu, w_down, b_down, g_final):
    assert g_mix.shape[0] == 1, "single-layer step"
    grp = w_pool_grp[0].astype(_BF16)
    zero = jnp.zeros((POOL_GROUP, POOL_GROUP), _BF16)
    wgrp = jnp.stack([jnp.block([[grp[0], zero], [zero, grp[1]]]),
                      jnp.block([[grp[2], zero], [zero, grp[3]]])])
    w = (
        g_mix[0].reshape(1, D_MODEL),
        w_in[0].astype(_BF16),
        wgrp,
        pool_scale[0].reshape(1, D_POOL),
        w_pool_up[0].astype(_BF16),
        w_conv[0],
        w_conv_out[0].astype(_BF16),
        w_o[0].astype(_BF16),
        g_ffn[0].reshape(1, D_MODEL),
        w_router[0].T,
        b_router[0].reshape(N_EXPERTS, 1),
    )
    xs_pad = jnp.pad(x_sample.reshape(DEC_BATCH, D_MODEL), ((0, MIX_TILE - DEC_BATCH), (0, 0)))
    spool_t = jnp.transpose(state_pool[0], (1, 0, 2))
    sconv_t = jnp.transpose(state_conv[0], (1, 0, 2))
    x1, h2, idx, gate, rank, cnt, pool_p, conv_p, u_s, cv_s = _mixer(
        x_prompt, xs_pad, spool_t, sconv_t, meta_tokens, w)

    plan, disp, pad_rows, dst_rows = _routing_tables(idx, rank, cnt)
    xs = _dispatch(disp, pad_rows, h2)
    out4 = _moe_experts(plan, dst_rows, xs,
                        w_gu.reshape(N_EXPERTS, D_MODEL, 2 * D_FF), b_gu.reshape(N_EXPERTS, 1, 2 * D_FF),
                        w_down.reshape(N_EXPERTS, D_FF, D_MODEL), b_down.reshape(N_EXPERTS, 1, D_MODEL))

    y_p, y_s = _combine(x1, gate.T, out4, g_final.reshape(1, D_MODEL))

    new_pool_p = pool_p[:, POOL_HALO - POOL_BUF:, :][None]
    new_conv_p = conv_p[:, CONV_HALO - CONV_BUF:, :][None]
    new_pool_s = jnp.concatenate([state_pool[0][:, 1:, :], u_s[:, None, :]], axis=1)[None]
    new_conv_s = jnp.concatenate([state_conv[0][:, 1:, :], cv_s[:, None, :]], axis=1)[None]
    return (y_p.reshape(BATCH, SEQ, D_MODEL), y_s.reshape(DEC_BATCH, 1, D_MODEL),
            new_pool_p, new_conv_p, new_pool_s, new_conv_s)
```

```python
import jax
import jax.numpy as jnp
from jax import lax
from jax.experimental import pallas as pl
from jax.experimental.pallas import tpu as pltpu

D_MODEL = 1024
BATCH = 8
SEQ = 2048
DEC_BATCH = 128
N_META = 16
D_POOL = 512
POOL_GROUP = 128
POOL_WINDOWS = (2, 4, 8, 16)
POOL_BUF = 15
D_CONV = 512
CONV_BUF = 2
N_EXPERTS = 32
TOP_K = 4
D_FF = 1024
SWIGLU_LIMIT = 7.0
SWIGLU_ALPHA = 1.702
RMS_EPS = 1e-5

O_CG = D_POOL
O_BG = O_CG + D_CONV
O_V = O_BG + D_CONV
O_GL = O_V + D_CONV

T_PROMPT = BATCH * SEQ
T_ALL = T_PROMPT + DEC_BATCH
N_ASSIGN = TOP_K * T_ALL

MIX_TILE = 512
MIX_PROMPT_STEPS = T_PROMPT // MIX_TILE
SEQ_TILES = SEQ // MIX_TILE
POOL_HALO = 16
CONV_HALO = 8
DISP_TILE = 256
DISP_STEPS = -(-T_ALL // DISP_TILE)
DISP_LAST_ROWS = T_ALL - (DISP_STEPS - 1) * DISP_TILE
MOE_BLK = 256
N_BLOCKS = -(-N_ASSIGN // MOE_BLK) + N_EXPERTS
N_PAD_ROWS = N_BLOCKS * MOE_BLK - N_ASSIGN
PAD_CHUNK = 64
assert N_PAD_ROWS % PAD_CHUNK == 0
ROW_DMA_PRIORITY = 0
WEIGHT_DMA_PRIORITY = 1
COMB_TILE = 512
COMB_PROMPT_STEPS = T_PROMPT // COMB_TILE
T_PAD = (T_ALL + COMB_TILE - 1) // COMB_TILE * COMB_TILE
SLAB_GAP = T_PAD - T_ALL
OUT_ROWS = (N_BLOCKS + 2) * MOE_BLK
TAIL0 = TOP_K * T_PAD
assert OUT_ROWS >= TAIL0

ROW_TILE = 8
LANES = D_MODEL // ROW_TILE

VMEM_LIMIT = 56 * 1024 * 1024

_F32 = jnp.float32
_BF16 = jnp.bfloat16


def _rms(x, g):
    ms = jnp.mean(x * x, axis=-1, keepdims=True)
    return x * lax.rsqrt(ms + RMS_EPS) * g


def _bdot(a, w):
    return jnp.dot(a.astype(_BF16), w, preferred_element_type=_F32)


def _store_row_tiles(ref, val):
    rows = val.shape[0]
    for c in range(ROW_TILE):
        ref[pl.ds(c, rows, stride=ROW_TILE), :] = val[:, c * LANES:(c + 1) * LANES]


def _load_row_tiles(ref, rows):
    return jnp.concatenate([ref[pl.ds(c, rows, stride=ROW_TILE), :] for c in range(ROW_TILE)], axis=-1)


def _branches(z, pm, conv, wgrp_ref, pscale_ref, wup_ref, wcout_ref):
    pmb = pm.astype(_BF16)
    half = 2 * POOL_GROUP
    pg = jnp.concatenate(
        [jnp.dot(pmb[:, :half], wgrp_ref[0], preferred_element_type=_F32),
         jnp.dot(pmb[:, half:], wgrp_ref[1], preferred_element_type=_F32)], axis=-1)
    branch_a = _bdot(pg * pscale_ref[...], wup_ref[...])
    branch_b = _bdot(z[:, O_BG:O_V] * conv, wcout_ref[...])
    return branch_a, branch_b


def _merge_and_route(x, z, branch_a, branch_b, wo_ref, gffn_ref, wrt_ref, br_ref, base_ref, n_live):
    gates = jax.nn.sigmoid(z[:, O_GL:])
    merged = gates[:, :D_MODEL] * branch_a + gates[:, D_MODEL:] * branch_b
    x1 = x + _bdot(merged, wo_ref[...])
    h2 = _rms(x1, gffn_ref[...])
    logits = lax.dot_general(wrt_ref[...], h2, (((1,), (1,)), ((), ())),
                             precision=lax.Precision.HIGHEST,
                             preferred_element_type=_F32) + br_ref[...]
    iota = lax.broadcasted_iota(jnp.int32, logits.shape, 0)
    vals, idxs = [], []
    cur = logits
    for _ in range(TOP_K):
        m = jnp.max(cur, axis=0, keepdims=True)
        ik = jnp.min(jnp.where(cur == m, iota, N_EXPERTS), axis=0, keepdims=True)
        vals.append(m)
        idxs.append(ik)
        cur = jnp.where(iota == ik, -jnp.inf, cur)
    exps = [jnp.exp(v - vals[0]) for v in vals]
    denom = (exps[0] + exps[1]) + (exps[2] + exps[3])
    gate = [e / denom for e in exps]
    rows = logits.shape[1]
    live = lax.broadcasted_iota(jnp.int32, logits.shape, 1) < n_live
    onehots = [jnp.where(live, jnp.where(iota == ik, 1.0, 0.0), 0.0) for ik in idxs]
    member = (onehots[0] + onehots[1]) + (onehots[2] + onehots[3])
    earlier = (lax.broadcasted_iota(jnp.int32, (rows, rows), 0)
               < lax.broadcasted_iota(jnp.int32, (rows, rows), 1))
    before = jnp.dot(member.astype(_BF16), jnp.where(earlier, 1.0, 0.0).astype(_BF16),
                     preferred_element_type=_F32)
    pos = base_ref[...] + before
    ranks = [jnp.sum(oh * pos, axis=0, keepdims=True).astype(jnp.int32) for oh in onehots]
    base_ref[...] = base_ref[...] + jnp.sum(member, axis=1, keepdims=True)
    return x1, h2, idxs, gate, ranks


def _mixer_kernel(xp_ref, xs_ref, spool_ref, sconv_ref, meta_ref, gmix_ref, win_ref, wgrp_ref, pscale_ref,
                  wup_ref, wconv_ref, wcout_ref, wo_ref, gffn_ref, wrt_ref, br_ref,
                  x1_ref, h2_ref, idx_ref, gate_ref, rank_ref, cnt_ref, pool_out_ref, conv_out_ref, us_ref, cvs_ref,
                  pool_ext, conv_ext, meta_pool, meta_conv, pm_scr, conv_scr, base_scr):
    s = pl.program_id(0)
    is_prompt = s < MIX_PROMPT_STEPS
    j = lax.rem(s, SEQ_TILES)

    @pl.when(s == 0)
    def _():
        hm = _rms(meta_ref[...], gmix_ref[...]).astype(_BF16)
        zm = jnp.dot(hm, win_ref[:, :O_GL], preferred_element_type=_F32)
        meta_pool[...] = zm[:, :D_POOL]
        meta_conv[...] = zm[:, O_CG:O_BG] * zm[:, O_V:O_GL]
        base_scr[...] = jnp.zeros_like(base_scr)

    @pl.when(is_prompt & (j == 0))
    def _():
        pool_ext[0:POOL_HALO, :] = meta_pool[...]
        conv_ext[0:CONV_HALO, :] = meta_conv[N_META - CONV_HALO:, :]

    x = jnp.where(is_prompt, xp_ref[...], xs_ref[...])
    h = _rms(x, gmix_ref[...]).astype(_BF16)
    z = jnp.dot(h, win_ref[...], preferred_element_type=_F32)
    u = z[:, :D_POOL]
    cv = z[:, O_CG:O_BG] * z[:, O_V:O_GL]
    wc = wconv_ref[...]

    @pl.when(is_prompt)
    def _():
        pool_ext[POOL_HALO:, :] = u
        ext = pool_ext[...]
        for g, w in enumerate(POOL_WINDOWS):
            lanes = slice(g * POOL_GROUP, (g + 1) * POOL_GROUP)
            acc = ext[:, lanes]
            sh = 1
            while sh < w:
                acc = acc + pltpu.roll(acc, sh, 0)
                sh *= 2
            pm_scr[:, lanes] = acc[POOL_HALO:, :] * (1.0 / w) - u[:, lanes]
        conv_ext[CONV_HALO:, :] = cv
        cext = conv_ext[...]
        conv_scr[...] = (wc[0:1, :] * pltpu.roll(cext, 2, 0)[CONV_HALO:, :]
                         + wc[1:2, :] * pltpu.roll(cext, 1, 0)[CONV_HALO:, :]) + wc[2:3, :] * cv
        pool_ext[0:POOL_HALO, :] = u[MIX_TILE - POOL_HALO:, :]
        conv_ext[0:CONV_HALO, :] = cv[MIX_TILE - CONV_HALO:, :]

    @pl.when(is_prompt & (j == SEQ_TILES - 1))
    def _():
        pool_out_ref[...] = u[MIX_TILE - POOL_HALO:, :]
        conv_out_ref[...] = cv[MIX_TILE - CONV_HALO:, :]

    @pl.when(jnp.logical_not(is_prompt))
    def _():
        us = u[:DEC_BATCH, :]
        cvs = cv[:DEC_BATCH, :]
        pm_scr[...] = jnp.zeros_like(pm_scr)
        conv_scr[...] = jnp.zeros_like(conv_scr)
        for g, w in enumerate(POOL_WINDOWS):
            lanes = slice(g * POOL_GROUP, (g + 1) * POOL_GROUP)
            acc = us[:, lanes]
            for t in range(POOL_BUF - (w - 1), POOL_BUF):
                acc = acc + spool_ref[t, :, lanes]
            pm_scr[0:DEC_BATCH, lanes] = acc * (1.0 / w) - us[:, lanes]
        conv_scr[0:DEC_BATCH, :] = (wc[0:1, :] * sconv_ref[0] + wc[1:2, :] * sconv_ref[1]) + wc[2:3, :] * cvs
        us_ref[...] = us
        cvs_ref[...] = cvs

    branch_a, branch_b = _branches(z, pm_scr[...], conv_scr[...], wgrp_ref, pscale_ref, wup_ref, wcout_ref)
    n_live = jnp.where(is_prompt, MIX_TILE, DEC_BATCH)
    x1, h2, idxs, gate, ranks = _merge_and_route(x, z, branch_a, branch_b, wo_ref, gffn_ref, wrt_ref, br_ref,
                                                 base_scr, n_live)
    x1_ref[...] = x1
    _store_row_tiles(h2_ref, h2)
    for k in range(TOP_K):
        idx_ref[k:k + 1, :] = idxs[k]
        gate_ref[k:k + 1, :] = gate[k]
        rank_ref[k:k + 1, :] = ranks[k]

    @pl.when(jnp.logical_not(is_prompt))
    def _():
        cnt_ref[...] = jnp.broadcast_to(base_scr[...], cnt_ref.shape)


def _const_spec(shape):
    nd = len(shape)
    return pl.BlockSpec(shape, lambda *_: (0,) * nd)


def _resident_spec(shape):
    nd = len(shape)
    return pl.BlockSpec(shape, lambda *_: (0,) * nd, pipeline_mode=pl.Buffered(1))


def _mixer(x_prompt, xs_pad, spool_t, sconv_t, meta, w):
    last = MIX_PROMPT_STEPS - 1

    def x_map(s):
        sp = jnp.minimum(s, last)
        return (sp // SEQ_TILES, sp % SEQ_TILES, 0)

    def seq_map(s):
        return (jnp.minimum(s, last) // SEQ_TILES, 0, 0)

    consts = (xs_pad, spool_t, sconv_t, meta) + tuple(w)
    return pl.pallas_call(
        _mixer_kernel,
        grid=(MIX_PROMPT_STEPS + 1,),
        in_specs=[pl.BlockSpec((None, MIX_TILE, D_MODEL), x_map)] + [_resident_spec(a.shape) for a in consts],
        out_specs=[
            pl.BlockSpec((MIX_TILE, D_MODEL), lambda s: (s, 0)),
            pl.BlockSpec((MIX_TILE * ROW_TILE, LANES), lambda s: (s, 0)),
            pl.BlockSpec((TOP_K, MIX_TILE), lambda s: (0, s)),
            pl.BlockSpec((TOP_K, MIX_TILE), lambda s: (0, s)),
            pl.BlockSpec((TOP_K, MIX_TILE), lambda s: (0, s)),
            _const_spec((N_EXPERTS, 128)),
            pl.BlockSpec((None, POOL_HALO, D_POOL), seq_map),
            pl.BlockSpec((None, CONV_HALO, D_CONV), seq_map),
            _const_spec((DEC_BATCH, D_POOL)),
            _const_spec((DEC_BATCH, D_CONV)),
        ],
        out_shape=[
            jax.ShapeDtypeStruct((T_ALL, D_MODEL), _F32),
            jax.ShapeDtypeStruct((T_ALL * ROW_TILE, LANES), _F32),
            jax.ShapeDtypeStruct((TOP_K, T_ALL), jnp.int32),
            jax.ShapeDtypeStruct((TOP_K, T_ALL), _F32),
            jax.ShapeDtypeStruct((TOP_K, T_ALL), jnp.int32),
            jax.ShapeDtypeStruct((N_EXPERTS, 128), _F32),
            jax.ShapeDtypeStruct((BATCH, POOL_HALO, D_POOL), _F32),
            jax.ShapeDtypeStruct((BATCH, CONV_HALO, D_CONV), _F32),
            jax.ShapeDtypeStruct((DEC_BATCH, D_POOL), _F32),
            jax.ShapeDtypeStruct((DEC_BATCH, D_CONV), _F32),
        ],
        scratch_shapes=[
            pltpu.VMEM((POOL_HALO + MIX_TILE, D_POOL), _F32),
            pltpu.VMEM((CONV_HALO + MIX_TILE, D_CONV), _F32),
            pltpu.VMEM((N_META, D_POOL), _F32),
            pltpu.VMEM((N_META, D_CONV), _F32),
            pltpu.VMEM((MIX_TILE, D_POOL), _F32),
            pltpu.VMEM((MIX_TILE, D_CONV), _F32),
            pltpu.VMEM((N_EXPERTS, 1), _F32),
        ],
        compiler_params=pltpu.CompilerParams(
            dimension_semantics=("arbitrary",), vmem_limit_bytes=VMEM_LIMIT),
        name="mixer",
    )(x_prompt, *consts)


def _routing_tables(idx, rank, cnt):
    def lookup(table, keys):
        ids = jnp.arange(table.shape[0], dtype=jnp.int32)
        return jnp.sum(jnp.where(keys[..., None] == ids, table, 0), axis=-1)

    counts = cnt[:, 0].astype(jnp.int32)
    padded = (counts + MOE_BLK - 1) // MOE_BLK * MOE_BLK
    pend = jnp.cumsum(padded)
    pstart = pend - padded
    start = jnp.cumsum(counts) - counts

    dest = lookup(pstart, idx) + rank
    dest = jnp.pad(dest, ((0, 0), (0, DISP_STEPS * DISP_TILE - T_ALL)))
    disp = dest.reshape(TOP_K, DISP_STEPS, DISP_TILE).transpose(1, 0, 2).reshape(DISP_STEPS, TOP_K * DISP_TILE)

    n_pad = padded - counts
    pad_cum = jnp.cumsum(n_pad)
    q = jnp.arange(N_PAD_ROWS, dtype=jnp.int32)
    e_q = jnp.sum((q[:, None] >= pad_cum[None, :]).astype(jnp.int32), axis=1)
    first_pad = lookup(jnp.concatenate([pstart + counts, pend[-1:]]), e_q)
    pads_before = lookup(jnp.concatenate([pad_cum - n_pad, pad_cum[-1:]]), e_q)
    pad_rows = first_pad + q - pads_before

    e_flat = idx.T.reshape(N_ASSIGN)
    order = jnp.argsort(e_flat, stable=True).astype(jnp.int32)
    block_start = jnp.arange(N_BLOCKS, dtype=jnp.int32) * MOE_BLK
    block_exp = jnp.minimum(jnp.sum((block_start[:, None] >= pend[None, :]).astype(jnp.int32), axis=1),
                            N_EXPERTS - 1)
    in_block = jnp.arange(MOE_BLK, dtype=jnp.int32)[None, :]
    jr = (block_start - lookup(pstart, block_exp))[:, None] + in_block
    valid = (jr < lookup(counts, block_exp)[:, None]).reshape(N_BLOCKS * MOE_BLK)
    sorted_pos = (lookup(start, block_exp)[:, None] + jr).reshape(N_BLOCKS * MOE_BLK)
    a_r = order[jnp.clip(sorted_pos, 0, N_ASSIGN - 1)]
    t_r = a_r // TOP_K
    k_r = a_r - t_r * TOP_K

    def spare_row(q):
        in_gap = q < TOP_K * SLAB_GAP
        gap_row = (q // SLAB_GAP) * T_PAD + T_ALL + q % SLAB_GAP
        return jnp.where(in_gap, gap_row, TAIL0 + q - TOP_K * SLAB_GAP)

    pad_rank = jnp.cumsum((~valid).astype(jnp.int32)) - 1
    dst = jnp.where(valid, k_r * T_PAD + t_r, spare_row(2 * MOE_BLK + pad_rank)).astype(jnp.int32)
    warmup = spare_row(jnp.arange(2 * MOE_BLK, dtype=jnp.int32)).astype(jnp.int32).reshape(2, MOE_BLK)
    dst_rows = jnp.concatenate([warmup, dst.reshape(N_BLOCKS, MOE_BLK)], axis=0)
    blocks = jnp.arange(N_BLOCKS, dtype=jnp.int32)
    first = jnp.concatenate([jnp.ones((1,), jnp.bool_), block_exp[1:] != block_exp[:-1]])
    run_slot = (jnp.cumsum(first.astype(jnp.int32)) - 1) % 2
    later_first = first[None, :] & (blocks[None, :] > blocks[:, None])
    next_first = jnp.min(jnp.where(later_first, blocks[None, :], N_BLOCKS), axis=1)
    has_next = next_first < N_BLOCKS
    next_exp = lookup(block_exp, jnp.minimum(next_first, N_BLOCKS - 1))
    n_used = jnp.broadcast_to(pend[-1] // MOE_BLK, (N_BLOCKS,))
    plan = jnp.stack([block_exp, first.astype(jnp.int32), run_slot, next_exp, has_next.astype(jnp.int32),
                      n_used, jnp.zeros_like(n_used)]).astype(jnp.int32)
    return plan, disp * ROW_TILE, pad_rows * ROW_TILE, dst_rows * ROW_TILE


def _dispatch_kernel(disp_hbm, pads_hbm, h2_ref, xs_hbm, tbl0, tbl1, pad_tbl, zero_tile, rsem, tsem, psem):
    s = pl.program_id(0)
    ns = pl.num_programs(0)
    tbls = (tbl0, tbl1)

    def table_copy(row, p):
        return pltpu.make_async_copy(disp_hbm.at[row], tbls[p], tsem.at[p])

    def row_copy(off, r, k):
        return pltpu.make_async_copy(h2_ref.at[pl.ds(r * ROW_TILE, ROW_TILE), :],
                                     xs_hbm.at[pl.ds(off, ROW_TILE), :], rsem.at[k])

    def pad_copy(off):
        return pltpu.make_async_copy(zero_tile, xs_hbm.at[pl.ds(off, ROW_TILE), :], psem.at[0])

    def scatter_rows(p, n_rows):
        for r in range(n_rows):
            for k in range(TOP_K):
                off = pl.multiple_of(tbls[p][k * DISP_TILE + r], ROW_TILE)
                row_copy(off, r, k).start(priority=(r + k) % 2)
        for r in range(n_rows):
            for k in range(TOP_K):
                row_copy(0, r, k).wait()

    def fill_pads():
        zero_tile[...] = jnp.zeros_like(zero_tile)
        load = pltpu.make_async_copy(pads_hbm, pad_tbl, psem.at[1])
        load.start()
        load.wait()

        def chunk(i, carry):
            for j in range(PAD_CHUNK):
                pad_copy(pl.multiple_of(pad_tbl[i * PAD_CHUNK + j], ROW_TILE)).start(priority=j % 2)

            @pl.when(i > 0)
            def _():
                for j in range(PAD_CHUNK):
                    pad_copy(0).wait()
            return carry

        lax.fori_loop(0, N_PAD_ROWS // PAD_CHUNK, chunk, 0)
        for j in range(PAD_CHUNK):
            pad_copy(0).wait()

    @pl.when(s == 0)
    def _():
        table_copy(0, 0).start()

    def step(p):
        table_copy(s, p).wait()

        @pl.when(s < ns - 1)
        def _():
            table_copy(s + 1, 1 - p).start()
            scatter_rows(p, DISP_TILE)

        @pl.when(s == ns - 1)
        def _():
            scatter_rows(p, DISP_LAST_ROWS)
            fill_pads()

    parity = lax.rem(s, 2)
    pl.when(parity == 0)(lambda: step(0))
    pl.when(parity == 1)(lambda: step(1))


def _dispatch(disp, pad_rows, h2):
    return pl.pallas_call(
        _dispatch_kernel,
        grid=(DISP_STEPS,),
        in_specs=[pl.BlockSpec(memory_space=pl.ANY),
                  pl.BlockSpec(memory_space=pl.ANY),
                  pl.BlockSpec((DISP_TILE * ROW_TILE, LANES), lambda s: (s, 0))],
        out_specs=pl.BlockSpec(memory_space=pl.ANY),
        out_shape=jax.ShapeDtypeStruct((N_BLOCKS * MOE_BLK * ROW_TILE, LANES), _F32),
        scratch_shapes=[
            pltpu.SMEM((TOP_K * DISP_TILE,), jnp.int32),
            pltpu.SMEM((TOP_K * DISP_TILE,), jnp.int32),
            pltpu.SMEM((N_PAD_ROWS,), jnp.int32),
            pltpu.VMEM((ROW_TILE, LANES), _F32),
            pltpu.SemaphoreType.DMA((TOP_K,)),
            pltpu.SemaphoreType.DMA((2,)),
            pltpu.SemaphoreType.DMA((2,)),
        ],
        compiler_params=pltpu.CompilerParams(
            dimension_semantics=("arbitrary",), vmem_limit_bytes=VMEM_LIMIT),
        name="dispatch",
    )(disp, pad_rows, h2)


PLAN_EXPERT, PLAN_FIRST, PLAN_SLOT, PLAN_NEXT, PLAN_HAS_NEXT, PLAN_USED, PLAN_ZERO = range(7)
SCATTER_GROUPS = 4


def _moe_kernel(plan_ref, table_hbm, x_ref, wgu_hbm, bgu_ref, wdn_hbm, bdn_ref, out_hbm,
                ybuf0, ybuf1, tbl0, tbl1, gu_stage0, gu_stage1, dn_stage0, dn_stage1, wgu_bf, wdn_bf,
                ssem, tsem, wsem):
    b = pl.program_id(0)
    nb = pl.num_programs(0)
    ybufs, tbls = (ybuf0, ybuf1), (tbl0, tbl1)
    gu_stages, dn_stages = (gu_stage0, gu_stage1), (dn_stage0, dn_stage1)
    n_used = plan_ref[PLAN_USED, 0]

    def table_copy(row, p):
        return pltpu.make_async_copy(table_hbm.at[row], tbls[p], tsem.at[p])

    def weight_copies(e, w):
        return (pltpu.make_async_copy(wgu_hbm.at[e], gu_stages[w], wsem.at[2 * w]),
                pltpu.make_async_copy(wdn_hbm.at[e], dn_stages[w], wsem.at[2 * w + 1]))

    def scatter_row(off, r, p):
        return pltpu.make_async_copy(ybufs[p].at[pl.ds(r * ROW_TILE, ROW_TILE), :],
                                     out_hbm.at[pl.ds(off, ROW_TILE), :], ssem.at[p])

    def start_scatter(tp, p, lo=0, hi=MOE_BLK, after=0):
        for r in range(lo, hi):
            off = pl.multiple_of(tbls[tp][r] + after, ROW_TILE)
            scatter_row(off, r, p).start(priority=ROW_DMA_PRIORITY)

    def zero_after(value):
        return lax.bitcast_convert_type(value, jnp.int32) * plan_ref[PLAN_ZERO, b]

    def wait_scatter(p):
        for r in range(MOE_BLK):
            scatter_row(0, r, p).wait()

    @pl.when(b == 0)
    def _():
        first = table_copy(0, 0)
        first.start()
        first.wait()
        ybuf0[...] = jnp.zeros_like(ybuf0)
        ybuf1[...] = jnp.zeros_like(ybuf1)
        start_scatter(0, 0)
        table_copy(1, 1).start()
        for c in weight_copies(plan_ref[PLAN_EXPERT, 0], 0):
            c.start(priority=WEIGHT_DMA_PRIORITY)

    def new_expert(w):
        for c in weight_copies(0, w):
            c.wait()

        @pl.when(plan_ref[PLAN_HAS_NEXT, b] == 1)
        def _():
            for c in weight_copies(plan_ref[PLAN_NEXT, b], 1 - w):
                c.start(priority=WEIGHT_DMA_PRIORITY)

        wgu_bf[...] = gu_stages[w][...].astype(_BF16)
        wdn_bf[...] = dn_stages[w][...].astype(_BF16)

    is_first = plan_ref[PLAN_FIRST, b] == 1
    wslot = plan_ref[PLAN_SLOT, b]
    pl.when(is_first & (wslot == 0))(lambda: new_expert(0))
    pl.when(is_first & (wslot == 1))(lambda: new_expert(1))

    def step(p):
        q = 1 - p
        table_copy(b + 1, q).wait()

        @pl.when(b < n_used)
        def _():
            group = MOE_BLK // SCATTER_GROUPS
            start_scatter(q, q, 0, group)
            table_copy(b + 2, p).start()
            x = _load_row_tiles(x_ref, MOE_BLK).astype(_BF16)
            hgu = jnp.dot(x, wgu_bf[...], preferred_element_type=_F32) + bgu_ref[...]
            for g in range(1, SCATTER_GROUPS):
                col = (2 * D_FF * g) // SCATTER_GROUPS - 1
                start_scatter(q, q, g * group, (g + 1) * group, zero_after(hgu[0, col]))
            gate = jnp.minimum(hgu[:, :D_FF], SWIGLU_LIMIT)
            up = jnp.clip(hgu[:, D_FF:], -SWIGLU_LIMIT, SWIGLU_LIMIT)
            act = gate * jax.nn.sigmoid(SWIGLU_ALPHA * gate) * (up + 1.0)
            y = jnp.dot(act.astype(_BF16), wdn_bf[...], preferred_element_type=_F32) + bdn_ref[...]
            wait_scatter(p)
            _store_row_tiles(ybufs[p], y)

        @pl.when(b >= n_used)
        def _():
            start_scatter(q, q)
            table_copy(b + 2, p).start()
            wait_scatter(p)
            ybufs[p][...] = jnp.zeros_like(ybufs[p])

        @pl.when(b == nb - 1)
        def _():
            table_copy(b + 2, p).wait()
            start_scatter(p, p)
            wait_scatter(p)
            wait_scatter(q)

    parity = lax.rem(b, 2)
    pl.when(parity == 0)(lambda: step(0))
    pl.when(parity == 1)(lambda: step(1))


def _moe_experts(plan, table, xs, wgu, bgu, wdn, bdn):
    grid_spec = pltpu.PrefetchScalarGridSpec(
        num_scalar_prefetch=1,
        grid=(N_BLOCKS,),
        in_specs=[
            pl.BlockSpec(memory_space=pl.ANY),
            pl.BlockSpec((MOE_BLK * ROW_TILE, LANES), lambda b, plan: (b, 0)),
            pl.BlockSpec(memory_space=pl.ANY),
            pl.BlockSpec((None, 1, 2 * D_FF), lambda b, plan: (plan[PLAN_EXPERT, b], 0, 0)),
            pl.BlockSpec(memory_space=pl.ANY),
            pl.BlockSpec((None, 1, D_MODEL), lambda b, plan: (plan[PLAN_EXPERT, b], 0, 0)),
        ],
        out_specs=pl.BlockSpec(memory_space=pl.ANY),
        scratch_shapes=[
            pltpu.VMEM((MOE_BLK * ROW_TILE, LANES), _F32),
            pltpu.VMEM((MOE_BLK * ROW_TILE, LANES), _F32),
            pltpu.SMEM((MOE_BLK,), jnp.int32),
            pltpu.SMEM((MOE_BLK,), jnp.int32),
            pltpu.VMEM((D_MODEL, 2 * D_FF), _F32),
            pltpu.VMEM((D_MODEL, 2 * D_FF), _F32),
            pltpu.VMEM((D_FF, D_MODEL), _F32),
            pltpu.VMEM((D_FF, D_MODEL), _F32),
            pltpu.VMEM((D_MODEL, 2 * D_FF), _BF16),
            pltpu.VMEM((D_FF, D_MODEL), _BF16),
            pltpu.SemaphoreType.DMA((2,)),
            pltpu.SemaphoreType.DMA((2,)),
            pltpu.SemaphoreType.DMA((4,)),
        ],
    )
    return pl.pallas_call(
        _moe_kernel,
        grid_spec=grid_spec,
        out_shape=jax.ShapeDtypeStruct((OUT_ROWS * ROW_TILE, LANES), _F32),
        compiler_params=pltpu.CompilerParams(
            dimension_semantics=("arbitrary",), vmem_limit_bytes=VMEM_LIMIT),
        name="moe_experts",
    )(plan, table, xs, wgu, bgu, wdn, bdn)


def _combine_kernel(x1_ref, g_ref, o0_ref, o1_ref, o2_ref, o3_ref, gfin_ref, yp_ref, ys_ref):
    i = pl.program_id(0)
    g = g_ref[...]
    o = [_load_row_tiles(r, COMB_TILE) for r in (o0_ref, o1_ref, o2_ref, o3_ref)]
    moe = (g[:, 0:1] * o[0] + g[:, 1:2] * o[1]) + (g[:, 2:3] * o[2] + g[:, 3:4] * o[3])
    y = _rms(x1_ref[...] + moe, gfin_ref[...])

    @pl.when(i < COMB_PROMPT_STEPS)
    def _():
        yp_ref[...] = y

    @pl.when(i == COMB_PROMPT_STEPS)
    def _():
        ys_ref[...] = y[:DEC_BATCH, :]


def _combine(x1, gates_tk, out4, g_final):
    slab_blocks = T_PAD // COMB_TILE

    def slab_spec(k):
        return pl.BlockSpec((COMB_TILE * ROW_TILE, LANES), lambda i, k=k: (k * slab_blocks + i, 0))

    return pl.pallas_call(
        _combine_kernel,
        grid=(COMB_PROMPT_STEPS + 1,),
        in_specs=[pl.BlockSpec((COMB_TILE, D_MODEL), lambda i: (i, 0)),
                  pl.BlockSpec((COMB_TILE, TOP_K), lambda i: (i, 0)),
                  slab_spec(0), slab_spec(1), slab_spec(2), slab_spec(3),
                  _const_spec(g_final.shape)],
        out_specs=[pl.BlockSpec((COMB_TILE, D_MODEL), lambda i: (jnp.minimum(i, COMB_PROMPT_STEPS - 1), 0)),
                   _const_spec((DEC_BATCH, D_MODEL))],
        out_shape=[jax.ShapeDtypeStruct((T_PROMPT, D_MODEL), _F32),
                   jax.ShapeDtypeStruct((DEC_BATCH, D_MODEL), _F32)],
        compiler_params=pltpu.CompilerParams(
            dimension_semantics=("arbitrary",), vmem_limit_bytes=VMEM_LIMIT),
        name="combine",
    )(x1, gates_tk, out4, out4, out4, out4, g_final)


def kernel(x_prompt, x_sample, state_pool, state_conv, meta_tokens, g_mix, w_in, w_pool_grp, pool_scale,
           w_pool_up, w_conv, w_conv_out, w_o, g_ffn, w_router, b_router, w_gu, b_gu, w_down, b_down, g_final):
    assert g_mix.shape[0] == 1, "single-layer step"
    grp = w_pool_grp[0].astype(_BF16)
    zero = jnp.zeros((POOL_GROUP, POOL_GROUP), _BF16)
    wgrp = jnp.stack([jnp.block([[grp[0], zero], [zero, grp[1]]]),
                      jnp.block([[grp[2], zero], [zero, grp[3]]])])
    w = (
        g_mix[0].reshape(1, D_MODEL),
        w_in[0].astype(_BF16),
        wgrp,
        pool_scale[0].reshape(1, D_POOL),
        w_pool_up[0].astype(_BF16),
        w_conv[0],
        w_conv_out[0].astype(_BF16),
        w_o[0].astype(_BF16),
        g_ffn[0].reshape(1, D_MODEL),
        w_router[0].T,
        b_router[0].reshape(N_EXPERTS, 1),
    )
    xs_pad = jnp.pad(x_sample.reshape(DEC_BATCH, D_MODEL), ((0, MIX_TILE - DEC_BATCH), (0, 0)))
    spool_t = jnp.transpose(state_pool[0], (1, 0, 2))
    sconv_t = jnp.transpose(state_conv[0], (1, 0, 2))
    x1, h2, idx, gate, rank, cnt, pool_p, conv_p, u_s, cv_s = _mixer(
        x_prompt, xs_pad, spool_t, sconv_t, meta_tokens, w)

    plan, disp, pad_rows, dst_rows = _routing_tables(idx, rank, cnt)
    xs = _dispatch(disp, pad_rows, h2)
    out4 = _moe_experts(plan, dst_rows, xs,
                        w_gu.reshape(N_EXPERTS, D_MODEL, 2 * D_FF), b_gu.reshape(N_EXPERTS, 1, 2 * D_FF),
                        w_down.reshape(N_EXPERTS, D_FF, D_MODEL), b_down.reshape(N_EXPERTS, 1, D_MODEL))

    y_p, y_s = _combine(x1, gate.T, out4, g_final.reshape(1, D_MODEL))

    new_pool_p = pool_p[:, POOL_HALO - POOL_BUF:, :][None]
    new_conv_p = conv_p[:, CONV_HALO - CONV_BUF:, :][None]
    new_pool_s = jnp.concatenate([state_pool[0][:, 1:, :], u_s[:, None, :]], axis=1)[None]
    new_conv_s = jnp.concatenate([state_conv[0][:, 1:, :], cv_s[:, None, :]], axis=1)[None]
    return (y_p.reshape(BATCH, SEQ, D_MODEL), y_s.reshape(DEC_BATCH, 1, D_MODEL),
            new_pool_p, new_conv_p, new_pool_s, new_conv_s)
```

```python
import jax
import jax.numpy as jnp
from jax import lax
from jax.experimental import pallas as pl
from jax.experimental.pallas import tpu as pltpu

D_MODEL = 1024
BATCH = 8
SEQ = 2048
DEC_BATCH = 128
N_META = 16
D_POOL = 512
POOL_GROUP = 128
POOL_WINDOWS = (2, 4, 8, 16)
POOL_BUF = 15
D_CONV = 512
CONV_BUF = 2
N_EXPERTS = 32
TOP_K = 4
D_FF = 1024
SWIGLU_LIMIT = 7.0
SWIGLU_ALPHA = 1.702
RMS_EPS = 1e-5

O_CG = D_POOL
O_BG = O_CG + D_CONV
O_V = O_BG + D_CONV
O_GL = O_V + D_CONV

T_PROMPT = BATCH * SEQ
T_ALL = T_PROMPT + DEC_BATCH
N_ASSIGN = TOP_K * T_ALL

MIX_TILE = 512
MIX_PROMPT_STEPS = T_PROMPT // MIX_TILE
SEQ_TILES = SEQ // MIX_TILE
POOL_HALO = 16
CONV_HALO = 8
DISP_TILE = 256
DISP_STEPS = -(-T_ALL // DISP_TILE)
DISP_LAST_ROWS = T_ALL - (DISP_STEPS - 1) * DISP_TILE
MOE_BLK = 256
N_BLOCKS = -(-N_ASSIGN // MOE_BLK) + N_EXPERTS
N_PAD_ROWS = N_BLOCKS * MOE_BLK - N_ASSIGN
PAD_CHUNK = 64
assert N_PAD_ROWS % PAD_CHUNK == 0
ROW_DMA_PRIORITY = 0
WEIGHT_DMA_PRIORITY = 1
COMB_TILE = 512
COMB_PROMPT_STEPS = T_PROMPT // COMB_TILE
T_PAD = (T_ALL + COMB_TILE - 1) // COMB_TILE * COMB_TILE
SLAB_GAP = T_PAD - T_ALL
OUT_ROWS = (N_BLOCKS + 2) * MOE_BLK
TAIL0 = TOP_K * T_PAD
assert OUT_ROWS >= TAIL0

ROW_TILE = 8
ROW_TILE_SHIFT = 3
LANES = D_MODEL // ROW_TILE
INV_LANES = 128
INV_LANE_SHIFT = 7
assert 1 << ROW_TILE_SHIFT == ROW_TILE and 1 << INV_LANE_SHIFT == INV_LANES
assert (N_BLOCKS * MOE_BLK) % INV_LANES == 0

VMEM_LIMIT = 56 * 1024 * 1024

_F32 = jnp.float32
_BF16 = jnp.bfloat16


def _rms(x, g):
    ms = jnp.mean(x * x, axis=-1, keepdims=True)
    return x * lax.rsqrt(ms + RMS_EPS) * g


def _bdot(a, w):
    return jnp.dot(a.astype(_BF16), w, preferred_element_type=_F32)


def _store_row_tiles(ref, val):
    rows = val.shape[0]
    for c in range(ROW_TILE):
        ref[pl.ds(c, rows, stride=ROW_TILE), :] = val[:, c * LANES:(c + 1) * LANES]


def _load_row_tiles(ref, rows):
    return jnp.concatenate([ref[pl.ds(c, rows, stride=ROW_TILE), :] for c in range(ROW_TILE)], axis=-1)


def _branches(z, pm, conv, wgrp_ref, pscale_ref, wup_ref, wcout_ref):
    pmb = pm.astype(_BF16)
    half = 2 * POOL_GROUP
    pg = jnp.concatenate(
        [jnp.dot(pmb[:, :half], wgrp_ref[0], preferred_element_type=_F32),
         jnp.dot(pmb[:, half:], wgrp_ref[1], preferred_element_type=_F32)], axis=-1)
    branch_a = _bdot(pg * pscale_ref[...], wup_ref[...])
    branch_b = _bdot(z[:, O_BG:O_V] * conv, wcout_ref[...])
    return branch_a, branch_b


def _merge_and_route(x, z, branch_a, branch_b, wo_ref, gffn_ref, wrt_ref, br_ref, base_ref, n_live):
    gates = jax.nn.sigmoid(z[:, O_GL:])
    merged = gates[:, :D_MODEL] * branch_a + gates[:, D_MODEL:] * branch_b
    x1 = x + _bdot(merged, wo_ref[...])
    h2 = _rms(x1, gffn_ref[...])
    logits = lax.dot_general(wrt_ref[...], h2, (((1,), (1,)), ((), ())),
                             precision=lax.Precision.HIGHEST,
                             preferred_element_type=_F32) + br_ref[...]
    iota = lax.broadcasted_iota(jnp.int32, logits.shape, 0)
    vals, idxs = [], []
    cur = logits
    for _ in range(TOP_K):
        m = jnp.max(cur, axis=0, keepdims=True)
        ik = jnp.min(jnp.where(cur == m, iota, N_EXPERTS), axis=0, keepdims=True)
        vals.append(m)
        idxs.append(ik)
        cur = jnp.where(iota == ik, -jnp.inf, cur)
    exps = [jnp.exp(v - vals[0]) for v in vals]
    denom = (exps[0] + exps[1]) + (exps[2] + exps[3])
    gate = [e / denom for e in exps]
    rows = logits.shape[1]
    live = lax.broadcasted_iota(jnp.int32, logits.shape, 1) < n_live
    onehots = [jnp.where(live, jnp.where(iota == ik, 1.0, 0.0), 0.0) for ik in idxs]
    member = (onehots[0] + onehots[1]) + (onehots[2] + onehots[3])
    earlier = (lax.broadcasted_iota(jnp.int32, (rows, rows), 0)
               < lax.broadcasted_iota(jnp.int32, (rows, rows), 1))
    before = jnp.dot(member.astype(_BF16), jnp.where(earlier, 1.0, 0.0).astype(_BF16),
                     preferred_element_type=_F32)
    pos = base_ref[...] + before
    ranks = [jnp.sum(oh * pos, axis=0, keepdims=True).astype(jnp.int32) for oh in onehots]
    base_ref[...] = base_ref[...] + jnp.sum(member, axis=1, keepdims=True)
    return x1, h2, idxs, gate, ranks


def _mixer_kernel(xp_ref, xs_ref, spool_ref, sconv_ref, meta_ref, gmix_ref, win_ref, wgrp_ref, pscale_ref,
                  wup_ref, wconv_ref, wcout_ref, wo_ref, gffn_ref, wrt_ref, br_ref,
                  x1_ref, h2_ref, idx_ref, gate_ref, rank_ref, cnt_ref, pool_out_ref, conv_out_ref, us_ref, cvs_ref,
                  pool_ext, conv_ext, meta_pool, meta_conv, pm_scr, conv_scr, base_scr):
    s = pl.program_id(0)
    is_prompt = s < MIX_PROMPT_STEPS
    j = lax.rem(s, SEQ_TILES)

    @pl.when(s == 0)
    def _():
        hm = _rms(meta_ref[...], gmix_ref[...]).astype(_BF16)
        zm = jnp.dot(hm, win_ref[:, :O_GL], preferred_element_type=_F32)
        meta_pool[...] = zm[:, :D_POOL]
        meta_conv[...] = zm[:, O_CG:O_BG] * zm[:, O_V:O_GL]
        base_scr[...] = jnp.zeros_like(base_scr)

    @pl.when(is_prompt & (j == 0))
    def _():
        pool_ext[0:POOL_HALO, :] = meta_pool[...]
        conv_ext[0:CONV_HALO, :] = meta_conv[N_META - CONV_HALO:, :]

    x = jnp.where(is_prompt, xp_ref[...], xs_ref[...])
    h = _rms(x, gmix_ref[...]).astype(_BF16)
    z = jnp.dot(h, win_ref[...], preferred_element_type=_F32)
    u = z[:, :D_POOL]
    cv = z[:, O_CG:O_BG] * z[:, O_V:O_GL]
    wc = wconv_ref[...]

    @pl.when(is_prompt)
    def _():
        pool_ext[POOL_HALO:, :] = u
        ext = pool_ext[...]
        for g, w in enumerate(POOL_WINDOWS):
            lanes = slice(g * POOL_GROUP, (g + 1) * POOL_GROUP)
            acc = ext[:, lanes]
            sh = 1
            while sh < w:
                acc = acc + pltpu.roll(acc, sh, 0)
                sh *= 2
            pm_scr[:, lanes] = acc[POOL_HALO:, :] * (1.0 / w) - u[:, lanes]
        conv_ext[CONV_HALO:, :] = cv
        cext = conv_ext[...]
        conv_scr[...] = (wc[0:1, :] * pltpu.roll(cext, 2, 0)[CONV_HALO:, :]
                         + wc[1:2, :] * pltpu.roll(cext, 1, 0)[CONV_HALO:, :]) + wc[2:3, :] * cv
        pool_ext[0:POOL_HALO, :] = u[MIX_TILE - POOL_HALO:, :]
        conv_ext[0:CONV_HALO, :] = cv[MIX_TILE - CONV_HALO:, :]

    @pl.when(is_prompt & (j == SEQ_TILES - 1))
    def _():
        pool_out_ref[...] = u[MIX_TILE - POOL_HALO:, :]
        conv_out_ref[...] = cv[MIX_TILE - CONV_HALO:, :]

    @pl.when(jnp.logical_not(is_prompt))
    def _():
        us = u[:DEC_BATCH, :]
        cvs = cv[:DEC_BATCH, :]
        pm_scr[...] = jnp.zeros_like(pm_scr)
        conv_scr[...] = jnp.zeros_like(conv_scr)
        for g, w in enumerate(POOL_WINDOWS):
            lanes = slice(g * POOL_GROUP, (g + 1) * POOL_GROUP)
            acc = us[:, lanes]
            for t in range(POOL_BUF - (w - 1), POOL_BUF):
                acc = acc + spool_ref[t, :, lanes]
            pm_scr[0:DEC_BATCH, lanes] = acc * (1.0 / w) - us[:, lanes]
        conv_scr[0:DEC_BATCH, :] = (wc[0:1, :] * sconv_ref[0] + wc[1:2, :] * sconv_ref[1]) + wc[2:3, :] * cvs
        us_ref[...] = us
        cvs_ref[...] = cvs

    branch_a, branch_b = _branches(z, pm_scr[...], conv_scr[...], wgrp_ref, pscale_ref, wup_ref, wcout_ref)
    n_live = jnp.where(is_prompt, MIX_TILE, DEC_BATCH)
    x1, h2, idxs, gate, ranks = _merge_and_route(x, z, branch_a, branch_b, wo_ref, gffn_ref, wrt_ref, br_ref,
                                                 base_scr, n_live)
    x1_ref[...] = x1
    _store_row_tiles(h2_ref, h2)
    for k in range(TOP_K):
        idx_ref[k:k + 1, :] = idxs[k]
        gate_ref[k:k + 1, :] = gate[k]
        rank_ref[k:k + 1, :] = ranks[k]

    @pl.when(jnp.logical_not(is_prompt))
    def _():
        cnt_ref[...] = jnp.broadcast_to(base_scr[...], cnt_ref.shape)


def _const_spec(shape):
    nd = len(shape)
    return pl.BlockSpec(shape, lambda *_: (0,) * nd)


def _resident_spec(shape):
    nd = len(shape)
    return pl.BlockSpec(shape, lambda *_: (0,) * nd, pipeline_mode=pl.Buffered(1))


def _mixer(x_prompt, xs_pad, spool_t, sconv_t, meta, w):
    last = MIX_PROMPT_STEPS - 1

    def x_map(s):
        sp = jnp.minimum(s, last)
        return (sp // SEQ_TILES, sp % SEQ_TILES, 0)

    def seq_map(s):
        return (jnp.minimum(s, last) // SEQ_TILES, 0, 0)

    consts = (xs_pad, spool_t, sconv_t, meta) + tuple(w)
    return pl.pallas_call(
        _mixer_kernel,
        grid=(MIX_PROMPT_STEPS + 1,),
        in_specs=[pl.BlockSpec((None, MIX_TILE, D_MODEL), x_map)] + [_resident_spec(a.shape) for a in consts],
        out_specs=[
            pl.BlockSpec((MIX_TILE, D_MODEL), lambda s: (s, 0)),
            pl.BlockSpec((MIX_TILE * ROW_TILE, LANES), lambda s: (s, 0)),
            pl.BlockSpec((TOP_K, MIX_TILE), lambda s: (0, s)),
            pl.BlockSpec((TOP_K, MIX_TILE), lambda s: (0, s)),
            pl.BlockSpec((TOP_K, MIX_TILE), lambda s: (0, s)),
            _const_spec((N_EXPERTS, 128)),
            pl.BlockSpec((None, POOL_HALO, D_POOL), seq_map),
            pl.BlockSpec((None, CONV_HALO, D_CONV), seq_map),
            _const_spec((DEC_BATCH, D_POOL)),
            _const_spec((DEC_BATCH, D_CONV)),
        ],
        out_shape=[
            jax.ShapeDtypeStruct((T_ALL, D_MODEL), _F32),
            jax.ShapeDtypeStruct((T_ALL * ROW_TILE, LANES), _F32),
            jax.ShapeDtypeStruct((TOP_K, T_ALL), jnp.int32),
            jax.ShapeDtypeStruct((TOP_K, T_ALL), _F32),
            jax.ShapeDtypeStruct((TOP_K, T_ALL), jnp.int32),
            jax.ShapeDtypeStruct((N_EXPERTS, 128), _F32),
            jax.ShapeDtypeStruct((BATCH, POOL_HALO, D_POOL), _F32),
            jax.ShapeDtypeStruct((BATCH, CONV_HALO, D_CONV), _F32),
            jax.ShapeDtypeStruct((DEC_BATCH, D_POOL), _F32),
            jax.ShapeDtypeStruct((DEC_BATCH, D_CONV), _F32),
        ],
        scratch_shapes=[
            pltpu.VMEM((POOL_HALO + MIX_TILE, D_POOL), _F32),
            pltpu.VMEM((CONV_HALO + MIX_TILE, D_CONV), _F32),
            pltpu.VMEM((N_META, D_POOL), _F32),
            pltpu.VMEM((N_META, D_CONV), _F32),
            pltpu.VMEM((MIX_TILE, D_POOL), _F32),
            pltpu.VMEM((MIX_TILE, D_CONV), _F32),
            pltpu.VMEM((N_EXPERTS, 1), _F32),
        ],
        compiler_params=pltpu.CompilerParams(
            dimension_semantics=("arbitrary",), vmem_limit_bytes=VMEM_LIMIT),
        name="mixer",
    )(x_prompt, *consts)


def _spare_row(q):
    in_gap = q < TOP_K * SLAB_GAP
    gap_row = (q // SLAB_GAP) * T_PAD + T_ALL + q % SLAB_GAP
    return jnp.where(in_gap, gap_row, TAIL0 + q - TOP_K * SLAB_GAP).astype(jnp.int32)


def _routing_tables(idx, rank, cnt):
    def lookup(table, keys):
        ids = jnp.arange(table.shape[0], dtype=jnp.int32)
        return jnp.sum(jnp.where(keys[..., None] == ids, table, 0), axis=-1)

    counts = cnt[:, 0].astype(jnp.int32)
    padded = (counts + MOE_BLK - 1) // MOE_BLK * MOE_BLK
    pend = jnp.cumsum(padded)
    pstart = pend - padded

    dest = lookup(pstart, idx) + rank
    dest = jnp.pad(dest, ((0, 0), (0, DISP_STEPS * DISP_TILE - T_ALL)))
    disp = dest.reshape(TOP_K, DISP_STEPS, DISP_TILE).transpose(1, 0, 2).reshape(DISP_STEPS, TOP_K * DISP_TILE)

    n_pad = padded - counts
    pad_cum = jnp.cumsum(n_pad)
    q = jnp.arange(N_PAD_ROWS, dtype=jnp.int32)
    e_q = jnp.sum((q[:, None] >= pad_cum[None, :]).astype(jnp.int32), axis=1)
    first_pad = lookup(jnp.concatenate([pstart + counts, pend[-1:]]), e_q)
    pads_before = lookup(jnp.concatenate([pad_cum - n_pad, pad_cum[-1:]]), e_q)
    pad_rows = first_pad + q - pads_before

    pads = jnp.stack([pad_rows, _spare_row(2 * MOE_BLK + q)]) * ROW_TILE

    block_start = jnp.arange(N_BLOCKS, dtype=jnp.int32) * MOE_BLK
    block_exp = jnp.minimum(jnp.sum((block_start[:, None] >= pend[None, :]).astype(jnp.int32), axis=1),
                            N_EXPERTS - 1)
    blocks = jnp.arange(N_BLOCKS, dtype=jnp.int32)
    first = jnp.concatenate([jnp.ones((1,), jnp.bool_), block_exp[1:] != block_exp[:-1]])
    run_slot = (jnp.cumsum(first.astype(jnp.int32)) - 1) % 2
    later_first = first[None, :] & (blocks[None, :] > blocks[:, None])
    next_first = jnp.min(jnp.where(later_first, blocks[None, :], N_BLOCKS), axis=1)
    has_next = next_first < N_BLOCKS
    next_exp = lookup(block_exp, jnp.minimum(next_first, N_BLOCKS - 1))
    n_used = jnp.broadcast_to(pend[-1] // MOE_BLK, (N_BLOCKS,))
    plan = jnp.stack([block_exp, first.astype(jnp.int32), run_slot, next_exp, has_next.astype(jnp.int32),
                      n_used, jnp.zeros_like(n_used)]).astype(jnp.int32)
    return plan, disp * ROW_TILE, pads


def _dispatch_kernel(disp_hbm, pads_hbm, h2_ref, xs_hbm, inv_hbm, tbl0, tbl1, pad_tbl, inv_tbl, zero_tile,
                     rsem, tsem, psem):
    s = pl.program_id(0)
    ns = pl.num_programs(0)
    tbls = (tbl0, tbl1)

    def set_inverse(in_off, out_off):
        row = lax.shift_right_logical(in_off, ROW_TILE_SHIFT)
        inv_tbl[lax.shift_right_logical(row, INV_LANE_SHIFT), row & (INV_LANES - 1)] = out_off

    def table_copy(row, p):
        return pltpu.make_async_copy(disp_hbm.at[row], tbls[p], tsem.at[p])

    def row_copy(off, r, k):
        return pltpu.make_async_copy(h2_ref.at[pl.ds(r * ROW_TILE, ROW_TILE), :],
                                     xs_hbm.at[pl.ds(off, ROW_TILE), :], rsem.at[k])

    def pad_copy(off):
        return pltpu.make_async_copy(zero_tile, xs_hbm.at[pl.ds(off, ROW_TILE), :], psem.at[0])

    def scatter_rows(p, n_rows):
        first_out = s * (DISP_TILE * ROW_TILE)
        for r in range(n_rows):
            for k in range(TOP_K):
                off = pl.multiple_of(tbls[p][k * DISP_TILE + r], ROW_TILE)
                row_copy(off, r, k).start(priority=(r + k) % 2)
                set_inverse(off, first_out + (k * T_PAD + r) * ROW_TILE)
        for r in range(n_rows):
            for k in range(TOP_K):
                row_copy(0, r, k).wait()

    def fill_pads():
        zero_tile[...] = jnp.zeros_like(zero_tile)
        load = pltpu.make_async_copy(pads_hbm, pad_tbl, psem.at[1])
        load.start()
        load.wait()

        def chunk(i, carry):
            for j in range(PAD_CHUNK):
                off = pl.multiple_of(pad_tbl[0, i * PAD_CHUNK + j], ROW_TILE)
                pad_copy(off).start(priority=j % 2)
                set_inverse(off, pad_tbl[1, i * PAD_CHUNK + j])

            @pl.when(i > 0)
            def _():
                for j in range(PAD_CHUNK):
                    pad_copy(0).wait()
            return carry

        lax.fori_loop(0, N_PAD_ROWS // PAD_CHUNK, chunk, 0)
        for j in range(PAD_CHUNK):
            pad_copy(0).wait()
        done = pltpu.make_async_copy(inv_tbl, inv_hbm, psem.at[1])
        done.start()
        done.wait()

    @pl.when(s == 0)
    def _():
        table_copy(0, 0).start()

    def step(p):
        table_copy(s, p).wait()

        @pl.when(s < ns - 1)
        def _():
            table_copy(s + 1, 1 - p).start()
            scatter_rows(p, DISP_TILE)

        @pl.when(s == ns - 1)
        def _():
            scatter_rows(p, DISP_LAST_ROWS)
            fill_pads()

    parity = lax.rem(s, 2)
    pl.when(parity == 0)(lambda: step(0))
    pl.when(parity == 1)(lambda: step(1))


def _dispatch(disp, pads, h2):
    return pl.pallas_call(
        _dispatch_kernel,
        grid=(DISP_STEPS,),
        in_specs=[pl.BlockSpec(memory_space=pl.ANY),
                  pl.BlockSpec(memory_space=pl.ANY),
                  pl.BlockSpec((DISP_TILE * ROW_TILE, LANES), lambda s: (s, 0))],
        out_specs=[pl.BlockSpec(memory_space=pl.ANY), pl.BlockSpec(memory_space=pl.ANY)],
        out_shape=[jax.ShapeDtypeStruct((N_BLOCKS * MOE_BLK * ROW_TILE, LANES), _F32),
                   jax.ShapeDtypeStruct((N_BLOCKS * MOE_BLK // INV_LANES, INV_LANES), jnp.int32)],
        scratch_shapes=[
            pltpu.SMEM((TOP_K * DISP_TILE,), jnp.int32),
            pltpu.SMEM((TOP_K * DISP_TILE,), jnp.int32),
            pltpu.SMEM((2, N_PAD_ROWS), jnp.int32),
            pltpu.SMEM((N_BLOCKS * MOE_BLK // INV_LANES, INV_LANES), jnp.int32),
            pltpu.VMEM((ROW_TILE, LANES), _F32),
            pltpu.SemaphoreType.DMA((TOP_K,)),
            pltpu.SemaphoreType.DMA((2,)),
            pltpu.SemaphoreType.DMA((2,)),
        ],
        compiler_params=pltpu.CompilerParams(
            dimension_semantics=("arbitrary",), vmem_limit_bytes=VMEM_LIMIT),
        name="dispatch",
    )(disp, pads, h2)


PLAN_EXPERT, PLAN_FIRST, PLAN_SLOT, PLAN_NEXT, PLAN_HAS_NEXT, PLAN_USED, PLAN_ZERO = range(7)
SCATTER_GROUPS = 4


def _moe_kernel(plan_ref, table_hbm, x_ref, wgu_hbm, bgu_ref, wdn_hbm, bdn_ref, out_hbm,
                ybuf0, ybuf1, tbl0, tbl1, gu_stage0, gu_stage1, dn_stage0, dn_stage1, wgu_bf, wdn_bf,
                ssem, tsem, wsem):
    b = pl.program_id(0)
    nb = pl.num_programs(0)
    ybufs, tbls = (ybuf0, ybuf1), (tbl0, tbl1)
    gu_stages, dn_stages = (gu_stage0, gu_stage1), (dn_stage0, dn_stage1)
    n_used = plan_ref[PLAN_USED, 0]

    def table_copy(row, p):
        return pltpu.make_async_copy(table_hbm.at[row], tbls[p], tsem.at[p])

    def weight_copies(e, w):
        return (pltpu.make_async_copy(wgu_hbm.at[e], gu_stages[w], wsem.at[2 * w]),
                pltpu.make_async_copy(wdn_hbm.at[e], dn_stages[w], wsem.at[2 * w + 1]))

    def scatter_row(off, r, p):
        return pltpu.make_async_copy(ybufs[p].at[pl.ds(r * ROW_TILE, ROW_TILE), :],
                                     out_hbm.at[pl.ds(off, ROW_TILE), :], ssem.at[p])

    def start_scatter(tp, p, lo=0, hi=MOE_BLK, after=0):
        for r in range(lo, hi):
            off = pl.multiple_of(tbls[tp][r] + after, ROW_TILE)
            scatter_row(off, r, p).start(priority=ROW_DMA_PRIORITY)

    def zero_after(value):
        return lax.bitcast_convert_type(value, jnp.int32) * plan_ref[PLAN_ZERO, b]

    def wait_scatter(p):
        for r in range(MOE_BLK):
            scatter_row(0, r, p).wait()

    @pl.when(b == 0)
    def _():
        first = table_copy(0, 0)
        first.start()
        first.wait()
        ybuf0[...] = jnp.zeros_like(ybuf0)
        ybuf1[...] = jnp.zeros_like(ybuf1)
        start_scatter(0, 0)
        table_copy(1, 1).start()
        for c in weight_copies(plan_ref[PLAN_EXPERT, 0], 0):
            c.start(priority=WEIGHT_DMA_PRIORITY)

    def new_expert(w):
        for c in weight_copies(0, w):
            c.wait()

        @pl.when(plan_ref[PLAN_HAS_NEXT, b] == 1)
        def _():
            for c in weight_copies(plan_ref[PLAN_NEXT, b], 1 - w):
                c.start(priority=WEIGHT_DMA_PRIORITY)

        wgu_bf[...] = gu_stages[w][...].astype(_BF16)
        wdn_bf[...] = dn_stages[w][...].astype(_BF16)

    is_first = plan_ref[PLAN_FIRST, b] == 1
    wslot = plan_ref[PLAN_SLOT, b]
    pl.when(is_first & (wslot == 0))(lambda: new_expert(0))
    pl.when(is_first & (wslot == 1))(lambda: new_expert(1))

    def step(p):
        q = 1 - p
        table_copy(b + 1, q).wait()

        @pl.when(b < n_used)
        def _():
            group = MOE_BLK // SCATTER_GROUPS
            start_scatter(q, q, 0, group)
            table_copy(b + 2, p).start()
            x = _load_row_tiles(x_ref, MOE_BLK).astype(_BF16)
            hgu = jnp.dot(x, wgu_bf[...], preferred_element_type=_F32) + bgu_ref[...]
            for g in range(1, SCATTER_GROUPS):
                col = (2 * D_FF * g) // SCATTER_GROUPS - 1
                start_scatter(q, q, g * group, (g + 1) * group, zero_after(hgu[0, col]))
            gate = jnp.minimum(hgu[:, :D_FF], SWIGLU_LIMIT)
            up = jnp.clip(hgu[:, D_FF:], -SWIGLU_LIMIT, SWIGLU_LIMIT)
            act = gate * jax.nn.sigmoid(SWIGLU_ALPHA * gate) * (up + 1.0)
            y = jnp.dot(act.astype(_BF16), wdn_bf[...], preferred_element_type=_F32) + bdn_ref[...]
            wait_scatter(p)
            _store_row_tiles(ybufs[p], y)

        @pl.when(b >= n_used)
        def _():
            start_scatter(q, q)
            table_copy(b + 2, p).start()
            wait_scatter(p)
            ybufs[p][...] = jnp.zeros_like(ybufs[p])

        @pl.when(b == nb - 1)
        def _():
            table_copy(b + 2, p).wait()
            start_scatter(p, p)
            wait_scatter(p)
            wait_scatter(q)

    parity = lax.rem(b, 2)
    pl.when(parity == 0)(lambda: step(0))
    pl.when(parity == 1)(lambda: step(1))


def _moe_experts(plan, table, xs, wgu, bgu, wdn, bdn):
    grid_spec = pltpu.PrefetchScalarGridSpec(
        num_scalar_prefetch=1,
        grid=(N_BLOCKS,),
        in_specs=[
            pl.BlockSpec(memory_space=pl.ANY),
            pl.BlockSpec((MOE_BLK * ROW_TILE, LANES), lambda b, plan: (b, 0)),
            pl.BlockSpec(memory_space=pl.ANY),
            pl.BlockSpec((None, 1, 2 * D_FF), lambda b, plan: (plan[PLAN_EXPERT, b], 0, 0)),
            pl.BlockSpec(memory_space=pl.ANY),
            pl.BlockSpec((None, 1, D_MODEL), lambda b, plan: (plan[PLAN_EXPERT, b], 0, 0)),
        ],
        out_specs=pl.BlockSpec(memory_space=pl.ANY),
        scratch_shapes=[
            pltpu.VMEM((MOE_BLK * ROW_TILE, LANES), _F32),
            pltpu.VMEM((MOE_BLK * ROW_TILE, LANES), _F32),
            pltpu.SMEM((MOE_BLK,), jnp.int32),
            pltpu.SMEM((MOE_BLK,), jnp.int32),
            pltpu.VMEM((D_MODEL, 2 * D_FF), _F32),
            pltpu.VMEM((D_MODEL, 2 * D_FF), _F32),
            pltpu.VMEM((D_FF, D_MODEL), _F32),
            pltpu.VMEM((D_FF, D_MODEL), _F32),
            pltpu.VMEM((D_MODEL, 2 * D_FF), _BF16),
            pltpu.VMEM((D_FF, D_MODEL), _BF16),
            pltpu.SemaphoreType.DMA((2,)),
            pltpu.SemaphoreType.DMA((2,)),
            pltpu.SemaphoreType.DMA((4,)),
        ],
    )
    return pl.pallas_call(
        _moe_kernel,
        grid_spec=grid_spec,
        out_shape=jax.ShapeDtypeStruct((OUT_ROWS * ROW_TILE, LANES), _F32),
        compiler_params=pltpu.CompilerParams(
            dimension_semantics=("arbitrary",), vmem_limit_bytes=VMEM_LIMIT),
        name="moe_experts",
    )(plan, table, xs, wgu, bgu, wdn, bdn)


def _combine_kernel(x1_ref, g_ref, o0_ref, o1_ref, o2_ref, o3_ref, gfin_ref, yp_ref, ys_ref):
    i = pl.program_id(0)
    g = g_ref[...]
    o = [_load_row_tiles(r, COMB_TILE) for r in (o0_ref, o1_ref, o2_ref, o3_ref)]
    moe = (g[:, 0:1] * o[0] + g[:, 1:2] * o[1]) + (g[:, 2:3] * o[2] + g[:, 3:4] * o[3])
    y = _rms(x1_ref[...] + moe, gfin_ref[...])

    @pl.when(i < COMB_PROMPT_STEPS)
    def _():
        yp_ref[...] = y

    @pl.when(i == COMB_PROMPT_STEPS)
    def _():
        ys_ref[...] = y[:DEC_BATCH, :]


def _combine(x1, gates_tk, out4, g_final):
    slab_blocks = T_PAD // COMB_TILE

    def slab_spec(k):
        return pl.BlockSpec((COMB_TILE * ROW_TILE, LANES), lambda i, k=k: (k * slab_blocks + i, 0))

    return pl.pallas_call(
        _combine_kernel,
        grid=(COMB_PROMPT_STEPS + 1,),
        in_specs=[pl.BlockSpec((COMB_TILE, D_MODEL), lambda i: (i, 0)),
                  pl.BlockSpec((COMB_TILE, TOP_K), lambda i: (i, 0)),
                  slab_spec(0), slab_spec(1), slab_spec(2), slab_spec(3),
                  _const_spec(g_final.shape)],
        out_specs=[pl.BlockSpec((COMB_TILE, D_MODEL), lambda i: (jnp.minimum(i, COMB_PROMPT_STEPS - 1), 0)),
                   _const_spec((DEC_BATCH, D_MODEL))],
        out_shape=[jax.ShapeDtypeStruct((T_PROMPT, D_MODEL), _F32),
                   jax.ShapeDtypeStruct((DEC_BATCH, D_MODEL), _F32)],
        compiler_params=pltpu.CompilerParams(
            dimension_semantics=("arbitrary",), vmem_limit_bytes=VMEM_LIMIT),
        name="combine",
    )(x1, gates_tk, out4, out4, out4, out4, g_final)


def kernel(x_prompt, x_sample, state_pool, state_conv, meta_tokens, g_mix, w_in, w_pool_grp, pool_scale,
           w_pool_up, w_conv, w_conv_out, w_o, g_ffn, w_router, b_router, w_gu, b_gu, w_down, b_down, g_final):
    assert g_mix.shape[0] == 1, "single-layer step"
    grp = w_pool_grp[0].astype(_BF16)
    zero = jnp.zeros((POOL_GROUP, POOL_GROUP), _BF16)
    wgrp = jnp.stack([jnp.block([[grp[0], zero], [zero, grp[1]]]),
                      jnp.block([[grp[2], zero], [zero, grp[3]]])])
    w = (
        g_mix[0].reshape(1, D_MODEL),
        w_in[0].astype(_BF16),
        wgrp,
        pool_scale[0].reshape(1, D_POOL),
        w_pool_up[0].astype(_BF16),
        w_conv[0],
        w_conv_out[0].astype(_BF16),
        w_o[0].astype(_BF16),
        g_ffn[0].reshape(1, D_MODEL),
        w_router[0].T,
        b_router[0].reshape(N_EXPERTS, 1),
    )
    xs_pad = jnp.pad(x_sample.reshape(DEC_BATCH, D_MODEL), ((0, MIX_TILE - DEC_BATCH), (0, 0)))
    spool_t = jnp.transpose(state_pool[0], (1, 0, 2))
    sconv_t = jnp.transpose(state_conv[0], (1, 0, 2))
    x1, h2, idx, gate, rank, cnt, pool_p, conv_p, u_s, cv_s = _mixer(
        x_prompt, xs_pad, spool_t, sconv_t, meta_tokens, w)

    plan, disp, pads = _routing_tables(idx, rank, cnt)
    xs, inv = _dispatch(disp, pads, h2)
    warmup = _spare_row(jnp.arange(2 * MOE_BLK, dtype=jnp.int32)).reshape(2, MOE_BLK) * ROW_TILE
    dst_rows = jnp.concatenate([warmup, inv.reshape(N_BLOCKS, MOE_BLK)], axis=0)
    out4 = _moe_experts(plan, dst_rows, xs,
                        w_gu.reshape(N_EXPERTS, D_MODEL, 2 * D_FF), b_gu.reshape(N_EXPERTS, 1, 2 * D_FF),
                        w_down.reshape(N_EXPERTS, D_FF, D_MODEL), b_down.reshape(N_EXPERTS, 1, D_MODEL))

    y_p, y_s = _combine(x1, gate.T, out4, g_final.reshape(1, D_MODEL))

    new_pool_p = pool_p[:, POOL_HALO - POOL_BUF:, :][None]
    new_conv_p = conv_p[:, CONV_HALO - CONV_BUF:, :][None]
    new_pool_s = jnp.concatenate([state_pool[0][:, 1:, :], u_s[:, None, :]], axis=1)[None]
    new_conv_s = jnp.concatenate([state_conv[0][:, 1:, :], cv_s[:, None, :]], axis=1)[None]
    return (y_p.reshape(BATCH, SEQ, D_MODEL), y_s.reshape(DEC_BATCH, 1, D_MODEL),
            new_pool_p, new_conv_p, new_pool_s, new_conv_s)
```

```python
import jax
import jax.numpy as jnp
from jax import lax
from jax.experimental import pallas as pl
from jax.experimental.pallas import tpu as pltpu

D_MODEL = 1024
BATCH = 8
SEQ = 2048
DEC_BATCH = 128
N_META = 16
D_POOL = 512
POOL_GROUP = 128
POOL_WINDOWS = (2, 4, 8, 16)
POOL_BUF = 15
D_CONV = 512
CONV_BUF = 2
N_EXPERTS = 32
TOP_K = 4
D_FF = 1024
SWIGLU_LIMIT = 7.0
SWIGLU_ALPHA = 1.702
RMS_EPS = 1e-5

O_CG = D_POOL
O_BG = O_CG + D_CONV
O_V = O_BG + D_CONV
O_GL = O_V + D_CONV

T_PROMPT = BATCH * SEQ
T_ALL = T_PROMPT + DEC_BATCH
N_ASSIGN = TOP_K * T_ALL

MIX_TILE = 512
MIX_PROMPT_STEPS = T_PROMPT // MIX_TILE
SEQ_TILES = SEQ // MIX_TILE
POOL_HALO = 16
CONV_HALO = 8
DISP_TILE = 256
DISP_STEPS = -(-T_ALL // DISP_TILE)
DISP_LAST_ROWS = T_ALL - (DISP_STEPS - 1) * DISP_TILE
MOE_BLK = 256
N_BLOCKS = -(-N_ASSIGN // MOE_BLK) + N_EXPERTS
N_PAD_ROWS = N_BLOCKS * MOE_BLK - N_ASSIGN
PAD_CHUNK = 64
assert N_PAD_ROWS % PAD_CHUNK == 0
ROW_DMA_PRIORITY = 0
WEIGHT_DMA_PRIORITY = 1
COMB_TILE = 512
COMB_PROMPT_STEPS = T_PROMPT // COMB_TILE
T_PAD = (T_ALL + COMB_TILE - 1) // COMB_TILE * COMB_TILE
SLAB_GAP = T_PAD - T_ALL
OUT_ROWS = (N_BLOCKS + 2) * MOE_BLK
TAIL0 = TOP_K * T_PAD
assert OUT_ROWS >= TAIL0

ROW_TILE = 8
ROW_TILE_SHIFT = 3
LANES = D_MODEL // ROW_TILE
assert 1 << ROW_TILE_SHIFT == ROW_TILE

VMEM_LIMIT = 56 * 1024 * 1024

_F32 = jnp.float32
_BF16 = jnp.bfloat16


def _rms(x, g):
    ms = jnp.mean(x * x, axis=-1, keepdims=True)
    return x * lax.rsqrt(ms + RMS_EPS) * g


def _bdot(a, w):
    return jnp.dot(a.astype(_BF16), w, preferred_element_type=_F32)


def _store_row_tiles(ref, val):
    rows = val.shape[0]
    for c in range(ROW_TILE):
        ref[pl.ds(c, rows, stride=ROW_TILE), :] = val[:, c * LANES:(c + 1) * LANES]


def _load_row_tiles(ref, rows):
    return jnp.concatenate([ref[pl.ds(c, rows, stride=ROW_TILE), :] for c in range(ROW_TILE)], axis=-1)


def _branches(z, pm, conv, wgrp_ref, pscale_ref, wup_ref, wcout_ref):
    pmb = pm.astype(_BF16)
    half = 2 * POOL_GROUP
    pg = jnp.concatenate(
        [jnp.dot(pmb[:, :half], wgrp_ref[0], preferred_element_type=_F32),
         jnp.dot(pmb[:, half:], wgrp_ref[1], preferred_element_type=_F32)], axis=-1)
    branch_a = _bdot(pg * pscale_ref[...], wup_ref[...])
    branch_b = _bdot(z[:, O_BG:O_V] * conv, wcout_ref[...])
    return branch_a, branch_b


def _merge_and_route(x, z, branch_a, branch_b, wo_ref, gffn_ref, wrt_ref, br_ref, base_ref, n_live):
    gates = jax.nn.sigmoid(z[:, O_GL:])
    merged = gates[:, :D_MODEL] * branch_a + gates[:, D_MODEL:] * branch_b
    x1 = x + _bdot(merged, wo_ref[...])
    h2 = _rms(x1, gffn_ref[...])
    logits = lax.dot_general(wrt_ref[...], h2, (((1,), (1,)), ((), ())),
                             precision=lax.Precision.HIGHEST,
                             preferred_element_type=_F32) + br_ref[...]
    iota = lax.broadcasted_iota(jnp.int32, logits.shape, 0)
    vals, idxs = [], []
    cur = logits
    for _ in range(TOP_K):
        m = jnp.max(cur, axis=0, keepdims=True)
        ik = jnp.min(jnp.where(cur == m, iota, N_EXPERTS), axis=0, keepdims=True)
        vals.append(m)
        idxs.append(ik)
        cur = jnp.where(iota == ik, -jnp.inf, cur)
    exps = [jnp.exp(v - vals[0]) for v in vals]
    denom = (exps[0] + exps[1]) + (exps[2] + exps[3])
    gate = [e / denom for e in exps]
    rows = logits.shape[1]
    live = lax.broadcasted_iota(jnp.int32, logits.shape, 1) < n_live
    onehots = [jnp.where(live, jnp.where(iota == ik, 1.0, 0.0), 0.0) for ik in idxs]
    member = (onehots[0] + onehots[1]) + (onehots[2] + onehots[3])
    earlier = (lax.broadcasted_iota(jnp.int32, (rows, rows), 0)
               < lax.broadcasted_iota(jnp.int32, (rows, rows), 1))
    before = jnp.dot(member.astype(_BF16), jnp.where(earlier, 1.0, 0.0).astype(_BF16),
                     preferred_element_type=_F32)
    pos = base_ref[...] + before
    ranks = [jnp.sum(oh * pos, axis=0, keepdims=True).astype(jnp.int32) for oh in onehots]
    base_ref[...] = base_ref[...] + jnp.sum(member, axis=1, keepdims=True)
    return x1, h2, idxs, gate, ranks


def _mixer_kernel(xp_ref, xs_ref, spool_ref, sconv_ref, meta_ref, gmix_ref, win_ref, wgrp_ref, pscale_ref,
                  wup_ref, wconv_ref, wcout_ref, wo_ref, gffn_ref, wrt_ref, br_ref,
                  x1_ref, h2_ref, idx_ref, gate_ref, rank_ref, cnt_ref, pool_out_ref, conv_out_ref, us_ref, cvs_ref,
                  pool_ext, conv_ext, meta_pool, meta_conv, pm_scr, conv_scr, base_scr):
    s = pl.program_id(0)
    is_prompt = s < MIX_PROMPT_STEPS
    j = lax.rem(s, SEQ_TILES)

    @pl.when(s == 0)
    def _():
        hm = _rms(meta_ref[...], gmix_ref[...]).astype(_BF16)
        zm = jnp.dot(hm, win_ref[:, :O_GL], preferred_element_type=_F32)
        meta_pool[...] = zm[:, :D_POOL]
        meta_conv[...] = zm[:, O_CG:O_BG] * zm[:, O_V:O_GL]
        base_scr[...] = jnp.zeros_like(base_scr)

    @pl.when(is_prompt & (j == 0))
    def _():
        pool_ext[0:POOL_HALO, :] = meta_pool[...]
        conv_ext[0:CONV_HALO, :] = meta_conv[N_META - CONV_HALO:, :]

    x = jnp.where(is_prompt, xp_ref[...], xs_ref[...])
    h = _rms(x, gmix_ref[...]).astype(_BF16)
    z = jnp.dot(h, win_ref[...], preferred_element_type=_F32)
    u = z[:, :D_POOL]
    cv = z[:, O_CG:O_BG] * z[:, O_V:O_GL]
    wc = wconv_ref[...]

    @pl.when(is_prompt)
    def _():
        pool_ext[POOL_HALO:, :] = u
        ext = pool_ext[...]
        for g, w in enumerate(POOL_WINDOWS):
            lanes = slice(g * POOL_GROUP, (g + 1) * POOL_GROUP)
            acc = ext[:, lanes]
            sh = 1
            while sh < w:
                acc = acc + pltpu.roll(acc, sh, 0)
                sh *= 2
            pm_scr[:, lanes] = acc[POOL_HALO:, :] * (1.0 / w) - u[:, lanes]
        conv_ext[CONV_HALO:, :] = cv
        cext = conv_ext[...]
        conv_scr[...] = (wc[0:1, :] * pltpu.roll(cext, 2, 0)[CONV_HALO:, :]
                         + wc[1:2, :] * pltpu.roll(cext, 1, 0)[CONV_HALO:, :]) + wc[2:3, :] * cv
        pool_ext[0:POOL_HALO, :] = u[MIX_TILE - POOL_HALO:, :]
        conv_ext[0:CONV_HALO, :] = cv[MIX_TILE - CONV_HALO:, :]

    @pl.when(is_prompt & (j == SEQ_TILES - 1))
    def _():
        pool_out_ref[...] = u[MIX_TILE - POOL_HALO:, :]
        conv_out_ref[...] = cv[MIX_TILE - CONV_HALO:, :]

    @pl.when(jnp.logical_not(is_prompt))
    def _():
        us = u[:DEC_BATCH, :]
        cvs = cv[:DEC_BATCH, :]
        pm_scr[...] = jnp.zeros_like(pm_scr)
        conv_scr[...] = jnp.zeros_like(conv_scr)
        for g, w in enumerate(POOL_WINDOWS):
            lanes = slice(g * POOL_GROUP, (g + 1) * POOL_GROUP)
            acc = us[:, lanes]
            for t in range(POOL_BUF - (w - 1), POOL_BUF):
                acc = acc + spool_ref[t, :, lanes]
            pm_scr[0:DEC_BATCH, lanes] = acc * (1.0 / w) - us[:, lanes]
        conv_scr[0:DEC_BATCH, :] = (wc[0:1, :] * sconv_ref[0] + wc[1:2, :] * sconv_ref[1]) + wc[2:3, :] * cvs
        us_ref[...] = us
        cvs_ref[...] = cvs

    branch_a, branch_b = _branches(z, pm_scr[...], conv_scr[...], wgrp_ref, pscale_ref, wup_ref, wcout_ref)
    n_live = jnp.where(is_prompt, MIX_TILE, DEC_BATCH)
    x1, h2, idxs, gate, ranks = _merge_and_route(x, z, branch_a, branch_b, wo_ref, gffn_ref, wrt_ref, br_ref,
                                                 base_scr, n_live)
    x1_ref[...] = x1
    _store_row_tiles(h2_ref, h2)
    for k in range(TOP_K):
        idx_ref[k:k + 1, :] = idxs[k]
        gate_ref[k:k + 1, :] = gate[k]
        rank_ref[k:k + 1, :] = ranks[k]

    @pl.when(jnp.logical_not(is_prompt))
    def _():
        cnt_ref[...] = jnp.broadcast_to(base_scr[...], cnt_ref.shape)


def _const_spec(shape):
    nd = len(shape)
    return pl.BlockSpec(shape, lambda *_: (0,) * nd)


def _resident_spec(shape):
    nd = len(shape)
    return pl.BlockSpec(shape, lambda *_: (0,) * nd, pipeline_mode=pl.Buffered(1))


def _mixer(x_prompt, xs_pad, spool_t, sconv_t, meta, w):
    last = MIX_PROMPT_STEPS - 1

    def x_map(s):
        sp = jnp.minimum(s, last)
        return (sp // SEQ_TILES, sp % SEQ_TILES, 0)

    def seq_map(s):
        return (jnp.minimum(s, last) // SEQ_TILES, 0, 0)

    consts = (xs_pad, spool_t, sconv_t, meta) + tuple(w)
    return pl.pallas_call(
        _mixer_kernel,
        grid=(MIX_PROMPT_STEPS + 1,),
        in_specs=[pl.BlockSpec((None, MIX_TILE, D_MODEL), x_map)] + [_resident_spec(a.shape) for a in consts],
        out_specs=[
            pl.BlockSpec((MIX_TILE, D_MODEL), lambda s: (s, 0)),
            pl.BlockSpec((MIX_TILE * ROW_TILE, LANES), lambda s: (s, 0)),
            pl.BlockSpec((TOP_K, MIX_TILE), lambda s: (0, s)),
            pl.BlockSpec((TOP_K, MIX_TILE), lambda s: (0, s)),
            pl.BlockSpec((TOP_K, MIX_TILE), lambda s: (0, s)),
            _const_spec((N_EXPERTS, 128)),
            pl.BlockSpec((None, POOL_HALO, D_POOL), seq_map),
            pl.BlockSpec((None, CONV_HALO, D_CONV), seq_map),
            _const_spec((DEC_BATCH, D_POOL)),
            _const_spec((DEC_BATCH, D_CONV)),
        ],
        out_shape=[
            jax.ShapeDtypeStruct((T_ALL, D_MODEL), _F32),
            jax.ShapeDtypeStruct((T_ALL * ROW_TILE, LANES), _F32),
            jax.ShapeDtypeStruct((TOP_K, T_ALL), jnp.int32),
            jax.ShapeDtypeStruct((TOP_K, T_ALL), _F32),
            jax.ShapeDtypeStruct((TOP_K, T_ALL), jnp.int32),
            jax.ShapeDtypeStruct((N_EXPERTS, 128), _F32),
            jax.ShapeDtypeStruct((BATCH, POOL_HALO, D_POOL), _F32),
            jax.ShapeDtypeStruct((BATCH, CONV_HALO, D_CONV), _F32),
            jax.ShapeDtypeStruct((DEC_BATCH, D_POOL), _F32),
            jax.ShapeDtypeStruct((DEC_BATCH, D_CONV), _F32),
        ],
        scratch_shapes=[
            pltpu.VMEM((POOL_HALO + MIX_TILE, D_POOL), _F32),
            pltpu.VMEM((CONV_HALO + MIX_TILE, D_CONV), _F32),
            pltpu.VMEM((N_META, D_POOL), _F32),
            pltpu.VMEM((N_META, D_CONV), _F32),
            pltpu.VMEM((MIX_TILE, D_POOL), _F32),
            pltpu.VMEM((MIX_TILE, D_CONV), _F32),
            pltpu.VMEM((N_EXPERTS, 1), _F32),
        ],
        compiler_params=pltpu.CompilerParams(
            dimension_semantics=("arbitrary",), vmem_limit_bytes=VMEM_LIMIT),
        name="mixer",
    )(x_prompt, *consts)


def _spare_row(q):
    in_gap = q < TOP_K * SLAB_GAP
    gap_row = (q // SLAB_GAP) * T_PAD + T_ALL + q % SLAB_GAP
    return jnp.where(in_gap, gap_row, TAIL0 + q - TOP_K * SLAB_GAP).astype(jnp.int32)


def _routing_tables(idx, rank, cnt):
    def lookup(table, keys):
        ids = jnp.arange(table.shape[0], dtype=jnp.int32)
        return jnp.sum(jnp.where(keys[..., None] == ids, table, 0), axis=-1)

    counts = cnt[:, 0].astype(jnp.int32)
    padded = (counts + MOE_BLK - 1) // MOE_BLK * MOE_BLK
    pend = jnp.cumsum(padded)
    pstart = pend - padded

    dest = lookup(pstart, idx) + rank
    dest = jnp.pad(dest, ((0, 0), (0, DISP_STEPS * DISP_TILE - T_ALL)))
    disp = dest.reshape(TOP_K, DISP_STEPS, DISP_TILE).transpose(1, 0, 2).reshape(DISP_STEPS, TOP_K * DISP_TILE)

    n_pad = padded - counts
    pad_cum = jnp.cumsum(n_pad)
    q = jnp.arange(N_PAD_ROWS, dtype=jnp.int32)
    e_q = jnp.sum((q[:, None] >= pad_cum[None, :]).astype(jnp.int32), axis=1)
    first_pad = lookup(jnp.concatenate([pstart + counts, pend[-1:]]), e_q)
    pads_before = lookup(jnp.concatenate([pad_cum - n_pad, pad_cum[-1:]]), e_q)
    pad_rows = first_pad + q - pads_before

    pads = jnp.concatenate([pad_rows, _spare_row(2 * MOE_BLK + q)]) * ROW_TILE

    block_start = jnp.arange(N_BLOCKS, dtype=jnp.int32) * MOE_BLK
    block_exp = jnp.minimum(jnp.sum((block_start[:, None] >= pend[None, :]).astype(jnp.int32), axis=1),
                            N_EXPERTS - 1)
    blocks = jnp.arange(N_BLOCKS, dtype=jnp.int32)
    first = jnp.concatenate([jnp.ones((1,), jnp.bool_), block_exp[1:] != block_exp[:-1]])
    run_slot = (jnp.cumsum(first.astype(jnp.int32)) - 1) % 2
    later_first = first[None, :] & (blocks[None, :] > blocks[:, None])
    next_first = jnp.min(jnp.where(later_first, blocks[None, :], N_BLOCKS), axis=1)
    has_next = next_first < N_BLOCKS
    next_exp = lookup(block_exp, jnp.minimum(next_first, N_BLOCKS - 1))
    n_used = jnp.broadcast_to(pend[-1] // MOE_BLK, (N_BLOCKS,))
    plan = jnp.stack([block_exp, first.astype(jnp.int32), run_slot, next_exp, has_next.astype(jnp.int32),
                      n_used, jnp.zeros_like(n_used)]).astype(jnp.int32)
    return plan, disp * ROW_TILE, pads


def _dispatch_kernel(disp_hbm, pads_hbm, h2_ref, xs_hbm, inv_hbm, tbl0, tbl1, pad_tbl, inv_tbl, zero_tile,
                     rsem, tsem, psem):
    s = pl.program_id(0)
    ns = pl.num_programs(0)
    tbls = (tbl0, tbl1)

    def set_inverse(in_off, out_off):
        inv_tbl[lax.shift_right_logical(in_off, ROW_TILE_SHIFT)] = out_off

    def table_copy(row, p):
        return pltpu.make_async_copy(disp_hbm.at[row], tbls[p], tsem.at[p])

    def row_copy(off, r, k):
        return pltpu.make_async_copy(h2_ref.at[pl.ds(r * ROW_TILE, ROW_TILE), :],
                                     xs_hbm.at[pl.ds(off, ROW_TILE), :], rsem.at[k])

    def pad_copy(off):
        return pltpu.make_async_copy(zero_tile, xs_hbm.at[pl.ds(off, ROW_TILE), :], psem.at[0])

    def scatter_rows(p, n_rows):
        first_out = s * (DISP_TILE * ROW_TILE)
        for r in range(n_rows):
            for k in range(TOP_K):
                off = pl.multiple_of(tbls[p][k * DISP_TILE + r], ROW_TILE)
                row_copy(off, r, k).start(priority=(r + k) % 2)
                set_inverse(off, first_out + (k * T_PAD + r) * ROW_TILE)
        for r in range(n_rows):
            for k in range(TOP_K):
                row_copy(0, r, k).wait()

    def fill_pads():
        zero_tile[...] = jnp.zeros_like(zero_tile)
        load = pltpu.make_async_copy(pads_hbm, pad_tbl, psem.at[1])
        load.start()
        load.wait()

        def chunk(i, carry):
            for j in range(PAD_CHUNK):
                off = pl.multiple_of(pad_tbl[i * PAD_CHUNK + j], ROW_TILE)
                pad_copy(off).start(priority=j % 2)
                set_inverse(off, pad_tbl[N_PAD_ROWS + i * PAD_CHUNK + j])

            @pl.when(i > 0)
            def _():
                for j in range(PAD_CHUNK):
                    pad_copy(0).wait()
            return carry

        lax.fori_loop(0, N_PAD_ROWS // PAD_CHUNK, chunk, 0)
        for j in range(PAD_CHUNK):
            pad_copy(0).wait()
        done = pltpu.make_async_copy(inv_tbl, inv_hbm, psem.at[1])
        done.start()
        done.wait()

    @pl.when(s == 0)
    def _():
        table_copy(0, 0).start()

    def step(p):
        table_copy(s, p).wait()

        @pl.when(s < ns - 1)
        def _():
            table_copy(s + 1, 1 - p).start()
            scatter_rows(p, DISP_TILE)

        @pl.when(s == ns - 1)
        def _():
            scatter_rows(p, DISP_LAST_ROWS)
            fill_pads()

    parity = lax.rem(s, 2)
    pl.when(parity == 0)(lambda: step(0))
    pl.when(parity == 1)(lambda: step(1))


def _dispatch(disp, pads, h2):
    return pl.pallas_call(
        _dispatch_kernel,
        grid=(DISP_STEPS,),
        in_specs=[pl.BlockSpec(memory_space=pl.ANY),
                  pl.BlockSpec(memory_space=pl.ANY),
                  pl.BlockSpec((DISP_TILE * ROW_TILE, LANES), lambda s: (s, 0))],
        out_specs=[pl.BlockSpec(memory_space=pl.ANY), pl.BlockSpec(memory_space=pl.ANY)],
        out_shape=[jax.ShapeDtypeStruct((N_BLOCKS * MOE_BLK * ROW_TILE, LANES), _F32),
                   jax.ShapeDtypeStruct((N_BLOCKS * MOE_BLK,), jnp.int32)],
        scratch_shapes=[
            pltpu.SMEM((TOP_K * DISP_TILE,), jnp.int32),
            pltpu.SMEM((TOP_K * DISP_TILE,), jnp.int32),
            pltpu.SMEM((2 * N_PAD_ROWS,), jnp.int32),
            pltpu.SMEM((N_BLOCKS * MOE_BLK,), jnp.int32),
            pltpu.VMEM((ROW_TILE, LANES), _F32),
            pltpu.SemaphoreType.DMA((TOP_K,)),
            pltpu.SemaphoreType.DMA((2,)),
            pltpu.SemaphoreType.DMA((2,)),
        ],
        compiler_params=pltpu.CompilerParams(
            dimension_semantics=("arbitrary",), vmem_limit_bytes=VMEM_LIMIT),
        name="dispatch",
    )(disp, pads, h2)


PLAN_EXPERT, PLAN_FIRST, PLAN_SLOT, PLAN_NEXT, PLAN_HAS_NEXT, PLAN_USED, PLAN_ZERO = range(7)
SCATTER_GROUPS = 4


def _moe_kernel(plan_ref, table_hbm, x_ref, wgu_hbm, bgu_ref, wdn_hbm, bdn_ref, out_hbm,
                ybuf0, ybuf1, tbl0, tbl1, gu_stage0, gu_stage1, dn_stage0, dn_stage1, wgu_bf, wdn_bf,
                ssem, tsem, wsem):
    b = pl.program_id(0)
    nb = pl.num_programs(0)
    ybufs, tbls = (ybuf0, ybuf1), (tbl0, tbl1)
    gu_stages, dn_stages = (gu_stage0, gu_stage1), (dn_stage0, dn_stage1)
    n_used = plan_ref[PLAN_USED, 0]

    def table_copy(row, p):
        return pltpu.make_async_copy(table_hbm.at[row], tbls[p], tsem.at[p])

    def weight_copies(e, w):
        return (pltpu.make_async_copy(wgu_hbm.at[e], gu_stages[w], wsem.at[2 * w]),
                pltpu.make_async_copy(wdn_hbm.at[e], dn_stages[w], wsem.at[2 * w + 1]))

    def scatter_row(off, r, p):
        return pltpu.make_async_copy(ybufs[p].at[pl.ds(r * ROW_TILE, ROW_TILE), :],
                                     out_hbm.at[pl.ds(off, ROW_TILE), :], ssem.at[p])

    def start_scatter(tp, p, lo=0, hi=MOE_BLK, after=0):
        for r in range(lo, hi):
            off = pl.multiple_of(tbls[tp][r] + after, ROW_TILE)
            scatter_row(off, r, p).start(priority=ROW_DMA_PRIORITY)

    def zero_after(value):
        return lax.bitcast_convert_type(value, jnp.int32) * plan_ref[PLAN_ZERO, b]

    def wait_scatter(p):
        for r in range(MOE_BLK):
            scatter_row(0, r, p).wait()

    @pl.when(b == 0)
    def _():
        first = table_copy(0, 0)
        first.start()
        first.wait()
        ybuf0[...] = jnp.zeros_like(ybuf0)
        ybuf1[...] = jnp.zeros_like(ybuf1)
        start_scatter(0, 0)
        table_copy(1, 1).start()
        for c in weight_copies(plan_ref[PLAN_EXPERT, 0], 0):
            c.start(priority=WEIGHT_DMA_PRIORITY)

    def new_expert(w):
        for c in weight_copies(0, w):
            c.wait()

        @pl.when(plan_ref[PLAN_HAS_NEXT, b] == 1)
        def _():
            for c in weight_copies(plan_ref[PLAN_NEXT, b], 1 - w):
                c.start(priority=WEIGHT_DMA_PRIORITY)

        wgu_bf[...] = gu_stages[w][...].astype(_BF16)
        wdn_bf[...] = dn_stages[w][...].astype(_BF16)

    is_first = plan_ref[PLAN_FIRST, b] == 1
    wslot = plan_ref[PLAN_SLOT, b]
    pl.when(is_first & (wslot == 0))(lambda: new_expert(0))
    pl.when(is_first & (wslot == 1))(lambda: new_expert(1))

    def step(p):
        q = 1 - p
        table_copy(b + 1, q).wait()

        @pl.when(b < n_used)
        def _():
            group = MOE_BLK // SCATTER_GROUPS
            start_scatter(q, q, 0, group)
            table_copy(b + 2, p).start()
            x = _load_row_tiles(x_ref, MOE_BLK).astype(_BF16)
            hgu = jnp.dot(x, wgu_bf[...], preferred_element_type=_F32) + bgu_ref[...]
            for g in range(1, SCATTER_GROUPS):
                col = (2 * D_FF * g) // SCATTER_GROUPS - 1
                start_scatter(q, q, g * group, (g + 1) * group, zero_after(hgu[0, col]))
            gate = jnp.minimum(hgu[:, :D_FF], SWIGLU_LIMIT)
            up = jnp.clip(hgu[:, D_FF:], -SWIGLU_LIMIT, SWIGLU_LIMIT)
            act = gate * jax.nn.sigmoid(SWIGLU_ALPHA * gate) * (up + 1.0)
            y = jnp.dot(act.astype(_BF16), wdn_bf[...], preferred_element_type=_F32) + bdn_ref[...]
            wait_scatter(p)
            _store_row_tiles(ybufs[p], y)

        @pl.when(b >= n_used)
        def _():
            start_scatter(q, q)
            table_copy(b + 2, p).start()
            wait_scatter(p)
            ybufs[p][...] = jnp.zeros_like(ybufs[p])

        @pl.when(b == nb - 1)
        def _():
            table_copy(b + 2, p).wait()
            start_scatter(p, p)
            wait_scatter(p)
            wait_scatter(q)

    parity = lax.rem(b, 2)
    pl.when(parity == 0)(lambda: step(0))
    pl.when(parity == 1)(lambda: step(1))


def _moe_experts(plan, table, xs, wgu, bgu, wdn, bdn):
    grid_spec = pltpu.PrefetchScalarGridSpec(
        num_scalar_prefetch=1,
        grid=(N_BLOCKS,),
        in_specs=[
            pl.BlockSpec(memory_space=pl.ANY),
            pl.BlockSpec((MOE_BLK * ROW_TILE, LANES), lambda b, plan: (b, 0)),
            pl.BlockSpec(memory_space=pl.ANY),
            pl.BlockSpec((None, 1, 2 * D_FF), lambda b, plan: (plan[PLAN_EXPERT, b], 0, 0)),
            pl.BlockSpec(memory_space=pl.ANY),
            pl.BlockSpec((None, 1, D_MODEL), lambda b, plan: (plan[PLAN_EXPERT, b], 0, 0)),
        ],
        out_specs=pl.BlockSpec(memory_space=pl.ANY),
        scratch_shapes=[
            pltpu.VMEM((MOE_BLK * ROW_TILE, LANES), _F32),
            pltpu.VMEM((MOE_BLK * ROW_TILE, LANES), _F32),
            pltpu.SMEM((MOE_BLK,), jnp.int32),
            pltpu.SMEM((MOE_BLK,), jnp.int32),
            pltpu.VMEM((D_MODEL, 2 * D_FF), _F32),
            pltpu.VMEM((D_MODEL, 2 * D_FF), _F32),
            pltpu.VMEM((D_FF, D_MODEL), _F32),
            pltpu.VMEM((D_FF, D_MODEL), _F32),
            pltpu.VMEM((D_MODEL, 2 * D_FF), _BF16),
            pltpu.VMEM((D_FF, D_MODEL), _BF16),
            pltpu.SemaphoreType.DMA((2,)),
            pltpu.SemaphoreType.DMA((2,)),
            pltpu.SemaphoreType.DMA((4,)),
        ],
    )
    return pl.pallas_call(
        _moe_kernel,
        grid_spec=grid_spec,
        out_shape=jax.ShapeDtypeStruct((OUT_ROWS * ROW_TILE, LANES), _F32),
        compiler_params=pltpu.CompilerParams(
            dimension_semantics=("arbitrary",), vmem_limit_bytes=VMEM_LIMIT),
        name="moe_experts",
    )(plan, table, xs, wgu, bgu, wdn, bdn)


def _combine_kernel(x1_ref, g_ref, o0_ref, o1_ref, o2_ref, o3_ref, gfin_ref, yp_ref, ys_ref):
    i = pl.program_id(0)
    g = g_ref[...]
    o = [_load_row_tiles(r, COMB_TILE) for r in (o0_ref, o1_ref, o2_ref, o3_ref)]
    moe = (g[:, 0:1] * o[0] + g[:, 1:2] * o[1]) + (g[:, 2:3] * o[2] + g[:, 3:4] * o[3])
    y = _rms(x1_ref[...] + moe, gfin_ref[...])

    @pl.when(i < COMB_PROMPT_STEPS)
    def _():
        yp_ref[...] = y

    @pl.when(i == COMB_PROMPT_STEPS)
    def _():
        ys_ref[...] = y[:DEC_BATCH, :]


def _combine(x1, gates_tk, out4, g_final):
    slab_blocks = T_PAD // COMB_TILE

    def slab_spec(k):
        return pl.BlockSpec((COMB_TILE * ROW_TILE, LANES), lambda i, k=k: (k * slab_blocks + i, 0))

    return pl.pallas_call(
        _combine_kernel,
        grid=(COMB_PROMPT_STEPS + 1,),
        in_specs=[pl.BlockSpec((COMB_TILE, D_MODEL), lambda i: (i, 0)),
                  pl.BlockSpec((COMB_TILE, TOP_K), lambda i: (i, 0)),
                  slab_spec(0), slab_spec(1), slab_spec(2), slab_spec(3),
                  _const_spec(g_final.shape)],
        out_specs=[pl.BlockSpec((COMB_TILE, D_MODEL), lambda i: (jnp.minimum(i, COMB_PROMPT_STEPS - 1), 0)),
                   _const_spec((DEC_BATCH, D_MODEL))],
        out_shape=[jax.ShapeDtypeStruct((T_PROMPT, D_MODEL), _F32),
                   jax.ShapeDtypeStruct((DEC_BATCH, D_MODEL), _F32)],
        compiler_params=pltpu.CompilerParams(
            dimension_semantics=("arbitrary",), vmem_limit_bytes=VMEM_LIMIT),
        name="combine",
    )(x1, gates_tk, out4, out4, out4, out4, g_final)


def kernel(x_prompt, x_sample, state_pool, state_conv, meta_tokens, g_mix, w_in, w_pool_grp, pool_scale,
           w_pool_up, w_conv, w_conv_out, w_o, g_ffn, w_router, b_router, w_gu, b_gu, w_down, b_down, g_final):
    assert g_mix.shape[0] == 1, "single-layer step"
    grp = w_pool_grp[0].astype(_BF16)
    zero = jnp.zeros((POOL_GROUP, POOL_GROUP), _BF16)
    wgrp = jnp.stack([jnp.block([[grp[0], zero], [zero, grp[1]]]),
                      jnp.block([[grp[2], zero], [zero, grp[3]]])])
    w = (
        g_mix[0].reshape(1, D_MODEL),
        w_in[0].astype(_BF16),
        wgrp,
        pool_scale[0].reshape(1, D_POOL),
        w_pool_up[0].astype(_BF16),
        w_conv[0],
        w_conv_out[0].astype(_BF16),
        w_o[0].astype(_BF16),
        g_ffn[0].reshape(1, D_MODEL),
        w_router[0].T,
        b_router[0].reshape(N_EXPERTS, 1),
    )
    xs_pad = jnp.pad(x_sample.reshape(DEC_BATCH, D_MODEL), ((0, MIX_TILE - DEC_BATCH), (0, 0)))
    spool_t = jnp.transpose(state_pool[0], (1, 0, 2))
    sconv_t = jnp.transpose(state_conv[0], (1, 0, 2))
    x1, h2, idx, gate, rank, cnt, pool_p, conv_p, u_s, cv_s = _mixer(
        x_prompt, xs_pad, spool_t, sconv_t, meta_tokens, w)

    plan, disp, pads = _routing_tables(idx, rank, cnt)
    xs, inv = _dispatch(disp, pads, h2)
    warmup = _spare_row(jnp.arange(2 * MOE_BLK, dtype=jnp.int32)).reshape(2, MOE_BLK) * ROW_TILE
    dst_rows = jnp.concatenate([warmup, inv.reshape(N_BLOCKS, MOE_BLK)], axis=0)
    out4 = _moe_experts(plan, dst_rows, xs,
                        w_gu.reshape(N_EXPERTS, D_MODEL, 2 * D_FF), b_gu.reshape(N_EXPERTS, 1, 2 * D_FF),
                        w_down.reshape(N_EXPERTS, D_FF, D_MODEL), b_down.reshape(N_EXPERTS, 1, D_MODEL))

    y_p, y_s = _combine(x1, gate.T, out4, g_final.reshape(1, D_MODEL))

    new_pool_p = pool_p[:, POOL_HALO - POOL_BUF:, :][None]
    new_conv_p = conv_p[:, CONV_HALO - CONV_BUF:, :][None]
    new_pool_s = jnp.concatenate([state_pool[0][:, 1:, :], u_s[:, None, :]], axis=1)[None]
    new_conv_s = jnp.concatenate([state_conv[0][:, 1:, :], cv_s[:, None, :]], axis=1)[None]
    return (y_p.reshape(BATCH, SEQ, D_MODEL), y_s.reshape(DEC_BATCH, 1, D_MODEL),
            new_pool_p, new_conv_p, new_pool_s, new_conv_s)
```

```python
import jax
import jax.numpy as jnp
from jax import lax
from jax.experimental import pallas as pl
from jax.experimental.pallas import tpu as pltpu

D_MODEL = 1024
BATCH = 8
SEQ = 2048
DEC_BATCH = 128
N_META = 16
D_POOL = 512
POOL_GROUP = 128
POOL_WINDOWS = (2, 4, 8, 16)
POOL_BUF = 15
D_CONV = 512
CONV_BUF = 2
N_EXPERTS = 32
TOP_K = 4
D_FF = 1024
SWIGLU_LIMIT = 7.0
SWIGLU_ALPHA = 1.702
RMS_EPS = 1e-5

O_CG = D_POOL
O_BG = O_CG + D_CONV
O_V = O_BG + D_CONV
O_GL = O_V + D_CONV

T_PROMPT = BATCH * SEQ
T_ALL = T_PROMPT + DEC_BATCH
N_ASSIGN = TOP_K * T_ALL

MIX_TILE = 512
MIX_PROMPT_STEPS = T_PROMPT // MIX_TILE
SEQ_TILES = SEQ // MIX_TILE
POOL_HALO = 16
CONV_HALO = 8
DISP_TILE = 256
DISP_STEPS = -(-T_ALL // DISP_TILE)
DISP_LAST_ROWS = T_ALL - (DISP_STEPS - 1) * DISP_TILE
MOE_BLK = 256
N_BLOCKS = -(-N_ASSIGN // MOE_BLK) + N_EXPERTS
N_PAD_ROWS = N_BLOCKS * MOE_BLK - N_ASSIGN
PAD_CHUNK = 64
assert N_PAD_ROWS % PAD_CHUNK == 0
ROW_DMA_PRIORITY = 0
WEIGHT_DMA_PRIORITY = 1
COMB_TILE = 512
COMB_PROMPT_STEPS = T_PROMPT // COMB_TILE
T_PAD = (T_ALL + COMB_TILE - 1) // COMB_TILE * COMB_TILE
SLAB_GAP = T_PAD - T_ALL
OUT_ROWS = (N_BLOCKS + 2) * MOE_BLK
TAIL0 = TOP_K * T_PAD
assert OUT_ROWS >= TAIL0

ROW_TILE = 8
ROW_TILE_SHIFT = 3
LANES = D_MODEL // ROW_TILE
assert 1 << ROW_TILE_SHIFT == ROW_TILE

VMEM_LIMIT = 56 * 1024 * 1024

_F32 = jnp.float32
_BF16 = jnp.bfloat16


def _rms(x, g):
    ms = jnp.mean(x * x, axis=-1, keepdims=True)
    return x * lax.rsqrt(ms + RMS_EPS) * g


def _bdot(a, w):
    return jnp.dot(a.astype(_BF16), w, preferred_element_type=_F32)


def _store_row_tiles(ref, val):
    rows = val.shape[0]
    for c in range(ROW_TILE):
        ref[pl.ds(c, rows, stride=ROW_TILE), :] = val[:, c * LANES:(c + 1) * LANES]


def _load_row_tiles(ref, rows):
    return jnp.concatenate([ref[pl.ds(c, rows, stride=ROW_TILE), :] for c in range(ROW_TILE)], axis=-1)


def _branches(z, pm, conv, wgrp_ref, pscale_ref, wup_ref, wcout_ref):
    pmb = pm.astype(_BF16)
    half = 2 * POOL_GROUP
    pg = jnp.concatenate(
        [jnp.dot(pmb[:, :half], wgrp_ref[0], preferred_element_type=_F32),
         jnp.dot(pmb[:, half:], wgrp_ref[1], preferred_element_type=_F32)], axis=-1)
    branch_a = _bdot(pg * pscale_ref[...], wup_ref[...])
    branch_b = _bdot(z[:, O_BG:O_V] * conv, wcout_ref[...])
    return branch_a, branch_b


def _merge_and_route(x, z, branch_a, branch_b, wo_ref, gffn_ref, wrt_ref, br_ref, base_ref, n_live):
    gates = jax.nn.sigmoid(z[:, O_GL:])
    merged = gates[:, :D_MODEL] * branch_a + gates[:, D_MODEL:] * branch_b
    x1 = x + _bdot(merged, wo_ref[...])
    h2 = _rms(x1, gffn_ref[...])
    logits = lax.dot_general(wrt_ref[...], h2, (((1,), (1,)), ((), ())),
                             precision=lax.Precision.HIGHEST,
                             preferred_element_type=_F32) + br_ref[...]
    iota = lax.broadcasted_iota(jnp.int32, logits.shape, 0)
    vals, idxs = [], []
    cur = logits
    for _ in range(TOP_K):
        m = jnp.max(cur, axis=0, keepdims=True)
        ik = jnp.min(jnp.where(cur == m, iota, N_EXPERTS), axis=0, keepdims=True)
        vals.append(m)
        idxs.append(ik)
        cur = jnp.where(iota == ik, -jnp.inf, cur)
    exps = [jnp.exp(v - vals[0]) for v in vals]
    denom = (exps[0] + exps[1]) + (exps[2] + exps[3])
    gate = [e / denom for e in exps]
    rows = logits.shape[1]
    live = lax.broadcasted_iota(jnp.int32, logits.shape, 1) < n_live
    onehots = [jnp.where(live, jnp.where(iota == ik, 1.0, 0.0), 0.0) for ik in idxs]
    member = (onehots[0] + onehots[1]) + (onehots[2] + onehots[3])
    earlier = (lax.broadcasted_iota(jnp.int32, (rows, rows), 0)
               < lax.broadcasted_iota(jnp.int32, (rows, rows), 1))
    before = jnp.dot(member.astype(_BF16), jnp.where(earlier, 1.0, 0.0).astype(_BF16),
                     preferred_element_type=_F32)
    pos = base_ref[...] + before
    ranks = [jnp.sum(oh * pos, axis=0, keepdims=True).astype(jnp.int32) for oh in onehots]
    base_ref[...] = base_ref[...] + jnp.sum(member, axis=1, keepdims=True)
    return x1, h2, idxs, gate, ranks


def _mixer_kernel(xp_ref, xs_ref, spool_ref, sconv_ref, meta_ref, gmix_ref, win_ref, wgrp_ref, pscale_ref,
                  wup_ref, wconv_ref, wcout_ref, wo_ref, gffn_ref, wrt_ref, br_ref,
                  x1_ref, h2_ref, idx_ref, gate_ref, rank_ref, cnt_ref, pool_out_ref, conv_out_ref, us_ref, cvs_ref,
                  pool_ext, conv_ext, meta_pool, meta_conv, pm_scr, conv_scr, base_scr):
    s = pl.program_id(0)
    is_prompt = s < MIX_PROMPT_STEPS
    j = lax.rem(s, SEQ_TILES)

    @pl.when(s == 0)
    def _():
        hm = _rms(meta_ref[...], gmix_ref[...]).astype(_BF16)
        zm = jnp.dot(hm, win_ref[:, :O_GL], preferred_element_type=_F32)
        meta_pool[...] = zm[:, :D_POOL]
        meta_conv[...] = zm[:, O_CG:O_BG] * zm[:, O_V:O_GL]
        base_scr[...] = jnp.zeros_like(base_scr)

    @pl.when(is_prompt & (j == 0))
    def _():
        pool_ext[0:POOL_HALO, :] = meta_pool[...]
        conv_ext[0:CONV_HALO, :] = meta_conv[N_META - CONV_HALO:, :]

    x = jnp.where(is_prompt, xp_ref[...], xs_ref[...])
    h = _rms(x, gmix_ref[...]).astype(_BF16)
    z = jnp.dot(h, win_ref[...], preferred_element_type=_F32)
    u = z[:, :D_POOL]
    cv = z[:, O_CG:O_BG] * z[:, O_V:O_GL]
    wc = wconv_ref[...]

    @pl.when(is_prompt)
    def _():
        pool_ext[POOL_HALO:, :] = u
        ext = pool_ext[...]
        for g, w in enumerate(POOL_WINDOWS):
            lanes = slice(g * POOL_GROUP, (g + 1) * POOL_GROUP)
            acc = ext[:, lanes]
            sh = 1
            while sh < w:
                acc = acc + pltpu.roll(acc, sh, 0)
                sh *= 2
            pm_scr[:, lanes] = acc[POOL_HALO:, :] * (1.0 / w) - u[:, lanes]
        conv_ext[CONV_HALO:, :] = cv
        cext = conv_ext[...]
        conv_scr[...] = (wc[0:1, :] * pltpu.roll(cext, 2, 0)[CONV_HALO:, :]
                         + wc[1:2, :] * pltpu.roll(cext, 1, 0)[CONV_HALO:, :]) + wc[2:3, :] * cv
        pool_ext[0:POOL_HALO, :] = u[MIX_TILE - POOL_HALO:, :]
        conv_ext[0:CONV_HALO, :] = cv[MIX_TILE - CONV_HALO:, :]

    @pl.when(is_prompt & (j == SEQ_TILES - 1))
    def _():
        pool_out_ref[...] = u[MIX_TILE - POOL_HALO:, :]
        conv_out_ref[...] = cv[MIX_TILE - CONV_HALO:, :]

    @pl.when(jnp.logical_not(is_prompt))
    def _():
        us = u[:DEC_BATCH, :]
        cvs = cv[:DEC_BATCH, :]
        pm_scr[...] = jnp.zeros_like(pm_scr)
        conv_scr[...] = jnp.zeros_like(conv_scr)
        for g, w in enumerate(POOL_WINDOWS):
            lanes = slice(g * POOL_GROUP, (g + 1) * POOL_GROUP)
            acc = us[:, lanes]
            for t in range(POOL_BUF - (w - 1), POOL_BUF):
                acc = acc + spool_ref[t, :, lanes]
            pm_scr[0:DEC_BATCH, lanes] = acc * (1.0 / w) - us[:, lanes]
        conv_scr[0:DEC_BATCH, :] = (wc[0:1, :] * sconv_ref[0] + wc[1:2, :] * sconv_ref[1]) + wc[2:3, :] * cvs
        us_ref[...] = us
        cvs_ref[...] = cvs

    branch_a, branch_b = _branches(z, pm_scr[...], conv_scr[...], wgrp_ref, pscale_ref, wup_ref, wcout_ref)
    n_live = jnp.where(is_prompt, MIX_TILE, DEC_BATCH)
    x1, h2, idxs, gate, ranks = _merge_and_route(x, z, branch_a, branch_b, wo_ref, gffn_ref, wrt_ref, br_ref,
                                                 base_scr, n_live)
    x1_ref[...] = x1
    _store_row_tiles(h2_ref, h2)
    for k in range(TOP_K):
        idx_ref[k:k + 1, :] = idxs[k]
        gate_ref[k:k + 1, :] = gate[k]
        rank_ref[k:k + 1, :] = ranks[k]

    @pl.when(jnp.logical_not(is_prompt))
    def _():
        cnt_ref[...] = jnp.broadcast_to(base_scr[...], cnt_ref.shape)


def _const_spec(shape):
    nd = len(shape)
    return pl.BlockSpec(shape, lambda *_: (0,) * nd)


def _resident_spec(shape):
    nd = len(shape)
    return pl.BlockSpec(shape, lambda *_: (0,) * nd, pipeline_mode=pl.Buffered(1))


def _mixer(x_prompt, xs_pad, spool_t, sconv_t, meta, w):
    last = MIX_PROMPT_STEPS - 1

    def x_map(s):
        sp = jnp.minimum(s, last)
        return (sp // SEQ_TILES, sp % SEQ_TILES, 0)

    def seq_map(s):
        return (jnp.minimum(s, last) // SEQ_TILES, 0, 0)

    consts = (xs_pad, spool_t, sconv_t, meta) + tuple(w)
    return pl.pallas_call(
        _mixer_kernel,
        grid=(MIX_PROMPT_STEPS + 1,),
        in_specs=[pl.BlockSpec((None, MIX_TILE, D_MODEL), x_map)] + [_resident_spec(a.shape) for a in consts],
        out_specs=[
            pl.BlockSpec((MIX_TILE, D_MODEL), lambda s: (s, 0)),
            pl.BlockSpec((MIX_TILE * ROW_TILE, LANES), lambda s: (s, 0)),
            pl.BlockSpec((TOP_K, MIX_TILE), lambda s: (0, s)),
            pl.BlockSpec((TOP_K, MIX_TILE), lambda s: (0, s)),
            pl.BlockSpec((TOP_K, MIX_TILE), lambda s: (0, s)),
            _const_spec((N_EXPERTS, 128)),
            pl.BlockSpec((None, POOL_HALO, D_POOL), seq_map),
            pl.BlockSpec((None, CONV_HALO, D_CONV), seq_map),
            _const_spec((DEC_BATCH, D_POOL)),
            _const_spec((DEC_BATCH, D_CONV)),
        ],
        out_shape=[
            jax.ShapeDtypeStruct((T_ALL, D_MODEL), _F32),
            jax.ShapeDtypeStruct((T_ALL * ROW_TILE, LANES), _F32),
            jax.ShapeDtypeStruct((TOP_K, T_ALL), jnp.int32),
            jax.ShapeDtypeStruct((TOP_K, T_ALL), _F32),
            jax.ShapeDtypeStruct((TOP_K, T_ALL), jnp.int32),
            jax.ShapeDtypeStruct((N_EXPERTS, 128), _F32),
            jax.ShapeDtypeStruct((BATCH, POOL_HALO, D_POOL), _F32),
            jax.ShapeDtypeStruct((BATCH, CONV_HALO, D_CONV), _F32),
            jax.ShapeDtypeStruct((DEC_BATCH, D_POOL), _F32),
            jax.ShapeDtypeStruct((DEC_BATCH, D_CONV), _F32),
        ],
        scratch_shapes=[
            pltpu.VMEM((POOL_HALO + MIX_TILE, D_POOL), _F32),
            pltpu.VMEM((CONV_HALO + MIX_TILE, D_CONV), _F32),
            pltpu.VMEM((N_META, D_POOL), _F32),
            pltpu.VMEM((N_META, D_CONV), _F32),
            pltpu.VMEM((MIX_TILE, D_POOL), _F32),
            pltpu.VMEM((MIX_TILE, D_CONV), _F32),
            pltpu.VMEM((N_EXPERTS, 1), _F32),
        ],
        compiler_params=pltpu.CompilerParams(
            dimension_semantics=("arbitrary",), vmem_limit_bytes=VMEM_LIMIT),
        name="mixer",
    )(x_prompt, *consts)


def _spare_row(q):
    in_gap = q < TOP_K * SLAB_GAP
    gap_row = (q // SLAB_GAP) * T_PAD + T_ALL + q % SLAB_GAP
    return jnp.where(in_gap, gap_row, TAIL0 + q - TOP_K * SLAB_GAP).astype(jnp.int32)


def _routing_tables(idx, rank, cnt):
    def lookup(table, keys):
        ids = jnp.arange(table.shape[0], dtype=jnp.int32)
        return jnp.sum(jnp.where(keys[..., None] == ids, table, 0), axis=-1)

    counts = cnt[:, 0].astype(jnp.int32)
    padded = (counts + MOE_BLK - 1) // MOE_BLK * MOE_BLK
    pend = jnp.cumsum(padded)
    pstart = pend - padded

    dest = lookup(pstart, idx) + rank
    dest = jnp.pad(dest, ((0, 0), (0, DISP_STEPS * DISP_TILE - T_ALL)))
    disp = dest.reshape(TOP_K, DISP_STEPS, DISP_TILE).transpose(1, 0, 2).reshape(DISP_STEPS, TOP_K * DISP_TILE)

    n_pad = padded - counts
    pad_cum = jnp.cumsum(n_pad)
    q = jnp.arange(N_PAD_ROWS, dtype=jnp.int32)
    e_q = jnp.sum((q[:, None] >= pad_cum[None, :]).astype(jnp.int32), axis=1)
    first_pad = lookup(jnp.concatenate([pstart + counts, pend[-1:]]), e_q)
    pads_before = lookup(jnp.concatenate([pad_cum - n_pad, pad_cum[-1:]]), e_q)
    pad_rows = first_pad + q - pads_before

    pads = jnp.concatenate([pad_rows, _spare_row(2 * MOE_BLK + q)]) * ROW_TILE

    block_start = jnp.arange(N_BLOCKS, dtype=jnp.int32) * MOE_BLK
    block_exp = jnp.minimum(jnp.sum((block_start[:, None] >= pend[None, :]).astype(jnp.int32), axis=1),
                            N_EXPERTS - 1)
    blocks = jnp.arange(N_BLOCKS, dtype=jnp.int32)
    first = jnp.concatenate([jnp.ones((1,), jnp.bool_), block_exp[1:] != block_exp[:-1]])
    run_slot = (jnp.cumsum(first.astype(jnp.int32)) - 1) % 2
    later_first = first[None, :] & (blocks[None, :] > blocks[:, None])
    next_first = jnp.min(jnp.where(later_first, blocks[None, :], N_BLOCKS), axis=1)
    has_next = next_first < N_BLOCKS
    next_exp = lookup(block_exp, jnp.minimum(next_first, N_BLOCKS - 1))
    n_used = jnp.broadcast_to(pend[-1] // MOE_BLK, (N_BLOCKS,))
    plan = jnp.stack([block_exp, first.astype(jnp.int32), run_slot, next_exp, has_next.astype(jnp.int32),
                      n_used, jnp.zeros_like(n_used)]).astype(jnp.int32)
    return plan, disp * ROW_TILE, pads


def _dispatch_kernel(disp_hbm, pads_hbm, h2_ref, xs_hbm, inv_hbm, tbl0, tbl1, pad_tbl, inv_tbl, zero_tile,
                     rsem, tsem, psem):
    s = pl.program_id(0)
    ns = pl.num_programs(0)
    tbls = (tbl0, tbl1)

    def set_inverse(in_off, out_off):
        inv_tbl[lax.shift_right_logical(in_off, ROW_TILE_SHIFT)] = out_off

    def table_copy(row, p):
        return pltpu.make_async_copy(disp_hbm.at[row], tbls[p], tsem.at[p])

    def row_copy(off, r, k):
        return pltpu.make_async_copy(h2_ref.at[pl.ds(r * ROW_TILE, ROW_TILE), :],
                                     xs_hbm.at[pl.ds(off, ROW_TILE), :], rsem.at[k])

    def pad_copy(off):
        return pltpu.make_async_copy(zero_tile, xs_hbm.at[pl.ds(off, ROW_TILE), :], psem.at[0])

    def scatter_rows(p, n_rows):
        first_out = s * (DISP_TILE * ROW_TILE)
        for r in range(n_rows):
            for k in range(TOP_K):
                off = pl.multiple_of(tbls[p][k * DISP_TILE + r], ROW_TILE)
                row_copy(off, r, k).start(priority=(r + k) % 2)
        for r in range(n_rows):
            for k in range(TOP_K):
                set_inverse(tbls[p][k * DISP_TILE + r], first_out + (k * T_PAD + r) * ROW_TILE)
        for r in range(n_rows):
            for k in range(TOP_K):
                row_copy(0, r, k).wait()

    def fill_pads():
        zero_tile[...] = jnp.zeros_like(zero_tile)
        load = pltpu.make_async_copy(pads_hbm, pad_tbl, psem.at[1])
        load.start()
        load.wait()

        def chunk(i, carry):
            for j in range(PAD_CHUNK):
                off = pl.multiple_of(pad_tbl[i * PAD_CHUNK + j], ROW_TILE)
                pad_copy(off).start(priority=j % 2)
                set_inverse(off, pad_tbl[N_PAD_ROWS + i * PAD_CHUNK + j])

            @pl.when(i > 0)
            def _():
                for j in range(PAD_CHUNK):
                    pad_copy(0).wait()
            return carry

        lax.fori_loop(0, N_PAD_ROWS // PAD_CHUNK, chunk, 0)
        for j in range(PAD_CHUNK):
            pad_copy(0).wait()
        done = pltpu.make_async_copy(inv_tbl, inv_hbm, psem.at[1])
        done.start()
        done.wait()

    @pl.when(s == 0)
    def _():
        table_copy(0, 0).start()

    def step(p):
        table_copy(s, p).wait()

        @pl.when(s < ns - 1)
        def _():
            table_copy(s + 1, 1 - p).start()
            scatter_rows(p, DISP_TILE)

        @pl.when(s == ns - 1)
        def _():
            scatter_rows(p, DISP_LAST_ROWS)
            fill_pads()

    parity = lax.rem(s, 2)
    pl.when(parity == 0)(lambda: step(0))
    pl.when(parity == 1)(lambda: step(1))


def _dispatch(disp, pads, h2):
    return pl.pallas_call(
        _dispatch_kernel,
        grid=(DISP_STEPS,),
        in_specs=[pl.BlockSpec(memory_space=pl.ANY),
                  pl.BlockSpec(memory_space=pl.ANY),
                  pl.BlockSpec((DISP_TILE * ROW_TILE, LANES), lambda s: (s, 0))],
        out_specs=[pl.BlockSpec(memory_space=pl.ANY), pl.BlockSpec(memory_space=pl.ANY)],
        out_shape=[jax.ShapeDtypeStruct((N_BLOCKS * MOE_BLK * ROW_TILE, LANES), _F32),
                   jax.ShapeDtypeStruct((N_BLOCKS * MOE_BLK,), jnp.int32)],
        scratch_shapes=[
            pltpu.SMEM((TOP_K * DISP_TILE,), jnp.int32),
            pltpu.SMEM((TOP_K * DISP_TILE,), jnp.int32),
            pltpu.SMEM((2 * N_PAD_ROWS,), jnp.int32),
            pltpu.SMEM((N_BLOCKS * MOE_BLK,), jnp.int32),
            pltpu.VMEM((ROW_TILE, LANES), _F32),
            pltpu.SemaphoreType.DMA((TOP_K,)),
            pltpu.SemaphoreType.DMA((2,)),
            pltpu.SemaphoreType.DMA((2,)),
        ],
        compiler_params=pltpu.CompilerParams(
            dimension_semantics=("arbitrary",), vmem_limit_bytes=VMEM_LIMIT),
        name="dispatch",
    )(disp, pads, h2)


PLAN_EXPERT, PLAN_FIRST, PLAN_SLOT, PLAN_NEXT, PLAN_HAS_NEXT, PLAN_USED, PLAN_ZERO = range(7)
SCATTER_GROUPS = 8


def _moe_kernel(plan_ref, table_hbm, x_ref, wgu_hbm, bgu_ref, wdn_hbm, bdn_ref, out_hbm,
                ybuf0, ybuf1, tbl0, tbl1, gu_stage0, gu_stage1, dn_stage0, dn_stage1, wgu_bf, wdn_bf,
                ssem, tsem, wsem):
    b = pl.program_id(0)
    nb = pl.num_programs(0)
    ybufs, tbls = (ybuf0, ybuf1), (tbl0, tbl1)
    gu_stages, dn_stages = (gu_stage0, gu_stage1), (dn_stage0, dn_stage1)
    n_used = plan_ref[PLAN_USED, 0]

    def table_copy(row, p):
        return pltpu.make_async_copy(table_hbm.at[row], tbls[p], tsem.at[p])

    def weight_copies(e, w):
        return (pltpu.make_async_copy(wgu_hbm.at[e], gu_stages[w], wsem.at[2 * w]),
                pltpu.make_async_copy(wdn_hbm.at[e], dn_stages[w], wsem.at[2 * w + 1]))

    def scatter_row(off, r, p):
        return pltpu.make_async_copy(ybufs[p].at[pl.ds(r * ROW_TILE, ROW_TILE), :],
                                     out_hbm.at[pl.ds(off, ROW_TILE), :], ssem.at[p])

    def start_scatter(tp, p, lo=0, hi=MOE_BLK, after=0):
        for r in range(lo, hi):
            off = pl.multiple_of(tbls[tp][r] + after, ROW_TILE)
            scatter_row(off, r, p).start(priority=ROW_DMA_PRIORITY)

    def zero_after(value):
        return lax.bitcast_convert_type(value, jnp.int32) * plan_ref[PLAN_ZERO, b]

    def wait_scatter(p):
        for r in range(MOE_BLK):
            scatter_row(0, r, p).wait()

    @pl.when(b == 0)
    def _():
        first = table_copy(0, 0)
        first.start()
        first.wait()
        ybuf0[...] = jnp.zeros_like(ybuf0)
        ybuf1[...] = jnp.zeros_like(ybuf1)
        start_scatter(0, 0)
        table_copy(1, 1).start()
        for c in weight_copies(plan_ref[PLAN_EXPERT, 0], 0):
            c.start(priority=WEIGHT_DMA_PRIORITY)

    def new_expert(w):
        for c in weight_copies(0, w):
            c.wait()

        @pl.when(plan_ref[PLAN_HAS_NEXT, b] == 1)
        def _():
            for c in weight_copies(plan_ref[PLAN_NEXT, b], 1 - w):
                c.start(priority=WEIGHT_DMA_PRIORITY)

        wgu_bf[...] = gu_stages[w][...].astype(_BF16)
        wdn_bf[...] = dn_stages[w][...].astype(_BF16)

    is_first = plan_ref[PLAN_FIRST, b] == 1
    wslot = plan_ref[PLAN_SLOT, b]
    pl.when(is_first & (wslot == 0))(lambda: new_expert(0))
    pl.when(is_first & (wslot == 1))(lambda: new_expert(1))

    def step(p):
        q = 1 - p
        table_copy(b + 1, q).wait()

        @pl.when(b < n_used)
        def _():
            group = MOE_BLK // SCATTER_GROUPS
            start_scatter(q, q, 0, group)
            table_copy(b + 2, p).start()
            x = _load_row_tiles(x_ref, MOE_BLK).astype(_BF16)
            hgu = jnp.dot(x, wgu_bf[...], preferred_element_type=_F32) + bgu_ref[...]
            for g in range(1, SCATTER_GROUPS):
                col = (2 * D_FF * g) // SCATTER_GROUPS - 1
                start_scatter(q, q, g * group, (g + 1) * group, zero_after(hgu[0, col]))
            gate = jnp.minimum(hgu[:, :D_FF], SWIGLU_LIMIT)
            up = jnp.clip(hgu[:, D_FF:], -SWIGLU_LIMIT, SWIGLU_LIMIT)
            act = gate * jax.nn.sigmoid(SWIGLU_ALPHA * gate) * (up + 1.0)
            y = jnp.dot(act.astype(_BF16), wdn_bf[...], preferred_element_type=_F32) + bdn_ref[...]
            wait_scatter(p)
            _store_row_tiles(ybufs[p], y)

        @pl.when(b >= n_used)
        def _():
            start_scatter(q, q)
            table_copy(b + 2, p).start()
            wait_scatter(p)
            ybufs[p][...] = jnp.zeros_like(ybufs[p])

        @pl.when(b == nb - 1)
        def _():
            table_copy(b + 2, p).wait()
            start_scatter(p, p)
            wait_scatter(p)
            wait_scatter(q)

    parity = lax.rem(b, 2)
    pl.when(parity == 0)(lambda: step(0))
    pl.when(parity == 1)(lambda: step(1))


def _moe_experts(plan, table, xs, wgu, bgu, wdn, bdn):
    grid_spec = pltpu.PrefetchScalarGridSpec(
        num_scalar_prefetch=1,
        grid=(N_BLOCKS,),
        in_specs=[
            pl.BlockSpec(memory_space=pl.ANY),
            pl.BlockSpec((MOE_BLK * ROW_TILE, LANES), lambda b, plan: (b, 0)),
            pl.BlockSpec(memory_space=pl.ANY),
            pl.BlockSpec((None, 1, 2 * D_FF), lambda b, plan: (plan[PLAN_EXPERT, b], 0, 0)),
            pl.BlockSpec(memory_space=pl.ANY),
            pl.BlockSpec((None, 1, D_MODEL), lambda b, plan: (plan[PLAN_EXPERT, b], 0, 0)),
        ],
        out_specs=pl.BlockSpec(memory_space=pl.ANY),
        scratch_shapes=[
            pltpu.VMEM((MOE_BLK * ROW_TILE, LANES), _F32),
            pltpu.VMEM((MOE_BLK * ROW_TILE, LANES), _F32),
            pltpu.SMEM((MOE_BLK,), jnp.int32),
            pltpu.SMEM((MOE_BLK,), jnp.int32),
            pltpu.VMEM((D_MODEL, 2 * D_FF), _F32),
            pltpu.VMEM((D_MODEL, 2 * D_FF), _F32),
            pltpu.VMEM((D_FF, D_MODEL), _F32),
            pltpu.VMEM((D_FF, D_MODEL), _F32),
            pltpu.VMEM((D_MODEL, 2 * D_FF), _BF16),
            pltpu.VMEM((D_FF, D_MODEL), _BF16),
            pltpu.SemaphoreType.DMA((2,)),
            pltpu.SemaphoreType.DMA((2,)),
            pltpu.SemaphoreType.DMA((4,)),
        ],
    )
    return pl.pallas_call(
        _moe_kernel,
        grid_spec=grid_spec,
        out_shape=jax.ShapeDtypeStruct((OUT_ROWS * ROW_TILE, LANES), _F32),
        compiler_params=pltpu.CompilerParams(
            dimension_semantics=("arbitrary",), vmem_limit_bytes=VMEM_LIMIT),
        name="moe_experts",
    )(plan, table, xs, wgu, bgu, wdn, bdn)


def _combine_kernel(x1_ref, g_ref, o0_ref, o1_ref, o2_ref, o3_ref, gfin_ref, yp_ref, ys_ref):
    i = pl.program_id(0)
    g = g_ref[...]
    o = [_load_row_tiles(r, COMB_TILE) for r in (o0_ref, o1_ref, o2_ref, o3_ref)]
    moe = (g[:, 0:1] * o[0] + g[:, 1:2] * o[1]) + (g[:, 2:3] * o[2] + g[:, 3:4] * o[3])
    y = _rms(x1_ref[...] + moe, gfin_ref[...])

    @pl.when(i < COMB_PROMPT_STEPS)
    def _():
        yp_ref[...] = y

    @pl.when(i == COMB_PROMPT_STEPS)
    def _():
        ys_ref[...] = y[:DEC_BATCH, :]


def _combine(x1, gates_tk, out4, g_final):
    slab_blocks = T_PAD // COMB_TILE

    def slab_spec(k):
        return pl.BlockSpec((COMB_TILE * ROW_TILE, LANES), lambda i, k=k: (k * slab_blocks + i, 0))

    return pl.pallas_call(
        _combine_kernel,
        grid=(COMB_PROMPT_STEPS + 1,),
        in_specs=[pl.BlockSpec((COMB_TILE, D_MODEL), lambda i: (i, 0)),
                  pl.BlockSpec((COMB_TILE, TOP_K), lambda i: (i, 0)),
                  slab_spec(0), slab_spec(1), slab_spec(2), slab_spec(3),
                  _const_spec(g_final.shape)],
        out_specs=[pl.BlockSpec((COMB_TILE, D_MODEL), lambda i: (jnp.minimum(i, COMB_PROMPT_STEPS - 1), 0)),
                   _const_spec((DEC_BATCH, D_MODEL))],
        out_shape=[jax.ShapeDtypeStruct((T_PROMPT, D_MODEL), _F32),
                   jax.ShapeDtypeStruct((DEC_BATCH, D_MODEL), _F32)],
        compiler_params=pltpu.CompilerParams(
            dimension_semantics=("arbitrary",), vmem_limit_bytes=VMEM_LIMIT),
        name="combine",
    )(x1, gates_tk, out4, out4, out4, out4, g_final)


def kernel(x_prompt, x_sample, state_pool, state_conv, meta_tokens, g_mix, w_in, w_pool_grp, pool_scale,
           w_pool_up, w_conv, w_conv_out, w_o, g_ffn, w_router, b_router, w_gu, b_gu, w_down, b_down, g_final):
    assert g_mix.shape[0] == 1, "single-layer step"
    grp = w_pool_grp[0].astype(_BF16)
    zero = jnp.zeros((POOL_GROUP, POOL_GROUP), _BF16)
    wgrp = jnp.stack([jnp.block([[grp[0], zero], [zero, grp[1]]]),
                      jnp.block([[grp[2], zero], [zero, grp[3]]])])
    w = (
        g_mix[0].reshape(1, D_MODEL),
        w_in[0].astype(_BF16),
        wgrp,
        pool_scale[0].reshape(1, D_POOL),
        w_pool_up[0].astype(_BF16),
        w_conv[0],
        w_conv_out[0].astype(_BF16),
        w_o[0].astype(_BF16),
        g_ffn[0].reshape(1, D_MODEL),
        w_router[0].T,
        b_router[0].reshape(N_EXPERTS, 1),
    )
    xs_pad = jnp.pad(x_sample.reshape(DEC_BATCH, D_MODEL), ((0, MIX_TILE - DEC_BATCH), (0, 0)))
    spool_t = jnp.transpose(state_pool[0], (1, 0, 2))
    sconv_t = jnp.transpose(state_conv[0], (1, 0, 2))
    x1, h2, idx, gate, rank, cnt, pool_p, conv_p, u_s, cv_s = _mixer(
        x_prompt, xs_pad, spool_t, sconv_t, meta_tokens, w)

    plan, disp, pads = _routing_tables(idx, rank, cnt)
    xs, inv = _dispatch(disp, pads, h2)
    warmup = _spare_row(jnp.arange(2 * MOE_BLK, dtype=jnp.int32)).reshape(2, MOE_BLK) * ROW_TILE
    dst_rows = jnp.concatenate([warmup, inv.reshape(N_BLOCKS, MOE_BLK)], axis=0)
    out4 = _moe_experts(plan, dst_rows, xs,
                        w_gu.reshape(N_EXPERTS, D_MODEL, 2 * D_FF), b_gu.reshape(N_EXPERTS, 1, 2 * D_FF),
                        w_down.reshape(N_EXPERTS, D_FF, D_MODEL), b_down.reshape(N_EXPERTS, 1, D_MODEL))

    y_p, y_s = _combine(x1, gate.T, out4, g_final.reshape(1, D_MODEL))

    new_pool_p = pool_p[:, POOL_HALO - POOL_BUF:, :][None]
    new_conv_p = conv_p[:, CONV_HALO - CONV_BUF:, :][None]
    new_pool_s = jnp.concatenate([state_pool[0][:, 1:, :], u_s[:, None, :]], axis=1)[None]
    new_conv_s = jnp.concatenate([state_conv[0][:, 1:, :], cv_s[:, None, :]], axis=1)[None]
    return (y_p.reshape(BATCH, SEQ, D_MODEL), y_s.reshape(DEC_BATCH, 1, D_MODEL),
            new_pool_p, new_conv_p, new_pool_s, new_conv_s)
```

```python
import jax
import jax.numpy as jnp
from jax import lax
from jax.experimental import pallas as pl
from jax.experimental.pallas import tpu as pltpu

D_MODEL = 1024
BATCH = 8
SEQ = 2048
DEC_BATCH = 128
N_META = 16
D_POOL = 512
POOL_GROUP = 128
POOL_WINDOWS = (2, 4, 8, 16)
POOL_BUF = 15
D_CONV = 512
CONV_BUF = 2
N_EXPERTS = 32
TOP_K = 4
D_FF = 1024
SWIGLU_LIMIT = 7.0
SWIGLU_ALPHA = 1.702
RMS_EPS = 1e-5

O_CG = D_POOL
O_BG = O_CG + D_CONV
O_V = O_BG + D_CONV
O_GL = O_V + D_CONV

T_PROMPT = BATCH * SEQ
T_ALL = T_PROMPT + DEC_BATCH
N_ASSIGN = TOP_K * T_ALL

MIX_TILE = 512
MIX_PROMPT_STEPS = T_PROMPT // MIX_TILE
SEQ_TILES = SEQ // MIX_TILE
POOL_HALO = 16
CONV_HALO = 8
DISP_TILE = 256
DISP_STEPS = -(-T_ALL // DISP_TILE)
DISP_LAST_ROWS = T_ALL - (DISP_STEPS - 1) * DISP_TILE
MOE_BLK = 256
N_BLOCKS = -(-N_ASSIGN // MOE_BLK) + N_EXPERTS
N_PAD_ROWS = N_BLOCKS * MOE_BLK - N_ASSIGN
PAD_CHUNK = 64
assert N_PAD_ROWS % PAD_CHUNK == 0
ROW_DMA_PRIORITY = 0
WEIGHT_DMA_PRIORITY = 1
COMB_TILE = 512
COMB_PROMPT_STEPS = T_PROMPT // COMB_TILE
T_PAD = (T_ALL + COMB_TILE - 1) // COMB_TILE * COMB_TILE
SLAB_GAP = T_PAD - T_ALL
OUT_ROWS = (N_BLOCKS + 2) * MOE_BLK
TAIL0 = TOP_K * T_PAD
assert OUT_ROWS >= TAIL0

ROW_TILE = 8
ROW_TILE_SHIFT = 3
LANES = D_MODEL // ROW_TILE
assert 1 << ROW_TILE_SHIFT == ROW_TILE

VMEM_LIMIT = 56 * 1024 * 1024

_F32 = jnp.float32
_BF16 = jnp.bfloat16


def _rms(x, g):
    ms = jnp.mean(x * x, axis=-1, keepdims=True)
    return x * lax.rsqrt(ms + RMS_EPS) * g


def _bdot(a, w):
    return jnp.dot(a.astype(_BF16), w, preferred_element_type=_F32)


def _store_row_tiles(ref, val):
    rows = val.shape[0]
    for c in range(ROW_TILE):
        ref[pl.ds(c, rows, stride=ROW_TILE), :] = val[:, c * LANES:(c + 1) * LANES]


def _load_row_tiles(ref, rows):
    return jnp.concatenate([ref[pl.ds(c, rows, stride=ROW_TILE), :] for c in range(ROW_TILE)], axis=-1)


def _branches(z, pm, conv, wgrp_ref, pscale_ref, wup_ref, wcout_ref):
    pmb = pm.astype(_BF16)
    half = 2 * POOL_GROUP
    pg = jnp.concatenate(
        [jnp.dot(pmb[:, :half], wgrp_ref[0], preferred_element_type=_F32),
         jnp.dot(pmb[:, half:], wgrp_ref[1], preferred_element_type=_F32)], axis=-1)
    branch_a = _bdot(pg * pscale_ref[...], wup_ref[...])
    branch_b = _bdot(z[:, O_BG:O_V] * conv, wcout_ref[...])
    return branch_a, branch_b


def _merge(x, z, branch_a, branch_b, wo_ref, gffn_ref):
    gates = jax.nn.sigmoid(z[:, O_GL:])
    merged = gates[:, :D_MODEL] * branch_a + gates[:, D_MODEL:] * branch_b
    x1 = x + _bdot(merged, wo_ref[...])
    return x1, _rms(x1, gffn_ref[...])


def _route(h2, wrt_ref, br_ref, base_ref, n_live):
    logits = lax.dot_general(wrt_ref[...], h2, (((1,), (1,)), ((), ())),
                             precision=lax.Precision.HIGHEST,
                             preferred_element_type=_F32) + br_ref[...]
    iota = lax.broadcasted_iota(jnp.int32, logits.shape, 0)
    vals, idxs = [], []
    cur = logits
    for _ in range(TOP_K):
        m = jnp.max(cur, axis=0, keepdims=True)
        ik = jnp.min(jnp.where(cur == m, iota, N_EXPERTS), axis=0, keepdims=True)
        vals.append(m)
        idxs.append(ik)
        cur = jnp.where(iota == ik, -jnp.inf, cur)
    exps = [jnp.exp(v - vals[0]) for v in vals]
    denom = (exps[0] + exps[1]) + (exps[2] + exps[3])
    gate = [e / denom for e in exps]
    rows = logits.shape[1]
    live = lax.broadcasted_iota(jnp.int32, logits.shape, 1) < n_live
    onehots = [jnp.where(live, jnp.where(iota == ik, 1.0, 0.0), 0.0) for ik in idxs]
    member = (onehots[0] + onehots[1]) + (onehots[2] + onehots[3])
    earlier = (lax.broadcasted_iota(jnp.int32, (rows, rows), 0)
               < lax.broadcasted_iota(jnp.int32, (rows, rows), 1))
    before = jnp.dot(member.astype(_BF16), jnp.where(earlier, 1.0, 0.0).astype(_BF16),
                     preferred_element_type=_F32)
    pos = base_ref[...] + before
    ranks = [jnp.sum(oh * pos, axis=0, keepdims=True).astype(jnp.int32) for oh in onehots]
    base_ref[...] = base_ref[...] + jnp.sum(member, axis=1, keepdims=True)
    return idxs, gate, ranks


def _mixer_kernel(xp_ref, xs_ref, spool_ref, sconv_ref, meta_ref, gmix_ref, win_ref, wgrp_ref, pscale_ref,
                  wup_ref, wconv_ref, wcout_ref, wo_ref, gffn_ref, wrt_ref, br_ref,
                  x1_ref, h2_ref, idx_ref, gate_ref, rank_ref, idx_s_ref, gate_s_ref, rank_s_ref, cnt_ref,
                  pool_out_ref, conv_out_ref, us_ref, cvs_ref,
                  pool_ext, conv_ext, meta_pool, meta_conv, pm_scr, conv_scr, base_scr, h2_prev):
    s = pl.program_id(0)
    is_prompt = s < MIX_PROMPT_STEPS
    j = lax.rem(s, SEQ_TILES)

    @pl.when(s == 0)
    def _():
        hm = _rms(meta_ref[...], gmix_ref[...]).astype(_BF16)
        zm = jnp.dot(hm, win_ref[:, :O_GL], preferred_element_type=_F32)
        meta_pool[...] = zm[:, :D_POOL]
        meta_conv[...] = zm[:, O_CG:O_BG] * zm[:, O_V:O_GL]
        base_scr[...] = jnp.zeros_like(base_scr)
        h2_prev[...] = jnp.zeros_like(h2_prev)

    @pl.when(is_prompt & (j == 0))
    def _():
        pool_ext[0:POOL_HALO, :] = meta_pool[...]
        conv_ext[0:CONV_HALO, :] = meta_conv[N_META - CONV_HALO:, :]

    idxs, gate, ranks = _route(h2_prev[...], wrt_ref, br_ref, base_scr, jnp.where(s == 0, 0, MIX_TILE))
    for k in range(TOP_K):
        idx_ref[k:k + 1, :] = idxs[k]
        gate_ref[k:k + 1, :] = gate[k]
        rank_ref[k:k + 1, :] = ranks[k]

    x = jnp.where(is_prompt, xp_ref[...], xs_ref[...])
    h = _rms(x, gmix_ref[...]).astype(_BF16)
    z = jnp.dot(h, win_ref[...], preferred_element_type=_F32)
    u = z[:, :D_POOL]
    cv = z[:, O_CG:O_BG] * z[:, O_V:O_GL]
    wc = wconv_ref[...]

    @pl.when(is_prompt)
    def _():
        pool_ext[POOL_HALO:, :] = u
        ext = pool_ext[...]
        for g, w in enumerate(POOL_WINDOWS):
            lanes = slice(g * POOL_GROUP, (g + 1) * POOL_GROUP)
            acc = ext[:, lanes]
            sh = 1
            while sh < w:
                acc = acc + pltpu.roll(acc, sh, 0)
                sh *= 2
            pm_scr[:, lanes] = acc[POOL_HALO:, :] * (1.0 / w) - u[:, lanes]
        conv_ext[CONV_HALO:, :] = cv
        cext = conv_ext[...]
        conv_scr[...] = (wc[0:1, :] * pltpu.roll(cext, 2, 0)[CONV_HALO:, :]
                         + wc[1:2, :] * pltpu.roll(cext, 1, 0)[CONV_HALO:, :]) + wc[2:3, :] * cv
        pool_ext[0:POOL_HALO, :] = u[MIX_TILE - POOL_HALO:, :]
        conv_ext[0:CONV_HALO, :] = cv[MIX_TILE - CONV_HALO:, :]

    @pl.when(is_prompt & (j == SEQ_TILES - 1))
    def _():
        pool_out_ref[...] = u[MIX_TILE - POOL_HALO:, :]
        conv_out_ref[...] = cv[MIX_TILE - CONV_HALO:, :]

    @pl.when(jnp.logical_not(is_prompt))
    def _():
        us = u[:DEC_BATCH, :]
        cvs = cv[:DEC_BATCH, :]
        pm_scr[...] = jnp.zeros_like(pm_scr)
        conv_scr[...] = jnp.zeros_like(conv_scr)
        for g, w in enumerate(POOL_WINDOWS):
            lanes = slice(g * POOL_GROUP, (g + 1) * POOL_GROUP)
            acc = us[:, lanes]
            for t in range(POOL_BUF - (w - 1), POOL_BUF):
                acc = acc + spool_ref[t, :, lanes]
            pm_scr[0:DEC_BATCH, lanes] = acc * (1.0 / w) - us[:, lanes]
        conv_scr[0:DEC_BATCH, :] = (wc[0:1, :] * sconv_ref[0] + wc[1:2, :] * sconv_ref[1]) + wc[2:3, :] * cvs
        us_ref[...] = us
        cvs_ref[...] = cvs

    branch_a, branch_b = _branches(z, pm_scr[...], conv_scr[...], wgrp_ref, pscale_ref, wup_ref, wcout_ref)
    x1, h2 = _merge(x, z, branch_a, branch_b, wo_ref, gffn_ref)
    x1_ref[...] = x1
    _store_row_tiles(h2_ref, h2)
    h2_prev[...] = h2

    @pl.when(jnp.logical_not(is_prompt))
    def _():
        idxs, gate, ranks = _route(h2, wrt_ref, br_ref, base_scr, DEC_BATCH)
        for k in range(TOP_K):
            idx_s_ref[k:k + 1, :] = idxs[k]
            gate_s_ref[k:k + 1, :] = gate[k]
            rank_s_ref[k:k + 1, :] = ranks[k]
        cnt_ref[...] = jnp.broadcast_to(base_scr[...], cnt_ref.shape)


def _const_spec(shape):
    nd = len(shape)
    return pl.BlockSpec(shape, lambda *_: (0,) * nd)


def _resident_spec(shape):
    nd = len(shape)
    return pl.BlockSpec(shape, lambda *_: (0,) * nd, pipeline_mode=pl.Buffered(1))


def _mixer(x_prompt, xs_pad, spool_t, sconv_t, meta, w):
    last = MIX_PROMPT_STEPS - 1

    def x_map(s):
        sp = jnp.minimum(s, last)
        return (sp // SEQ_TILES, sp % SEQ_TILES, 0)

    def seq_map(s):
        return (jnp.minimum(s, last) // SEQ_TILES, 0, 0)

    def prev_map(s):
        return (0, jnp.maximum(s - 1, 0))

    consts = (xs_pad, spool_t, sconv_t, meta) + tuple(w)
    return pl.pallas_call(
        _mixer_kernel,
        grid=(MIX_PROMPT_STEPS + 1,),
        in_specs=[pl.BlockSpec((None, MIX_TILE, D_MODEL), x_map)] + [_resident_spec(a.shape) for a in consts],
        out_specs=[
            pl.BlockSpec((MIX_TILE, D_MODEL), lambda s: (s, 0)),
            pl.BlockSpec((MIX_TILE * ROW_TILE, LANES), lambda s: (s, 0)),
            pl.BlockSpec((TOP_K, MIX_TILE), prev_map),
            pl.BlockSpec((TOP_K, MIX_TILE), prev_map),
            pl.BlockSpec((TOP_K, MIX_TILE), prev_map),
            _const_spec((TOP_K, MIX_TILE)),
            _const_spec((TOP_K, MIX_TILE)),
            _const_spec((TOP_K, MIX_TILE)),
            _const_spec((N_EXPERTS, 128)),
            pl.BlockSpec((None, POOL_HALO, D_POOL), seq_map),
            pl.BlockSpec((None, CONV_HALO, D_CONV), seq_map),
            _const_spec((DEC_BATCH, D_POOL)),
            _const_spec((DEC_BATCH, D_CONV)),
        ],
        out_shape=[
            jax.ShapeDtypeStruct((T_ALL, D_MODEL), _F32),
            jax.ShapeDtypeStruct((T_ALL * ROW_TILE, LANES), _F32),
            jax.ShapeDtypeStruct((TOP_K, T_PROMPT), jnp.int32),
            jax.ShapeDtypeStruct((TOP_K, T_PROMPT), _F32),
            jax.ShapeDtypeStruct((TOP_K, T_PROMPT), jnp.int32),
            jax.ShapeDtypeStruct((TOP_K, MIX_TILE), jnp.int32),
            jax.ShapeDtypeStruct((TOP_K, MIX_TILE), _F32),
            jax.ShapeDtypeStruct((TOP_K, MIX_TILE), jnp.int32),
            jax.ShapeDtypeStruct((N_EXPERTS, 128), _F32),
            jax.ShapeDtypeStruct((BATCH, POOL_HALO, D_POOL), _F32),
            jax.ShapeDtypeStruct((BATCH, CONV_HALO, D_CONV), _F32),
            jax.ShapeDtypeStruct((DEC_BATCH, D_POOL), _F32),
            jax.ShapeDtypeStruct((DEC_BATCH, D_CONV), _F32),
        ],
        scratch_shapes=[
            pltpu.VMEM((POOL_HALO + MIX_TILE, D_POOL), _F32),
            pltpu.VMEM((CONV_HALO + MIX_TILE, D_CONV), _F32),
            pltpu.VMEM((N_META, D_POOL), _F32),
            pltpu.VMEM((N_META, D_CONV), _F32),
            pltpu.VMEM((MIX_TILE, D_POOL), _F32),
            pltpu.VMEM((MIX_TILE, D_CONV), _F32),
            pltpu.VMEM((N_EXPERTS, 1), _F32),
            pltpu.VMEM((MIX_TILE, D_MODEL), _F32),
        ],
        compiler_params=pltpu.CompilerParams(
            dimension_semantics=("arbitrary",), vmem_limit_bytes=VMEM_LIMIT),
        name="mixer",
    )(x_prompt, *consts)


def _spare_row(q):
    in_gap = q < TOP_K * SLAB_GAP
    gap_row = (q // SLAB_GAP) * T_PAD + T_ALL + q % SLAB_GAP
    return jnp.where(in_gap, gap_row, TAIL0 + q - TOP_K * SLAB_GAP).astype(jnp.int32)


def _routing_tables(idx, rank, cnt):
    def lookup(table, keys):
        ids = jnp.arange(table.shape[0], dtype=jnp.int32)
        return jnp.sum(jnp.where(keys[..., None] == ids, table, 0), axis=-1)

    counts = cnt[:, 0].astype(jnp.int32)
    padded = (counts + MOE_BLK - 1) // MOE_BLK * MOE_BLK
    pend = jnp.cumsum(padded)
    pstart = pend - padded

    dest = lookup(pstart, idx) + rank
    dest = jnp.pad(dest, ((0, 0), (0, DISP_STEPS * DISP_TILE - T_ALL)))
    disp = dest.reshape(TOP_K, DISP_STEPS, DISP_TILE).transpose(1, 0, 2).reshape(DISP_STEPS, TOP_K * DISP_TILE)

    n_pad = padded - counts
    pad_cum = jnp.cumsum(n_pad)
    q = jnp.arange(N_PAD_ROWS, dtype=jnp.int32)
    e_q = jnp.sum((q[:, None] >= pad_cum[None, :]).astype(jnp.int32), axis=1)
    first_pad = lookup(jnp.concatenate([pstart + counts, pend[-1:]]), e_q)
    pads_before = lookup(jnp.concatenate([pad_cum - n_pad, pad_cum[-1:]]), e_q)
    pad_rows = first_pad + q - pads_before

    pads = jnp.concatenate([pad_rows, _spare_row(2 * MOE_BLK + q)]) * ROW_TILE

    block_start = jnp.arange(N_BLOCKS, dtype=jnp.int32) * MOE_BLK
    block_exp = jnp.minimum(jnp.sum((block_start[:, None] >= pend[None, :]).astype(jnp.int32), axis=1),
                            N_EXPERTS - 1)
    blocks = jnp.arange(N_BLOCKS, dtype=jnp.int32)
    first = jnp.concatenate([jnp.ones((1,), jnp.bool_), block_exp[1:] != block_exp[:-1]])
    run_slot = (jnp.cumsum(first.astype(jnp.int32)) - 1) % 2
    later_first = first[None, :] & (blocks[None, :] > blocks[:, None])
    next_first = jnp.min(jnp.where(later_first, blocks[None, :], N_BLOCKS), axis=1)
    has_next = next_first < N_BLOCKS
    next_exp = lookup(block_exp, jnp.minimum(next_first, N_BLOCKS - 1))
    n_used = jnp.broadcast_to(pend[-1] // MOE_BLK, (N_BLOCKS,))
    plan = jnp.stack([block_exp, first.astype(jnp.int32), run_slot, next_exp, has_next.astype(jnp.int32),
                      n_used, jnp.zeros_like(n_used)]).astype(jnp.int32)
    return plan, disp * ROW_TILE, pads


def _dispatch_kernel(disp_hbm, pads_hbm, h2_ref, xs_hbm, inv_hbm, tbl0, tbl1, pad_tbl, inv_tbl, zero_tile,
                     rsem, tsem, psem):
    s = pl.program_id(0)
    ns = pl.num_programs(0)
    tbls = (tbl0, tbl1)

    def set_inverse(in_off, out_off):
        inv_tbl[lax.shift_right_logical(in_off, ROW_TILE_SHIFT)] = out_off

    def table_copy(row, p):
        return pltpu.make_async_copy(disp_hbm.at[row], tbls[p], tsem.at[p])

    def row_copy(off, r, k):
        return pltpu.make_async_copy(h2_ref.at[pl.ds(r * ROW_TILE, ROW_TILE), :],
                                     xs_hbm.at[pl.ds(off, ROW_TILE), :], rsem.at[k])

    def pad_copy(off):
        return pltpu.make_async_copy(zero_tile, xs_hbm.at[pl.ds(off, ROW_TILE), :], psem.at[0])

    def scatter_rows(p, n_rows):
        first_out = s * (DISP_TILE * ROW_TILE)
        for r in range(n_rows):
            for k in range(TOP_K):
                off = pl.multiple_of(tbls[p][k * DISP_TILE + r], ROW_TILE)
                row_copy(off, r, k).start(priority=(r + k) % 2)
                set_inverse(off, first_out + (k * T_PAD + r) * ROW_TILE)
        for r in range(n_rows):
            for k in range(TOP_K):
                row_copy(0, r, k).wait()

    def fill_pads():
        zero_tile[...] = jnp.zeros_like(zero_tile)
        load = pltpu.make_async_copy(pads_hbm, pad_tbl, psem.at[1])
        load.start()
        load.wait()

        def chunk(i, carry):
            for j in range(PAD_CHUNK):
                off = pl.multiple_of(pad_tbl[i * PAD_CHUNK + j], ROW_TILE)
                pad_copy(off).start(priority=j % 2)
                set_inverse(off, pad_tbl[N_PAD_ROWS + i * PAD_CHUNK + j])

            @pl.when(i > 0)
            def _():
                for j in range(PAD_CHUNK):
                    pad_copy(0).wait()
            return carry

        lax.fori_loop(0, N_PAD_ROWS // PAD_CHUNK, chunk, 0)
        for j in range(PAD_CHUNK):
            pad_copy(0).wait()
        done = pltpu.make_async_copy(inv_tbl, inv_hbm, psem.at[1])
        done.start()
        done.wait()

    @pl.when(s == 0)
    def _():
        table_copy(0, 0).start()

    def step(p):
        table_copy(s, p).wait()

        @pl.when(s < ns - 1)
        def _():
            table_copy(s + 1, 1 - p).start()
            scatter_rows(p, DISP_TILE)

        @pl.when(s == ns - 1)
        def _():
            scatter_rows(p, DISP_LAST_ROWS)
            fill_pads()

    parity = lax.rem(s, 2)
    pl.when(parity == 0)(lambda: step(0))
    pl.when(parity == 1)(lambda: step(1))


def _dispatch(disp, pads, h2):
    return pl.pallas_call(
        _dispatch_kernel,
        grid=(DISP_STEPS,),
        in_specs=[pl.BlockSpec(memory_space=pl.ANY),
                  pl.BlockSpec(memory_space=pl.ANY),
                  pl.BlockSpec((DISP_TILE * ROW_TILE, LANES), lambda s: (s, 0))],
        out_specs=[pl.BlockSpec(memory_space=pl.ANY), pl.BlockSpec(memory_space=pl.ANY)],
        out_shape=[jax.ShapeDtypeStruct((N_BLOCKS * MOE_BLK * ROW_TILE, LANES), _F32),
                   jax.ShapeDtypeStruct((N_BLOCKS * MOE_BLK,), jnp.int32)],
        scratch_shapes=[
            pltpu.SMEM((TOP_K * DISP_TILE,), jnp.int32),
            pltpu.SMEM((TOP_K * DISP_TILE,), jnp.int32),
            pltpu.SMEM((2 * N_PAD_ROWS,), jnp.int32),
            pltpu.SMEM((N_BLOCKS * MOE_BLK,), jnp.int32),
            pltpu.VMEM((ROW_TILE, LANES), _F32),
            pltpu.SemaphoreType.DMA((TOP_K,)),
            pltpu.SemaphoreType.DMA((2,)),
            pltpu.SemaphoreType.DMA((2,)),
        ],
        compiler_params=pltpu.CompilerParams(
            dimension_semantics=("arbitrary",), vmem_limit_bytes=VMEM_LIMIT),
        name="dispatch",
    )(disp, pads, h2)


PLAN_EXPERT, PLAN_FIRST, PLAN_SLOT, PLAN_NEXT, PLAN_HAS_NEXT, PLAN_USED, PLAN_ZERO = range(7)
SCATTER_GROUPS = 4


def _moe_kernel(plan_ref, table_hbm, x_ref, wgu_hbm, bgu_ref, wdn_hbm, bdn_ref, out_hbm,
                ybuf0, ybuf1, tbl0, tbl1, gu_stage0, gu_stage1, dn_stage0, dn_stage1, wgu_bf, wdn_bf,
                ssem, tsem, wsem):
    b = pl.program_id(0)
    nb = pl.num_programs(0)
    ybufs, tbls = (ybuf0, ybuf1), (tbl0, tbl1)
    gu_stages, dn_stages = (gu_stage0, gu_stage1), (dn_stage0, dn_stage1)
    n_used = plan_ref[PLAN_USED, 0]

    def table_copy(row, p):
        return pltpu.make_async_copy(table_hbm.at[row], tbls[p], tsem.at[p])

    def weight_copies(e, w):
        return (pltpu.make_async_copy(wgu_hbm.at[e], gu_stages[w], wsem.at[2 * w]),
                pltpu.make_async_copy(wdn_hbm.at[e], dn_stages[w], wsem.at[2 * w + 1]))

    def scatter_row(off, r, p):
        return pltpu.make_async_copy(ybufs[p].at[pl.ds(r * ROW_TILE, ROW_TILE), :],
                                     out_hbm.at[pl.ds(off, ROW_TILE), :], ssem.at[p])

    def start_scatter(tp, p, lo=0, hi=MOE_BLK, after=0):
        for r in range(lo, hi):
            off = pl.multiple_of(tbls[tp][r] + after, ROW_TILE)
            scatter_row(off, r, p).start(priority=ROW_DMA_PRIORITY)

    def zero_after(value):
        return lax.bitcast_convert_type(value, jnp.int32) * plan_ref[PLAN_ZERO, b]

    def wait_scatter(p):
        for r in range(MOE_BLK):
            scatter_row(0, r, p).wait()

    @pl.when(b == 0)
    def _():
        first = table_copy(0, 0)
        first.start()
        first.wait()
        ybuf0[...] = jnp.zeros_like(ybuf0)
        ybuf1[...] = jnp.zeros_like(ybuf1)
        start_scatter(0, 0)
        table_copy(1, 1).start()
        for c in weight_copies(plan_ref[PLAN_EXPERT, 0], 0):
            c.start(priority=WEIGHT_DMA_PRIORITY)

    def new_expert(w):
        for c in weight_copies(0, w):
            c.wait()

        @pl.when(plan_ref[PLAN_HAS_NEXT, b] == 1)
        def _():
            for c in weight_copies(plan_ref[PLAN_NEXT, b], 1 - w):
                c.start(priority=WEIGHT_DMA_PRIORITY)

        wgu_bf[...] = gu_stages[w][...].astype(_BF16)
        wdn_bf[...] = dn_stages[w][...].astype(_BF16)

    is_first = plan_ref[PLAN_FIRST, b] == 1
    wslot = plan_ref[PLAN_SLOT, b]
    pl.when(is_first & (wslot == 0))(lambda: new_expert(0))
    pl.when(is_first & (wslot == 1))(lambda: new_expert(1))

    def step(p):
        q = 1 - p
        table_copy(b + 1, q).wait()

        @pl.when(b < n_used)
        def _():
            group = MOE_BLK // SCATTER_GROUPS
            start_scatter(q, q, 0, group)
            table_copy(b + 2, p).start()
            x = _load_row_tiles(x_ref, MOE_BLK).astype(_BF16)
            hgu = jnp.dot(x, wgu_bf[...], preferred_element_type=_F32) + bgu_ref[...]
            for g in range(1, SCATTER_GROUPS):
                col = (2 * D_FF * g) // SCATTER_GROUPS - 1
                start_scatter(q, q, g * group, (g + 1) * group, zero_after(hgu[0, col]))
            gate = jnp.minimum(hgu[:, :D_FF], SWIGLU_LIMIT)
            up = jnp.clip(hgu[:, D_FF:], -SWIGLU_LIMIT, SWIGLU_LIMIT)
            act = gate * jax.nn.sigmoid(SWIGLU_ALPHA * gate) * (up + 1.0)
            y = jnp.dot(act.astype(_BF16), wdn_bf[...], preferred_element_type=_F32) + bdn_ref[...]
            wait_scatter(p)
            _store_row_tiles(ybufs[p], y)

        @pl.when(b >= n_used)
        def _():
            start_scatter(q, q)
            table_copy(b + 2, p).start()
            wait_scatter(p)
            ybufs[p][...] = jnp.zeros_like(ybufs[p])

        @pl.when(b == nb - 1)
        def _():
            table_copy(b + 2, p).wait()
            start_scatter(p, p)
            wait_scatter(p)
            wait_scatter(q)

    parity = lax.rem(b, 2)
    pl.when(parity == 0)(lambda: step(0))
    pl.when(parity == 1)(lambda: step(1))


def _moe_experts(plan, table, xs, wgu, bgu, wdn, bdn):
    grid_spec = pltpu.PrefetchScalarGridSpec(
        num_scalar_prefetch=1,
        grid=(N_BLOCKS,),
        in_specs=[
            pl.BlockSpec(memory_space=pl.ANY),
            pl.BlockSpec((MOE_BLK * ROW_TILE, LANES), lambda b, plan: (b, 0)),
            pl.BlockSpec(memory_space=pl.ANY),
            pl.BlockSpec((None, 1, 2 * D_FF), lambda b, plan: (plan[PLAN_EXPERT, b], 0, 0)),
            pl.BlockSpec(memory_space=pl.ANY),
            pl.BlockSpec((None, 1, D_MODEL), lambda b, plan: (plan[PLAN_EXPERT, b], 0, 0)),
        ],
        out_specs=pl.BlockSpec(memory_space=pl.ANY),
        scratch_shapes=[
            pltpu.VMEM((MOE_BLK * ROW_TILE, LANES), _F32),
            pltpu.VMEM((MOE_BLK * ROW_TILE, LANES), _F32),
            pltpu.SMEM((MOE_BLK,), jnp.int32),
            pltpu.SMEM((MOE_BLK,), jnp.int32),
            pltpu.VMEM((D_MODEL, 2 * D_FF), _F32),
            pltpu.VMEM((D_MODEL, 2 * D_FF), _F32),
            pltpu.VMEM((D_FF, D_MODEL), _F32),
            pltpu.VMEM((D_FF, D_MODEL), _F32),
            pltpu.VMEM((D_MODEL, 2 * D_FF), _BF16),
            pltpu.VMEM((D_FF, D_MODEL), _BF16),
            pltpu.SemaphoreType.DMA((2,)),
            pltpu.SemaphoreType.DMA((2,)),
            pltpu.SemaphoreType.DMA((4,)),
        ],
    )
    return pl.pallas_call(
        _moe_kernel,
        grid_spec=grid_spec,
        out_shape=jax.ShapeDtypeStruct((OUT_ROWS * ROW_TILE, LANES), _F32),
        compiler_params=pltpu.CompilerParams(
            dimension_semantics=("arbitrary",), vmem_limit_bytes=VMEM_LIMIT),
        name="moe_experts",
    )(plan, table, xs, wgu, bgu, wdn, bdn)


def _combine_kernel(x1_ref, g_ref, o0_ref, o1_ref, o2_ref, o3_ref, gfin_ref, yp_ref, ys_ref):
    i = pl.program_id(0)
    g = g_ref[...]
    o = [_load_row_tiles(r, COMB_TILE) for r in (o0_ref, o1_ref, o2_ref, o3_ref)]
    moe = (g[:, 0:1] * o[0] + g[:, 1:2] * o[1]) + (g[:, 2:3] * o[2] + g[:, 3:4] * o[3])
    y = _rms(x1_ref[...] + moe, gfin_ref[...])

    @pl.when(i < COMB_PROMPT_STEPS)
    def _():
        yp_ref[...] = y

    @pl.when(i == COMB_PROMPT_STEPS)
    def _():
        ys_ref[...] = y[:DEC_BATCH, :]


def _combine(x1, gates_tk, out4, g_final):
    slab_blocks = T_PAD // COMB_TILE

    def slab_spec(k):
        return pl.BlockSpec((COMB_TILE * ROW_TILE, LANES), lambda i, k=k: (k * slab_blocks + i, 0))

    return pl.pallas_call(
        _combine_kernel,
        grid=(COMB_PROMPT_STEPS + 1,),
        in_specs=[pl.BlockSpec((COMB_TILE, D_MODEL), lambda i: (i, 0)),
                  pl.BlockSpec((COMB_TILE, TOP_K), lambda i: (i, 0)),
                  slab_spec(0), slab_spec(1), slab_spec(2), slab_spec(3),
                  _const_spec(g_final.shape)],
        out_specs=[pl.BlockSpec((COMB_TILE, D_MODEL), lambda i: (jnp.minimum(i, COMB_PROMPT_STEPS - 1), 0)),
                   _const_spec((DEC_BATCH, D_MODEL))],
        out_shape=[jax.ShapeDtypeStruct((T_PROMPT, D_MODEL), _F32),
                   jax.ShapeDtypeStruct((DEC_BATCH, D_MODEL), _F32)],
        compiler_params=pltpu.CompilerParams(
            dimension_semantics=("arbitrary",), vmem_limit_bytes=VMEM_LIMIT),
        name="combine",
    )(x1, gates_tk, out4, out4, out4, out4, g_final)


def kernel(x_prompt, x_sample, state_pool, state_conv, meta_tokens, g_mix, w_in, w_pool_grp, pool_scale,
           w_pool_up, w_conv, w_conv_out, w_o, g_ffn, w_router, b_router, w_gu, b_gu, w_down, b_down, g_final):
    assert g_mix.shape[0] == 1, "single-layer step"
    grp = w_pool_grp[0].astype(_BF16)
    zero = jnp.zeros((POOL_GROUP, POOL_GROUP), _BF16)
    wgrp = jnp.stack([jnp.block([[grp[0], zero], [zero, grp[1]]]),
                      jnp.block([[grp[2], zero], [zero, grp[3]]])])
    w = (
        g_mix[0].reshape(1, D_MODEL),
        w_in[0].astype(_BF16),
        wgrp,
        pool_scale[0].reshape(1, D_POOL),
        w_pool_up[0].astype(_BF16),
        w_conv[0],
        w_conv_out[0].astype(_BF16),
        w_o[0].astype(_BF16),
        g_ffn[0].reshape(1, D_MODEL),
        w_router[0].T,
        b_router[0].reshape(N_EXPERTS, 1),
    )
    xs_pad = jnp.pad(x_sample.reshape(DEC_BATCH, D_MODEL), ((0, MIX_TILE - DEC_BATCH), (0, 0)))
    spool_t = jnp.transpose(state_pool[0], (1, 0, 2))
    sconv_t = jnp.transpose(state_conv[0], (1, 0, 2))
    (x1, h2, idx_p, gate_p, rank_p, idx_s, gate_s, rank_s, cnt, pool_p, conv_p, u_s, cv_s) = _mixer(
        x_prompt, xs_pad, spool_t, sconv_t, meta_tokens, w)
    idx = jnp.concatenate([idx_p, idx_s[:, :DEC_BATCH]], axis=1)
    gate = jnp.concatenate([gate_p, gate_s[:, :DEC_BATCH]], axis=1)
    rank = jnp.concatenate([rank_p, rank_s[:, :DEC_BATCH]], axis=1)

    plan, disp, pads = _routing_tables(idx, rank, cnt)
    xs, inv = _dispatch(disp, pads, h2)
    warmup = _spare_row(jnp.arange(2 * MOE_BLK, dtype=jnp.int32)).reshape(2, MOE_BLK) * ROW_TILE
    dst_rows = jnp.concatenate([warmup, inv.reshape(N_BLOCKS, MOE_BLK)], axis=0)
    out4 = _moe_experts(plan, dst_rows, xs,
                        w_gu.reshape(N_EXPERTS, D_MODEL, 2 * D_FF), b_gu.reshape(N_EXPERTS, 1, 2 * D_FF),
                        w_down.reshape(N_EXPERTS, D_FF, D_MODEL), b_down.reshape(N_EXPERTS, 1, D_MODEL))

    y_p, y_s = _combine(x1, gate.T, out4, g_final.reshape(1, D_MODEL))

    new_pool_p = pool_p[:, POOL_HALO - POOL_BUF:, :][None]
    new_conv_p = conv_p[:, CONV_HALO - CONV_BUF:, :][None]
    new_pool_s = jnp.concatenate([state_pool[0][:, 1:, :], u_s[:, None, :]], axis=1)[None]
    new_conv_s = jnp.concatenate([state_conv[0][:, 1:, :], cv_s[:, None, :]], axis=1)[None]
    return (y_p.reshape(BATCH, SEQ, D_MODEL), y_s.reshape(DEC_BATCH, 1, D_MODEL),
            new_pool_p, new_conv_p, new_pool_s, new_conv_s)
```

```python
import jax
import jax.numpy as jnp
from jax import lax
from jax.experimental import pallas as pl
from jax.experimental.pallas import tpu as pltpu

D_MODEL = 1024
BATCH = 8
SEQ = 2048
DEC_BATCH = 128
N_META = 16
D_POOL = 512
POOL_GROUP = 128
POOL_WINDOWS = (2, 4, 8, 16)
POOL_BUF = 15
D_CONV = 512
CONV_BUF = 2
N_EXPERTS = 32
TOP_K = 4
D_FF = 1024
SWIGLU_LIMIT = 7.0
SWIGLU_ALPHA = 1.702
RMS_EPS = 1e-5

O_CG = D_POOL
O_BG = O_CG + D_CONV
O_V = O_BG + D_CONV
O_GL = O_V + D_CONV

T_PROMPT = BATCH * SEQ
T_ALL = T_PROMPT + DEC_BATCH
N_ASSIGN = TOP_K * T_ALL

MIX_TILE = 512
MIX_PROMPT_STEPS = T_PROMPT // MIX_TILE
SEQ_TILES = SEQ // MIX_TILE
POOL_HALO = 16
CONV_HALO = 8
DISP_TILE = 256
DISP_STEPS = -(-T_ALL // DISP_TILE)
DISP_LAST_ROWS = T_ALL - (DISP_STEPS - 1) * DISP_TILE
MOE_BLK = 256
N_BLOCKS = -(-N_ASSIGN // MOE_BLK) + N_EXPERTS
N_PAD_ROWS = N_BLOCKS * MOE_BLK - N_ASSIGN
PAD_CHUNK = 64
assert N_PAD_ROWS % PAD_CHUNK == 0
ROW_DMA_PRIORITY = 0
WEIGHT_DMA_PRIORITY = 1
COMB_TILE = 512
COMB_PROMPT_STEPS = T_PROMPT // COMB_TILE
T_PAD = (T_ALL + COMB_TILE - 1) // COMB_TILE * COMB_TILE
SLAB_GAP = T_PAD - T_ALL
OUT_ROWS = (N_BLOCKS + 2) * MOE_BLK
TAIL0 = TOP_K * T_PAD
assert OUT_ROWS >= TAIL0

ROW_TILE = 8
ROW_TILE_SHIFT = 3
LANES = D_MODEL // ROW_TILE
assert 1 << ROW_TILE_SHIFT == ROW_TILE

VMEM_LIMIT = 56 * 1024 * 1024

_F32 = jnp.float32
_BF16 = jnp.bfloat16


def _rms(x, g):
    ms = jnp.mean(x * x, axis=-1, keepdims=True)
    return x * lax.rsqrt(ms + RMS_EPS) * g


def _bdot(a, w):
    return jnp.dot(a.astype(_BF16), w, preferred_element_type=_F32)


def _store_row_tiles(ref, val):
    rows = val.shape[0]
    for c in range(ROW_TILE):
        ref[pl.ds(c, rows, stride=ROW_TILE), :] = val[:, c * LANES:(c + 1) * LANES]


def _load_row_tiles(ref, rows):
    return jnp.concatenate([ref[pl.ds(c, rows, stride=ROW_TILE), :] for c in range(ROW_TILE)], axis=-1)


def _branches(z, pm, conv, wgrp_ref, pscale_ref, wup_ref, wcout_ref):
    pmb = pm.astype(_BF16)
    half = 2 * POOL_GROUP
    pg = jnp.concatenate(
        [jnp.dot(pmb[:, :half], wgrp_ref[0], preferred_element_type=_F32),
         jnp.dot(pmb[:, half:], wgrp_ref[1], preferred_element_type=_F32)], axis=-1)
    branch_a = _bdot(pg * pscale_ref[...], wup_ref[...])
    branch_b = _bdot(z[:, O_BG:O_V] * conv, wcout_ref[...])
    return branch_a, branch_b


def _merge(x, z, branch_a, branch_b, wo_ref, gffn_ref):
    gates = jax.nn.sigmoid(z[:, O_GL:])
    merged = gates[:, :D_MODEL] * branch_a + gates[:, D_MODEL:] * branch_b
    x1 = x + _bdot(merged, wo_ref[...])
    return x1, _rms(x1, gffn_ref[...])


def _route(h2, wrt_ref, br_ref, base_ref, n_live):
    logits = lax.dot_general(wrt_ref[...], h2, (((1,), (1,)), ((), ())),
                             precision=lax.Precision.HIGHEST,
                             preferred_element_type=_F32) + br_ref[...]
    iota = lax.broadcasted_iota(jnp.int32, logits.shape, 0)
    vals, idxs = [], []
    cur = logits
    for _ in range(TOP_K):
        m = jnp.max(cur, axis=0, keepdims=True)
        ik = jnp.min(jnp.where(cur == m, iota, N_EXPERTS), axis=0, keepdims=True)
        vals.append(m)
        idxs.append(ik)
        cur = jnp.where(iota == ik, -jnp.inf, cur)
    exps = [jnp.exp(v - vals[0]) for v in vals]
    denom = (exps[0] + exps[1]) + (exps[2] + exps[3])
    gate = [e / denom for e in exps]
    rows = logits.shape[1]
    live = lax.broadcasted_iota(jnp.int32, logits.shape, 1) < n_live
    onehots = [jnp.where(live, jnp.where(iota == ik, 1.0, 0.0), 0.0) for ik in idxs]
    member = (onehots[0] + onehots[1]) + (onehots[2] + onehots[3])
    earlier = (lax.broadcasted_iota(jnp.int32, (rows, rows), 0)
               < lax.broadcasted_iota(jnp.int32, (rows, rows), 1))
    before = jnp.dot(member.astype(_BF16), jnp.where(earlier, 1.0, 0.0).astype(_BF16),
                     preferred_element_type=_F32)
    pos = base_ref[...] + before
    ranks = [jnp.sum(oh * pos, axis=0, keepdims=True).astype(jnp.int32) for oh in onehots]
    base_ref[...] = base_ref[...] + jnp.sum(member, axis=1, keepdims=True)
    return idxs, gate, ranks


def _mixer_kernel(xp_ref, xs_ref, spool_ref, sconv_ref, meta_ref, gmix_ref, win_ref, wgrp_ref, pscale_ref,
                  wup_ref, wconv_ref, wcout_ref, wo_ref, gffn_ref, wrt_ref, br_ref,
                  x1_ref, h2_ref, idx_ref, gate_ref, rank_ref, cnt_ref, pool_out_ref, conv_out_ref, us_ref, cvs_ref,
                  pool_ext, conv_ext, meta_pool, meta_conv, base_scr):
    s = pl.program_id(0)
    is_prompt = s < MIX_PROMPT_STEPS
    j = lax.rem(s, SEQ_TILES)

    @pl.when(s == 0)
    def _():
        hm = _rms(meta_ref[...], gmix_ref[...]).astype(_BF16)
        zm = jnp.dot(hm, win_ref[:, :O_GL], preferred_element_type=_F32)
        meta_pool[...] = zm[:, :D_POOL]
        meta_conv[...] = zm[:, O_CG:O_BG] * zm[:, O_V:O_GL]
        base_scr[...] = jnp.zeros_like(base_scr)

    @pl.when(is_prompt & (j == 0))
    def _():
        pool_ext[0:POOL_HALO, :] = meta_pool[...]
        conv_ext[0:CONV_HALO, :] = meta_conv[N_META - CONV_HALO:, :]

    x = jnp.where(is_prompt, xp_ref[...], xs_ref[...])
    h = _rms(x, gmix_ref[...]).astype(_BF16)
    z = jnp.dot(h, win_ref[...], preferred_element_type=_F32)
    u = z[:, :D_POOL]
    cv = z[:, O_CG:O_BG] * z[:, O_V:O_GL]
    wc = wconv_ref[...]

    pool_ext[POOL_HALO:, :] = u
    ext = pool_ext[...]
    pms = []
    for g, w in enumerate(POOL_WINDOWS):
        lanes = slice(g * POOL_GROUP, (g + 1) * POOL_GROUP)
        acc = ext[:, lanes]
        sh = 1
        while sh < w:
            acc = acc + pltpu.roll(acc, sh, 0)
            sh *= 2
        pms.append(acc[POOL_HALO:, :] * (1.0 / w) - u[:, lanes])
    pm_prompt = jnp.concatenate(pms, axis=-1)
    conv_ext[CONV_HALO:, :] = cv
    cext = conv_ext[...]
    conv_prompt = (wc[0:1, :] * pltpu.roll(cext, 2, 0)[CONV_HALO:, :]
                   + wc[1:2, :] * pltpu.roll(cext, 1, 0)[CONV_HALO:, :]) + wc[2:3, :] * cv
    pool_ext[0:POOL_HALO, :] = u[MIX_TILE - POOL_HALO:, :]
    conv_ext[0:CONV_HALO, :] = cv[MIX_TILE - CONV_HALO:, :]

    us = u[:DEC_BATCH, :]
    cvs = cv[:DEC_BATCH, :]
    pms = []
    for g, w in enumerate(POOL_WINDOWS):
        lanes = slice(g * POOL_GROUP, (g + 1) * POOL_GROUP)
        acc = us[:, lanes]
        for t in range(POOL_BUF - (w - 1), POOL_BUF):
            acc = acc + spool_ref[t, :, lanes]
        pms.append(acc * (1.0 / w) - us[:, lanes])
    pad_rows = jnp.zeros((MIX_TILE - DEC_BATCH, D_POOL), _F32)
    pm_sample = jnp.concatenate([jnp.concatenate(pms, axis=-1), pad_rows], axis=0)
    conv_sample = jnp.concatenate(
        [(wc[0:1, :] * sconv_ref[0] + wc[1:2, :] * sconv_ref[1]) + wc[2:3, :] * cvs,
         jnp.zeros((MIX_TILE - DEC_BATCH, D_CONV), _F32)], axis=0)

    pm = jnp.where(is_prompt, pm_prompt, pm_sample)
    conv = jnp.where(is_prompt, conv_prompt, conv_sample)

    branch_a, branch_b = _branches(z, pm, conv, wgrp_ref, pscale_ref, wup_ref, wcout_ref)
    x1, h2 = _merge(x, z, branch_a, branch_b, wo_ref, gffn_ref)
    idxs, gate, ranks = _route(h2, wrt_ref, br_ref, base_scr, jnp.where(is_prompt, MIX_TILE, DEC_BATCH))
    x1_ref[...] = x1
    _store_row_tiles(h2_ref, h2)
    for k in range(TOP_K):
        idx_ref[k:k + 1, :] = idxs[k]
        gate_ref[k:k + 1, :] = gate[k]
        rank_ref[k:k + 1, :] = ranks[k]

    @pl.when(is_prompt & (j == SEQ_TILES - 1))
    def _():
        pool_out_ref[...] = u[MIX_TILE - POOL_HALO:, :]
        conv_out_ref[...] = cv[MIX_TILE - CONV_HALO:, :]

    @pl.when(jnp.logical_not(is_prompt))
    def _():
        us_ref[...] = us
        cvs_ref[...] = cvs
        cnt_ref[...] = jnp.broadcast_to(base_scr[...], cnt_ref.shape)


def _const_spec(shape):
    nd = len(shape)
    return pl.BlockSpec(shape, lambda *_: (0,) * nd)


def _resident_spec(shape):
    nd = len(shape)
    return pl.BlockSpec(shape, lambda *_: (0,) * nd, pipeline_mode=pl.Buffered(1))


def _mixer(x_prompt, xs_pad, spool_t, sconv_t, meta, w):
    last = MIX_PROMPT_STEPS - 1

    def x_map(s):
        sp = jnp.minimum(s, last)
        return (sp // SEQ_TILES, sp % SEQ_TILES, 0)

    def seq_map(s):
        return (jnp.minimum(s, last) // SEQ_TILES, 0, 0)


    consts = (xs_pad, spool_t, sconv_t, meta) + tuple(w)
    return pl.pallas_call(
        _mixer_kernel,
        grid=(MIX_PROMPT_STEPS + 1,),
        in_specs=[pl.BlockSpec((None, MIX_TILE, D_MODEL), x_map)] + [_resident_spec(a.shape) for a in consts],
        out_specs=[
            pl.BlockSpec((MIX_TILE, D_MODEL), lambda s: (s, 0)),
            pl.BlockSpec((MIX_TILE * ROW_TILE, LANES), lambda s: (s, 0)),
            pl.BlockSpec((TOP_K, MIX_TILE), lambda s: (0, s)),
            pl.BlockSpec((TOP_K, MIX_TILE), lambda s: (0, s)),
            pl.BlockSpec((TOP_K, MIX_TILE), lambda s: (0, s)),
            _const_spec((N_EXPERTS, 128)),
            pl.BlockSpec((None, POOL_HALO, D_POOL), seq_map),
            pl.BlockSpec((None, CONV_HALO, D_CONV), seq_map),
            _const_spec((DEC_BATCH, D_POOL)),
            _const_spec((DEC_BATCH, D_CONV)),
        ],
        out_shape=[
            jax.ShapeDtypeStruct((T_ALL, D_MODEL), _F32),
            jax.ShapeDtypeStruct((T_ALL * ROW_TILE, LANES), _F32),
            jax.ShapeDtypeStruct((TOP_K, T_ALL), jnp.int32),
            jax.ShapeDtypeStruct((TOP_K, T_ALL), _F32),
            jax.ShapeDtypeStruct((TOP_K, T_ALL), jnp.int32),
            jax.ShapeDtypeStruct((N_EXPERTS, 128), _F32),
            jax.ShapeDtypeStruct((BATCH, POOL_HALO, D_POOL), _F32),
            jax.ShapeDtypeStruct((BATCH, CONV_HALO, D_CONV), _F32),
            jax.ShapeDtypeStruct((DEC_BATCH, D_POOL), _F32),
            jax.ShapeDtypeStruct((DEC_BATCH, D_CONV), _F32),
        ],
        scratch_shapes=[
            pltpu.VMEM((POOL_HALO + MIX_TILE, D_POOL), _F32),
            pltpu.VMEM((CONV_HALO + MIX_TILE, D_CONV), _F32),
            pltpu.VMEM((N_META, D_POOL), _F32),
            pltpu.VMEM((N_META, D_CONV), _F32),
            pltpu.VMEM((N_EXPERTS, 1), _F32),
        ],
        compiler_params=pltpu.CompilerParams(
            dimension_semantics=("arbitrary",), vmem_limit_bytes=VMEM_LIMIT),
        name="mixer",
    )(x_prompt, *consts)


def _spare_row(q):
    in_gap = q < TOP_K * SLAB_GAP
    gap_row = (q // SLAB_GAP) * T_PAD + T_ALL + q % SLAB_GAP
    return jnp.where(in_gap, gap_row, TAIL0 + q - TOP_K * SLAB_GAP).astype(jnp.int32)


def _routing_tables(idx, rank, cnt):
    def lookup(table, keys):
        ids = jnp.arange(table.shape[0], dtype=jnp.int32)
        return jnp.sum(jnp.where(keys[..., None] == ids, table, 0), axis=-1)

    counts = cnt[:, 0].astype(jnp.int32)
    padded = (counts + MOE_BLK - 1) // MOE_BLK * MOE_BLK
    pend = jnp.cumsum(padded)
    pstart = pend - padded

    dest = lookup(pstart, idx) + rank
    dest = jnp.pad(dest, ((0, 0), (0, DISP_STEPS * DISP_TILE - T_ALL)))
    disp = dest.reshape(TOP_K, DISP_STEPS, DISP_TILE).transpose(1, 0, 2).reshape(DISP_STEPS, TOP_K * DISP_TILE)

    n_pad = padded - counts
    pad_cum = jnp.cumsum(n_pad)
    q = jnp.arange(N_PAD_ROWS, dtype=jnp.int32)
    e_q = jnp.sum((q[:, None] >= pad_cum[None, :]).astype(jnp.int32), axis=1)
    first_pad = lookup(jnp.concatenate([pstart + counts, pend[-1:]]), e_q)
    pads_before = lookup(jnp.concatenate([pad_cum - n_pad, pad_cum[-1:]]), e_q)
    pad_rows = first_pad + q - pads_before

    pads = jnp.concatenate([pad_rows, _spare_row(2 * MOE_BLK + q)]) * ROW_TILE

    block_start = jnp.arange(N_BLOCKS, dtype=jnp.int32) * MOE_BLK
    block_exp = jnp.minimum(jnp.sum((block_start[:, None] >= pend[None, :]).astype(jnp.int32), axis=1),
                            N_EXPERTS - 1)
    blocks = jnp.arange(N_BLOCKS, dtype=jnp.int32)
    first = jnp.concatenate([jnp.ones((1,), jnp.bool_), block_exp[1:] != block_exp[:-1]])
    run_slot = (jnp.cumsum(first.astype(jnp.int32)) - 1) % 2
    later_first = first[None, :] & (blocks[None, :] > blocks[:, None])
    next_first = jnp.min(jnp.where(later_first, blocks[None, :], N_BLOCKS), axis=1)
    has_next = next_first < N_BLOCKS
    next_exp = lookup(block_exp, jnp.minimum(next_first, N_BLOCKS - 1))
    n_used = jnp.broadcast_to(pend[-1] // MOE_BLK, (N_BLOCKS,))
    plan = jnp.stack([block_exp, first.astype(jnp.int32), run_slot, next_exp, has_next.astype(jnp.int32),
                      n_used, jnp.zeros_like(n_used)]).astype(jnp.int32)
    return plan, disp * ROW_TILE, pads


def _dispatch_kernel(disp_hbm, pads_hbm, h2_ref, xs_hbm, inv_hbm, tbl0, tbl1, pad_tbl, inv_tbl, zero_tile,
                     rsem, tsem, psem):
    s = pl.program_id(0)
    ns = pl.num_programs(0)
    tbls = (tbl0, tbl1)

    def set_inverse(in_off, out_off):
        inv_tbl[lax.shift_right_logical(in_off, ROW_TILE_SHIFT)] = out_off

    def table_copy(row, p):
        return pltpu.make_async_copy(disp_hbm.at[row], tbls[p], tsem.at[p])

    def row_copy(off, r, k):
        return pltpu.make_async_copy(h2_ref.at[pl.ds(r * ROW_TILE, ROW_TILE), :],
                                     xs_hbm.at[pl.ds(off, ROW_TILE), :], rsem.at[k])

    def pad_copy(off):
        return pltpu.make_async_copy(zero_tile, xs_hbm.at[pl.ds(off, ROW_TILE), :], psem.at[0])

    def scatter_rows(p, n_rows):
        first_out = s * (DISP_TILE * ROW_TILE)
        for r in range(n_rows):
            for k in range(TOP_K):
                off = pl.multiple_of(tbls[p][k * DISP_TILE + r], ROW_TILE)
                row_copy(off, r, k).start(priority=(r + k) % 2)
                set_inverse(off, first_out + (k * T_PAD + r) * ROW_TILE)
        for r in range(n_rows):
            for k in range(TOP_K):
                row_copy(0, r, k).wait()

    def fill_pads():
        zero_tile[...] = jnp.zeros_like(zero_tile)
        load = pltpu.make_async_copy(pads_hbm, pad_tbl, psem.at[1])
        load.start()
        load.wait()

        def chunk(i, carry):
            for j in range(PAD_CHUNK):
                off = pl.multiple_of(pad_tbl[i * PAD_CHUNK + j], ROW_TILE)
                pad_copy(off).start(priority=j % 2)
                set_inverse(off, pad_tbl[N_PAD_ROWS + i * PAD_CHUNK + j])

            @pl.when(i > 0)
            def _():
                for j in range(PAD_CHUNK):
                    pad_copy(0).wait()
            return carry

        lax.fori_loop(0, N_PAD_ROWS // PAD_CHUNK, chunk, 0)
        for j in range(PAD_CHUNK):
            pad_copy(0).wait()
        done = pltpu.make_async_copy(inv_tbl, inv_hbm, psem.at[1])
        done.start()
        done.wait()

    @pl.when(s == 0)
    def _():
        table_copy(0, 0).start()

    def step(p):
        table_copy(s, p).wait()

        @pl.when(s < ns - 1)
        def _():
            table_copy(s + 1, 1 - p).start()
            scatter_rows(p, DISP_TILE)

        @pl.when(s == ns - 1)
        def _():
            scatter_rows(p, DISP_LAST_ROWS)
            fill_pads()

    parity = lax.rem(s, 2)
    pl.when(parity == 0)(lambda: step(0))
    pl.when(parity == 1)(lambda: step(1))


def _dispatch(disp, pads, h2):
    return pl.pallas_call(
        _dispatch_kernel,
        grid=(DISP_STEPS,),
        in_specs=[pl.BlockSpec(memory_space=pl.ANY),
                  pl.BlockSpec(memory_space=pl.ANY),
                  pl.BlockSpec((DISP_TILE * ROW_TILE, LANES), lambda s: (s, 0))],
        out_specs=[pl.BlockSpec(memory_space=pl.ANY), pl.BlockSpec(memory_space=pl.ANY)],
        out_shape=[jax.ShapeDtypeStruct((N_BLOCKS * MOE_BLK * ROW_TILE, LANES), _F32),
                   jax.ShapeDtypeStruct((N_BLOCKS * MOE_BLK,), jnp.int32)],
        scratch_shapes=[
            pltpu.SMEM((TOP_K * DISP_TILE,), jnp.int32),
            pltpu.SMEM((TOP_K * DISP_TILE,), jnp.int32),
            pltpu.SMEM((2 * N_PAD_ROWS,), jnp.int32),
            pltpu.SMEM((N_BLOCKS * MOE_BLK,), jnp.int32),
            pltpu.VMEM((ROW_TILE, LANES), _F32),
            pltpu.SemaphoreType.DMA((TOP_K,)),
            pltpu.SemaphoreType.DMA((2,)),
            pltpu.SemaphoreType.DMA((2,)),
        ],
        compiler_params=pltpu.CompilerParams(
            dimension_semantics=("arbitrary",), vmem_limit_bytes=VMEM_LIMIT),
        name="dispatch",
    )(disp, pads, h2)


PLAN_EXPERT, PLAN_FIRST, PLAN_SLOT, PLAN_NEXT, PLAN_HAS_NEXT, PLAN_USED, PLAN_ZERO = range(7)
SCATTER_GROUPS = 4


def _moe_kernel(plan_ref, table_hbm, x_ref, wgu_hbm, bgu_ref, wdn_hbm, bdn_ref, out_hbm,
                ybuf0, ybuf1, tbl0, tbl1, gu_stage0, gu_stage1, dn_stage0, dn_stage1, wgu_bf, wdn_bf,
                ssem, tsem, wsem):
    b = pl.program_id(0)
    nb = pl.num_programs(0)
    ybufs, tbls = (ybuf0, ybuf1), (tbl0, tbl1)
    gu_stages, dn_stages = (gu_stage0, gu_stage1), (dn_stage0, dn_stage1)
    n_used = plan_ref[PLAN_USED, 0]

    def table_copy(row, p):
        return pltpu.make_async_copy(table_hbm.at[row], tbls[p], tsem.at[p])

    def weight_copies(e, w):
        return (pltpu.make_async_copy(wgu_hbm.at[e], gu_stages[w], wsem.at[2 * w]),
                pltpu.make_async_copy(wdn_hbm.at[e], dn_stages[w], wsem.at[2 * w + 1]))

    def scatter_row(off, r, p):
        return pltpu.make_async_copy(ybufs[p].at[pl.ds(r * ROW_TILE, ROW_TILE), :],
                                     out_hbm.at[pl.ds(off, ROW_TILE), :], ssem.at[p])

    def start_scatter(tp, p, lo=0, hi=MOE_BLK, after=0):
        for r in range(lo, hi):
            off = pl.multiple_of(tbls[tp][r] + after, ROW_TILE)
            scatter_row(off, r, p).start(priority=ROW_DMA_PRIORITY)

    def zero_after(value):
        return lax.bitcast_convert_type(value, jnp.int32) * plan_ref[PLAN_ZERO, b]

    def wait_scatter(p):
        for r in range(MOE_BLK):
            scatter_row(0, r, p).wait()

    @pl.when(b == 0)
    def _():
        first = table_copy(0, 0)
        first.start()
        first.wait()
        ybuf0[...] = jnp.zeros_like(ybuf0)
        ybuf1[...] = jnp.zeros_like(ybuf1)
        start_scatter(0, 0)
        table_copy(1, 1).start()
        for c in weight_copies(plan_ref[PLAN_EXPERT, 0], 0):
            c.start(priority=WEIGHT_DMA_PRIORITY)

    def new_expert(w):
        for c in weight_copies(0, w):
            c.wait()

        @pl.when(plan_ref[PLAN_HAS_NEXT, b] == 1)
        def _():
            for c in weight_copies(plan_ref[PLAN_NEXT, b], 1 - w):
                c.start(priority=WEIGHT_DMA_PRIORITY)

        wgu_bf[...] = gu_stages[w][...].astype(_BF16)
        wdn_bf[...] = dn_stages[w][...].astype(_BF16)

    is_first = plan_ref[PLAN_FIRST, b] == 1
    wslot = plan_ref[PLAN_SLOT, b]
    pl.when(is_first & (wslot == 0))(lambda: new_expert(0))
    pl.when(is_first & (wslot == 1))(lambda: new_expert(1))

    def step(p):
        q = 1 - p
        table_copy(b + 1, q).wait()

        @pl.when(b < n_used)
        def _():
            group = MOE_BLK // SCATTER_GROUPS
            start_scatter(q, q, 0, group)
            table_copy(b + 2, p).start()
            x = _load_row_tiles(x_ref, MOE_BLK).astype(_BF16)
            hgu = jnp.dot(x, wgu_bf[...], preferred_element_type=_F32) + bgu_ref[...]
            for g in range(1, SCATTER_GROUPS):
                col = (2 * D_FF * g) // SCATTER_GROUPS - 1
                start_scatter(q, q, g * group, (g + 1) * group, zero_after(hgu[0, col]))
            gate = jnp.minimum(hgu[:, :D_FF], SWIGLU_LIMIT)
            up = jnp.clip(hgu[:, D_FF:], -SWIGLU_LIMIT, SWIGLU_LIMIT)
            act = gate * jax.nn.sigmoid(SWIGLU_ALPHA * gate) * (up + 1.0)
            y = jnp.dot(act.astype(_BF16), wdn_bf[...], preferred_element_type=_F32) + bdn_ref[...]
            wait_scatter(p)
            _store_row_tiles(ybufs[p], y)

        @pl.when(b >= n_used)
        def _():
            start_scatter(q, q)
            table_copy(b + 2, p).start()
            wait_scatter(p)
            ybufs[p][...] = jnp.zeros_like(ybufs[p])

        @pl.when(b == nb - 1)
        def _():
            table_copy(b + 2, p).wait()
            start_scatter(p, p)
            wait_scatter(p)
            wait_scatter(q)

    parity = lax.rem(b, 2)
    pl.when(parity == 0)(lambda: step(0))
    pl.when(parity == 1)(lambda: step(1))


def _moe_experts(plan, table, xs, wgu, bgu, wdn, bdn):
    grid_spec = pltpu.PrefetchScalarGridSpec(
        num_scalar_prefetch=1,
        grid=(N_BLOCKS,),
        in_specs=[
            pl.BlockSpec(memory_space=pl.ANY),
            pl.BlockSpec((MOE_BLK * ROW_TILE, LANES), lambda b, plan: (b, 0)),
            pl.BlockSpec(memory_space=pl.ANY),
            pl.BlockSpec((None, 1, 2 * D_FF), lambda b, plan: (plan[PLAN_EXPERT, b], 0, 0)),
            pl.BlockSpec(memory_space=pl.ANY),
            pl.BlockSpec((None, 1, D_MODEL), lambda b, plan: (plan[PLAN_EXPERT, b], 0, 0)),
        ],
        out_specs=pl.BlockSpec(memory_space=pl.ANY),
        scratch_shapes=[
            pltpu.VMEM((MOE_BLK * ROW_TILE, LANES), _F32),
            pltpu.VMEM((MOE_BLK * ROW_TILE, LANES), _F32),
            pltpu.SMEM((MOE_BLK,), jnp.int32),
            pltpu.SMEM((MOE_BLK,), jnp.int32),
            pltpu.VMEM((D_MODEL, 2 * D_FF), _F32),
            pltpu.VMEM((D_MODEL, 2 * D_FF), _F32),
            pltpu.VMEM((D_FF, D_MODEL), _F32),
            pltpu.VMEM((D_FF, D_MODEL), _F32),
            pltpu.VMEM((D_MODEL, 2 * D_FF), _BF16),
            pltpu.VMEM((D_FF, D_MODEL), _BF16),
            pltpu.SemaphoreType.DMA((2,)),
            pltpu.SemaphoreType.DMA((2,)),
            pltpu.SemaphoreType.DMA((4,)),
        ],
    )
    return pl.pallas_call(
        _moe_kernel,
        grid_spec=grid_spec,
        out_shape=jax.ShapeDtypeStruct((OUT_ROWS * ROW_TILE, LANES), _F32),
        compiler_params=pltpu.CompilerParams(
            dimension_semantics=("arbitrary",), vmem_limit_bytes=VMEM_LIMIT),
        name="moe_experts",
    )(plan, table, xs, wgu, bgu, wdn, bdn)


def _combine_kernel(x1_ref, g_ref, o0_ref, o1_ref, o2_ref, o3_ref, gfin_ref, yp_ref, ys_ref):
    i = pl.program_id(0)
    g = g_ref[...]
    o = [_load_row_tiles(r, COMB_TILE) for r in (o0_ref, o1_ref, o2_ref, o3_ref)]
    moe = (g[:, 0:1] * o[0] + g[:, 1:2] * o[1]) + (g[:, 2:3] * o[2] + g[:, 3:4] * o[3])
    y = _rms(x1_ref[...] + moe, gfin_ref[...])

    @pl.when(i < COMB_PROMPT_STEPS)
    def _():
        yp_ref[...] = y

    @pl.when(i == COMB_PROMPT_STEPS)
    def _():
        ys_ref[...] = y[:DEC_BATCH, :]


def _combine(x1, gates_tk, out4, g_final):
    slab_blocks = T_PAD // COMB_TILE

    def slab_spec(k):
        return pl.BlockSpec((COMB_TILE * ROW_TILE, LANES), lambda i, k=k: (k * slab_blocks + i, 0))

    return pl.pallas_call(
        _combine_kernel,
        grid=(COMB_PROMPT_STEPS + 1,),
        in_specs=[pl.BlockSpec((COMB_TILE, D_MODEL), lambda i: (i, 0)),
                  pl.BlockSpec((COMB_TILE, TOP_K), lambda i: (i, 0)),
                  slab_spec(0), slab_spec(1), slab_spec(2), slab_spec(3),
                  _const_spec(g_final.shape)],
        out_specs=[pl.BlockSpec((COMB_TILE, D_MODEL), lambda i: (jnp.minimum(i, COMB_PROMPT_STEPS - 1), 0)),
                   _const_spec((DEC_BATCH, D_MODEL))],
        out_shape=[jax.ShapeDtypeStruct((T_PROMPT, D_MODEL), _F32),
                   jax.ShapeDtypeStruct((DEC_BATCH, D_MODEL), _F32)],
        compiler_params=pltpu.CompilerParams(
            dimension_semantics=("arbitrary",), vmem_limit_bytes=VMEM_LIMIT),
        name="combine",
    )(x1, gates_tk, out4, out4, out4, out4, g_final)


def kernel(x_prompt, x_sample, state_pool, state_conv, meta_tokens, g_mix, w_in, w_pool_grp, pool_scale,
           w_pool_up, w_conv, w_conv_out, w_o, g_ffn, w_router, b_router, w_gu, b_gu, w_down, b_down, g_final):
    assert g_mix.shape[0] == 1, "single-layer step"
    grp = w_pool_grp[0].astype(_BF16)
    zero = jnp.zeros((POOL_GROUP, POOL_GROUP), _BF16)
    wgrp = jnp.stack([jnp.block([[grp[0], zero], [zero, grp[1]]]),
                      jnp.block([[grp[2], zero], [zero, grp[3]]])])
    w = (
        g_mix[0].reshape(1, D_MODEL),
        w_in[0].astype(_BF16),
        wgrp,
        pool_scale[0].reshape(1, D_POOL),
        w_pool_up[0].astype(_BF16),
        w_conv[0],
        w_conv_out[0].astype(_BF16),
        w_o[0].astype(_BF16),
        g_ffn[0].reshape(1, D_MODEL),
        w_router[0].T,
        b_router[0].reshape(N_EXPERTS, 1),
    )
    xs_pad = jnp.pad(x_sample.reshape(DEC_BATCH, D_MODEL), ((0, MIX_TILE - DEC_BATCH), (0, 0)))
    spool_t = jnp.transpose(state_pool[0], (1, 0, 2))
    sconv_t = jnp.transpose(state_conv[0], (1, 0, 2))
    x1, h2, idx, gate, rank, cnt, pool_p, conv_p, u_s, cv_s = _mixer(
        x_prompt, xs_pad, spool_t, sconv_t, meta_tokens, w)

    plan, disp, pads = _routing_tables(idx, rank, cnt)
    xs, inv = _dispatch(disp, pads, h2)
    warmup = _spare_row(jnp.arange(2 * MOE_BLK, dtype=jnp.int32)).reshape(2, MOE_BLK) * ROW_TILE
    dst_rows = jnp.concatenate([warmup, inv.reshape(N_BLOCKS, MOE_BLK)], axis=0)
    out4 = _moe_experts(plan, dst_rows, xs,
                        w_gu.reshape(N_EXPERTS, D_MODEL, 2 * D_FF), b_gu.reshape(N_EXPERTS, 1, 2 * D_FF),
                        w_down.reshape(N_EXPERTS, D_FF, D_MODEL), b_down.reshape(N_EXPERTS, 1, D_MODEL))

    y_p, y_s = _combine(x1, gate.T, out4, g_final.reshape(1, D_MODEL))

    new_pool_p = pool_p[:, POOL_HALO - POOL_BUF:, :][None]
    new_conv_p = conv_p[:, CONV_HALO - CONV_BUF:, :][None]
    new_pool_s = jnp.concatenate([state_pool[0][:, 1:, :], u_s[:, None, :]], axis=1)[None]
    new_conv_s = jnp.concatenate([state_conv[0][:, 1:, :], cv_s[:, None, :]], axis=1)[None]
    return (y_p.reshape(BATCH, SEQ, D_MODEL), y_s.reshape(DEC_BATCH, 1, D_MODEL),
            new_pool_p, new_conv_p, new_pool_s, new_conv_s)
```

```python
import jax
import jax.numpy as jnp
from jax import lax
from jax.experimental import pallas as pl
from jax.experimental.pallas import tpu as pltpu

D_MODEL = 1024
BATCH = 8
SEQ = 2048
DEC_BATCH = 128
N_META = 16
D_POOL = 512
POOL_GROUP = 128
POOL_WINDOWS = (2, 4, 8, 16)
POOL_BUF = 15
D_CONV = 512
CONV_BUF = 2
N_EXPERTS = 32
TOP_K = 4
D_FF = 1024
SWIGLU_LIMIT = 7.0
SWIGLU_ALPHA = 1.702
RMS_EPS = 1e-5

SUBLANES = 8
VREG_LANES = 128
V7X_VMEM_BYTES = 64 * 1024 * 1024

O_CG = D_POOL
O_BG = O_CG + D_CONV
O_V = O_BG + D_CONV
O_GL = O_V + D_CONV

T_PROMPT = BATCH * SEQ
T_ALL = T_PROMPT + DEC_BATCH
N_ASSIGN = TOP_K * T_ALL

MIX_TILE = 512
MIX_PROMPT_STEPS = T_PROMPT // MIX_TILE
SEQ_TILES = SEQ // MIX_TILE
POOL_HALO = 2 * SUBLANES
CONV_HALO = SUBLANES
DISP_TILE = 512
DISP_STEPS = -(-T_ALL // DISP_TILE)
DISP_LAST_ROWS = T_ALL - (DISP_STEPS - 1) * DISP_TILE
MOE_BLK = 256
N_BLOCKS = -(-N_ASSIGN // MOE_BLK) + N_EXPERTS
N_PAD_ROWS = N_BLOCKS * MOE_BLK - N_ASSIGN
PAD_CHUNK = 64
assert N_PAD_ROWS % PAD_CHUNK == 0
ROW_DMA_PRIORITY = 0
WEIGHT_DMA_PRIORITY = 1
COMB_TILE = 512
COMB_PROMPT_STEPS = T_PROMPT // COMB_TILE
T_PAD = (T_ALL + COMB_TILE - 1) // COMB_TILE * COMB_TILE
SLAB_GAP = T_PAD - T_ALL
OUT_ROWS = (N_BLOCKS + 2) * MOE_BLK
TAIL0 = TOP_K * T_PAD
assert OUT_ROWS >= TAIL0

ROW_TILE = SUBLANES
ROW_TILE_SHIFT = 3
LANES = D_MODEL // ROW_TILE
assert 1 << ROW_TILE_SHIFT == ROW_TILE and LANES == VREG_LANES

VMEM_LIMIT = V7X_VMEM_BYTES // 8 * 7

_F32 = jnp.float32
_BF16 = jnp.bfloat16


def _rms(x, g):
    ms = jnp.mean(x * x, axis=-1, keepdims=True)
    return x * lax.rsqrt(ms + RMS_EPS) * g


def _bdot(a, w):
    return jnp.dot(a.astype(_BF16), w, preferred_element_type=_F32)


def _store_row_tiles(ref, val):
    rows = val.shape[0]
    for c in range(ROW_TILE):
        ref[pl.ds(c, rows, stride=ROW_TILE), :] = val[:, c * LANES:(c + 1) * LANES]


def _load_row_tiles(ref, rows):
    return jnp.concatenate([ref[pl.ds(c, rows, stride=ROW_TILE), :] for c in range(ROW_TILE)], axis=-1)


def _branches(z, pm, conv, wgrp_ref, pscale_ref, wup_ref, wcout_ref):
    pmb = pm.astype(_BF16)
    half = 2 * POOL_GROUP
    pg = jnp.concatenate(
        [jnp.dot(pmb[:, :half], wgrp_ref[0], preferred_element_type=_F32),
         jnp.dot(pmb[:, half:], wgrp_ref[1], preferred_element_type=_F32)], axis=-1)
    branch_a = _bdot(pg * pscale_ref[...], wup_ref[...])
    branch_b = _bdot(z[:, O_BG:O_V] * conv, wcout_ref[...])
    return branch_a, branch_b


def _merge(x, z, branch_a, branch_b, wo_ref, gffn_ref):
    gates = jax.nn.sigmoid(z[:, O_GL:])
    merged = gates[:, :D_MODEL] * branch_a + gates[:, D_MODEL:] * branch_b
    x1 = x + _bdot(merged, wo_ref[...])
    return x1, _rms(x1, gffn_ref[...])


def _route(h2, wrt_ref, br_ref, base_ref, n_live):
    logits = lax.dot_general(wrt_ref[...], h2, (((1,), (1,)), ((), ())),
                             precision=lax.Precision.HIGHEST,
                             preferred_element_type=_F32) + br_ref[...]
    iota = lax.broadcasted_iota(jnp.int32, logits.shape, 0)
    vals, idxs = [], []
    cur = logits
    for _ in range(TOP_K):
        m = jnp.max(cur, axis=0, keepdims=True)
        ik = jnp.min(jnp.where(cur == m, iota, N_EXPERTS), axis=0, keepdims=True)
        vals.append(m)
        idxs.append(ik)
        cur = jnp.where(iota == ik, -jnp.inf, cur)
    exps = [jnp.exp(v - vals[0]) for v in vals]
    denom = (exps[0] + exps[1]) + (exps[2] + exps[3])
    gate = [e / denom for e in exps]
    rows = logits.shape[1]
    live = lax.broadcasted_iota(jnp.int32, logits.shape, 1) < n_live
    onehots = [jnp.where(live, jnp.where(iota == ik, 1.0, 0.0), 0.0) for ik in idxs]
    member = (onehots[0] + onehots[1]) + (onehots[2] + onehots[3])
    earlier = (lax.broadcasted_iota(jnp.int32, (rows, rows), 0)
               < lax.broadcasted_iota(jnp.int32, (rows, rows), 1))
    before = jnp.dot(member.astype(_BF16), jnp.where(earlier, 1.0, 0.0).astype(_BF16),
                     preferred_element_type=_F32)
    pos = base_ref[...] + before
    ranks = [jnp.sum(oh * pos, axis=0, keepdims=True).astype(jnp.int32) for oh in onehots]
    base_ref[...] = base_ref[...] + jnp.sum(member, axis=1, keepdims=True)
    return idxs, gate, ranks


def _mixer_kernel(xp_ref, xs_ref, spool_ref, sconv_ref, meta_ref, gmix_ref, win_ref, wgrp_ref, pscale_ref,
                  wup_ref, wconv_ref, wcout_ref, wo_ref, gffn_ref, wrt_ref, br_ref,
                  x1_ref, h2_ref, idx_ref, gate_ref, rank_ref, cnt_ref, pool_out_ref, conv_out_ref, us_ref, cvs_ref,
                  pool_ext, conv_ext, meta_pool, meta_conv, base_scr):
    s = pl.program_id(0)
    is_prompt = s < MIX_PROMPT_STEPS
    j = lax.rem(s, SEQ_TILES)

    @pl.when(s == 0)
    def _():
        hm = _rms(meta_ref[...], gmix_ref[...]).astype(_BF16)
        zm = jnp.dot(hm, win_ref[:, :O_GL], preferred_element_type=_F32)
        meta_pool[...] = zm[:, :D_POOL]
        meta_conv[...] = zm[:, O_CG:O_BG] * zm[:, O_V:O_GL]
        base_scr[...] = jnp.zeros_like(base_scr)

    @pl.when(is_prompt & (j == 0))
    def _():
        pool_ext[0:POOL_HALO, :] = meta_pool[...]
        conv_ext[0:CONV_HALO, :] = meta_conv[N_META - CONV_HALO:, :]

    x = jnp.where(is_prompt, xp_ref[...], xs_ref[...])
    h = _rms(x, gmix_ref[...]).astype(_BF16)
    z = jnp.dot(h, win_ref[...], preferred_element_type=_F32)
    u = z[:, :D_POOL]
    cv = z[:, O_CG:O_BG] * z[:, O_V:O_GL]
    wc = wconv_ref[...]

    pool_ext[POOL_HALO:, :] = u
    ext = pool_ext[...]
    pms = []
    for g, w in enumerate(POOL_WINDOWS):
        lanes = slice(g * POOL_GROUP, (g + 1) * POOL_GROUP)
        acc = ext[:, lanes]
        sh = 1
        while sh < w:
            acc = acc + pltpu.roll(acc, sh, 0)
            sh *= 2
        pms.append(acc[POOL_HALO:, :] * (1.0 / w) - u[:, lanes])
    pm_prompt = jnp.concatenate(pms, axis=-1)
    conv_ext[CONV_HALO:, :] = cv
    cext = conv_ext[...]
    conv_prompt = (wc[0:1, :] * pltpu.roll(cext, 2, 0)[CONV_HALO:, :]
                   + wc[1:2, :] * pltpu.roll(cext, 1, 0)[CONV_HALO:, :]) + wc[2:3, :] * cv
    pool_ext[0:POOL_HALO, :] = u[MIX_TILE - POOL_HALO:, :]
    conv_ext[0:CONV_HALO, :] = cv[MIX_TILE - CONV_HALO:, :]

    us = u[:DEC_BATCH, :]
    cvs = cv[:DEC_BATCH, :]
    pms = []
    for g, w in enumerate(POOL_WINDOWS):
        lanes = slice(g * POOL_GROUP, (g + 1) * POOL_GROUP)
        acc = us[:, lanes]
        for t in range(POOL_BUF - (w - 1), POOL_BUF):
            acc = acc + spool_ref[t, :, lanes]
        pms.append(acc * (1.0 / w) - us[:, lanes])
    pad_rows = jnp.zeros((MIX_TILE - DEC_BATCH, D_POOL), _F32)
    pm_sample = jnp.concatenate([jnp.concatenate(pms, axis=-1), pad_rows], axis=0)
    conv_sample = jnp.concatenate(
        [(wc[0:1, :] * sconv_ref[0] + wc[1:2, :] * sconv_ref[1]) + wc[2:3, :] * cvs,
         jnp.zeros((MIX_TILE - DEC_BATCH, D_CONV), _F32)], axis=0)

    pm = jnp.where(is_prompt, pm_prompt, pm_sample)
    conv = jnp.where(is_prompt, conv_prompt, conv_sample)

    branch_a, branch_b = _branches(z, pm, conv, wgrp_ref, pscale_ref, wup_ref, wcout_ref)
    x1, h2 = _merge(x, z, branch_a, branch_b, wo_ref, gffn_ref)
    idxs, gate, ranks = _route(h2, wrt_ref, br_ref, base_scr, jnp.where(is_prompt, MIX_TILE, DEC_BATCH))
    x1_ref[...] = x1
    _store_row_tiles(h2_ref, h2)
    for k in range(TOP_K):
        idx_ref[k:k + 1, :] = idxs[k]
        gate_ref[k:k + 1, :] = gate[k]
        rank_ref[k:k + 1, :] = ranks[k]

    @pl.when(is_prompt & (j == SEQ_TILES - 1))
    def _():
        pool_out_ref[...] = u[MIX_TILE - POOL_HALO:, :]
        conv_out_ref[...] = cv[MIX_TILE - CONV_HALO:, :]

    @pl.when(jnp.logical_not(is_prompt))
    def _():
        us_ref[...] = us
        cvs_ref[...] = cvs
        cnt_ref[...] = jnp.broadcast_to(base_scr[...], cnt_ref.shape)


def _const_spec(shape):
    nd = len(shape)
    return pl.BlockSpec(shape, lambda *_: (0,) * nd)


def _resident_spec(shape):
    nd = len(shape)
    return pl.BlockSpec(shape, lambda *_: (0,) * nd, pipeline_mode=pl.Buffered(1))


def _mixer(x_prompt, xs_pad, spool_t, sconv_t, meta, w):
    last = MIX_PROMPT_STEPS - 1

    def x_map(s):
        sp = jnp.minimum(s, last)
        return (sp // SEQ_TILES, sp % SEQ_TILES, 0)

    def seq_map(s):
        return (jnp.minimum(s, last) // SEQ_TILES, 0, 0)


    consts = (xs_pad, spool_t, sconv_t, meta) + tuple(w)
    return pl.pallas_call(
        _mixer_kernel,
        grid=(MIX_PROMPT_STEPS + 1,),
        in_specs=[pl.BlockSpec((None, MIX_TILE, D_MODEL), x_map)] + [_resident_spec(a.shape) for a in consts],
        out_specs=[
            pl.BlockSpec((MIX_TILE, D_MODEL), lambda s: (s, 0)),
            pl.BlockSpec((MIX_TILE * ROW_TILE, LANES), lambda s: (s, 0)),
            pl.BlockSpec((TOP_K, MIX_TILE), lambda s: (0, s)),
            pl.BlockSpec((TOP_K, MIX_TILE), lambda s: (0, s)),
            pl.BlockSpec((TOP_K, MIX_TILE), lambda s: (0, s)),
            _const_spec((N_EXPERTS, VREG_LANES)),
            pl.BlockSpec((None, POOL_HALO, D_POOL), seq_map),
            pl.BlockSpec((None, CONV_HALO, D_CONV), seq_map),
            _const_spec((DEC_BATCH, D_POOL)),
            _const_spec((DEC_BATCH, D_CONV)),
        ],
        out_shape=[
            jax.ShapeDtypeStruct((T_ALL, D_MODEL), _F32),
            jax.ShapeDtypeStruct((T_ALL * ROW_TILE, LANES), _F32),
            jax.ShapeDtypeStruct((TOP_K, T_ALL), jnp.int32),
            jax.ShapeDtypeStruct((TOP_K, T_ALL), _F32),
            jax.ShapeDtypeStruct((TOP_K, T_ALL), jnp.int32),
            jax.ShapeDtypeStruct((N_EXPERTS, VREG_LANES), _F32),
            jax.ShapeDtypeStruct((BATCH, POOL_HALO, D_POOL), _F32),
            jax.ShapeDtypeStruct((BATCH, CONV_HALO, D_CONV), _F32),
            jax.ShapeDtypeStruct((DEC_BATCH, D_POOL), _F32),
            jax.ShapeDtypeStruct((DEC_BATCH, D_CONV), _F32),
        ],
        scratch_shapes=[
            pltpu.VMEM((POOL_HALO + MIX_TILE, D_POOL), _F32),
            pltpu.VMEM((CONV_HALO + MIX_TILE, D_CONV), _F32),
            pltpu.VMEM((N_META, D_POOL), _F32),
            pltpu.VMEM((N_META, D_CONV), _F32),
            pltpu.VMEM((N_EXPERTS, 1), _F32),
        ],
        compiler_params=pltpu.CompilerParams(
            dimension_semantics=("arbitrary",), vmem_limit_bytes=VMEM_LIMIT),
        name="mixer",
    )(x_prompt, *consts)


def _spare_row(q):
    in_gap = q < TOP_K * SLAB_GAP
    gap_row = (q // SLAB_GAP) * T_PAD + T_ALL + q % SLAB_GAP
    return jnp.where(in_gap, gap_row, TAIL0 + q - TOP_K * SLAB_GAP).astype(jnp.int32)


def _routing_tables(idx, rank, cnt):
    def lookup(table, keys):
        ids = jnp.arange(table.shape[0], dtype=jnp.int32)
        return jnp.sum(jnp.where(keys[..., None] == ids, table, 0), axis=-1)

    counts = cnt[:, 0].astype(jnp.int32)
    padded = (counts + MOE_BLK - 1) // MOE_BLK * MOE_BLK
    pend = jnp.cumsum(padded)
    pstart = pend - padded

    dest = lookup(pstart, idx) + rank
    dest = jnp.pad(dest, ((0, 0), (0, DISP_STEPS * DISP_TILE - T_ALL)))
    disp = dest.reshape(TOP_K, DISP_STEPS, DISP_TILE).transpose(1, 0, 2).reshape(DISP_STEPS, TOP_K * DISP_TILE)

    n_pad = padded - counts
    pad_cum = jnp.cumsum(n_pad)
    q = jnp.arange(N_PAD_ROWS, dtype=jnp.int32)
    e_q = jnp.sum((q[:, None] >= pad_cum[None, :]).astype(jnp.int32), axis=1)
    first_pad = lookup(jnp.concatenate([pstart + counts, pend[-1:]]), e_q)
    pads_before = lookup(jnp.concatenate([pad_cum - n_pad, pad_cum[-1:]]), e_q)
    pad_rows = first_pad + q - pads_before

    pads = jnp.concatenate([pad_rows, _spare_row(2 * MOE_BLK + q)]) * ROW_TILE

    block_start = jnp.arange(N_BLOCKS, dtype=jnp.int32) * MOE_BLK
    block_exp = jnp.minimum(jnp.sum((block_start[:, None] >= pend[None, :]).astype(jnp.int32), axis=1),
                            N_EXPERTS - 1)
    blocks = jnp.arange(N_BLOCKS, dtype=jnp.int32)
    first = jnp.concatenate([jnp.ones((1,), jnp.bool_), block_exp[1:] != block_exp[:-1]])
    run_slot = (jnp.cumsum(first.astype(jnp.int32)) - 1) % 2
    later_first = first[None, :] & (blocks[None, :] > blocks[:, None])
    next_first = jnp.min(jnp.where(later_first, blocks[None, :], N_BLOCKS), axis=1)
    has_next = next_first < N_BLOCKS
    next_exp = lookup(block_exp, jnp.minimum(next_first, N_BLOCKS - 1))
    n_used = jnp.broadcast_to(pend[-1] // MOE_BLK, (N_BLOCKS,))
    plan = jnp.stack([block_exp, first.astype(jnp.int32), run_slot, next_exp, has_next.astype(jnp.int32),
                      n_used, jnp.zeros_like(n_used)]).astype(jnp.int32)
    return plan, disp * ROW_TILE, pads


def _dispatch_kernel(disp_hbm, pads_hbm, h2_ref, xs_hbm, inv_hbm, tbl0, tbl1, pad_tbl, inv_tbl, zero_tile,
                     rsem, tsem, psem):
    s = pl.program_id(0)
    ns = pl.num_programs(0)
    tbls = (tbl0, tbl1)

    def set_inverse(in_off, out_off):
        inv_tbl[lax.shift_right_logical(in_off, ROW_TILE_SHIFT)] = out_off

    def table_copy(row, p):
        return pltpu.make_async_copy(disp_hbm.at[row], tbls[p], tsem.at[p])

    def row_copy(off, r, k):
        return pltpu.make_async_copy(h2_ref.at[pl.ds(r * ROW_TILE, ROW_TILE), :],
                                     xs_hbm.at[pl.ds(off, ROW_TILE), :], rsem.at[k])

    def pad_copy(off):
        return pltpu.make_async_copy(zero_tile, xs_hbm.at[pl.ds(off, ROW_TILE), :], psem.at[0])

    def scatter_rows(p, n_rows):
        first_out = s * (DISP_TILE * ROW_TILE)
        for r in range(n_rows):
            for k in range(TOP_K):
                off = pl.multiple_of(tbls[p][k * DISP_TILE + r], ROW_TILE)
                row_copy(off, r, k).start(priority=(r + k) % 2)
                set_inverse(off, first_out + (k * T_PAD + r) * ROW_TILE)
        for r in range(n_rows):
            for k in range(TOP_K):
                row_copy(0, r, k).wait()

    def fill_pads():
        zero_tile[...] = jnp.zeros_like(zero_tile)
        load = pltpu.make_async_copy(pads_hbm, pad_tbl, psem.at[1])
        load.start()
        load.wait()

        def chunk(i, carry):
            for j in range(PAD_CHUNK):
                off = pl.multiple_of(pad_tbl[i * PAD_CHUNK + j], ROW_TILE)
                pad_copy(off).start(priority=j % 2)
                set_inverse(off, pad_tbl[N_PAD_ROWS + i * PAD_CHUNK + j])

            @pl.when(i > 0)
            def _():
                for j in range(PAD_CHUNK):
                    pad_copy(0).wait()
            return carry

        lax.fori_loop(0, N_PAD_ROWS // PAD_CHUNK, chunk, 0)
        for j in range(PAD_CHUNK):
            pad_copy(0).wait()
        done = pltpu.make_async_copy(inv_tbl, inv_hbm, psem.at[1])
        done.start()
        done.wait()

    @pl.when(s == 0)
    def _():
        table_copy(0, 0).start()

    def step(p):
        table_copy(s, p).wait()

        @pl.when(s < ns - 1)
        def _():
            table_copy(s + 1, 1 - p).start()
            scatter_rows(p, DISP_TILE)

        @pl.when(s == ns - 1)
        def _():
            scatter_rows(p, DISP_LAST_ROWS)
            fill_pads()

    parity = lax.rem(s, 2)
    pl.when(parity == 0)(lambda: step(0))
    pl.when(parity == 1)(lambda: step(1))


def _dispatch(disp, pads, h2):
    return pl.pallas_call(
        _dispatch_kernel,
        grid=(DISP_STEPS,),
        in_specs=[pl.BlockSpec(memory_space=pl.ANY),
                  pl.BlockSpec(memory_space=pl.ANY),
                  pl.BlockSpec((DISP_TILE * ROW_TILE, LANES), lambda s: (s, 0))],
        out_specs=[pl.BlockSpec(memory_space=pl.ANY), pl.BlockSpec(memory_space=pl.ANY)],
        out_shape=[jax.ShapeDtypeStruct((N_BLOCKS * MOE_BLK * ROW_TILE, LANES), _F32),
                   jax.ShapeDtypeStruct((N_BLOCKS * MOE_BLK,), jnp.int32)],
        scratch_shapes=[
            pltpu.SMEM((TOP_K * DISP_TILE,), jnp.int32),
            pltpu.SMEM((TOP_K * DISP_TILE,), jnp.int32),
            pltpu.SMEM((2 * N_PAD_ROWS,), jnp.int32),
            pltpu.SMEM((N_BLOCKS * MOE_BLK,), jnp.int32),
            pltpu.VMEM((ROW_TILE, LANES), _F32),
            pltpu.SemaphoreType.DMA((TOP_K,)),
            pltpu.SemaphoreType.DMA((2,)),
            pltpu.SemaphoreType.DMA((2,)),
        ],
        compiler_params=pltpu.CompilerParams(
            dimension_semantics=("arbitrary",), vmem_limit_bytes=VMEM_LIMIT),
        name="dispatch",
    )(disp, pads, h2)


PLAN_EXPERT, PLAN_FIRST, PLAN_SLOT, PLAN_NEXT, PLAN_HAS_NEXT, PLAN_USED, PLAN_ZERO = range(7)
SCATTER_GROUPS = 4


def _moe_kernel(plan_ref, table_ref, x_ref, wgu_hbm, bgu_ref, wdn_hbm, bdn_ref, out_hbm,
                ybuf0, ybuf1, gu_stage0, gu_stage1, dn_stage0, dn_stage1, wgu_bf, wdn_bf, ssem, wsem):
    b = pl.program_id(0)
    nb = pl.num_programs(0)
    ybufs = (ybuf0, ybuf1)
    gu_stages, dn_stages = (gu_stage0, gu_stage1), (dn_stage0, dn_stage1)
    n_used = plan_ref[PLAN_USED, 0]

    def weight_copies(e, w):
        return (pltpu.make_async_copy(wgu_hbm.at[e], gu_stages[w], wsem.at[2 * w]),
                pltpu.make_async_copy(wdn_hbm.at[e], dn_stages[w], wsem.at[2 * w + 1]))

    def scatter_row(off, r, p):
        return pltpu.make_async_copy(ybufs[p].at[pl.ds(r * ROW_TILE, ROW_TILE), :],
                                     out_hbm.at[pl.ds(off, ROW_TILE), :], ssem.at[p])

    def start_scatter(row, p, lo=0, hi=MOE_BLK, after=0):
        first = row * MOE_BLK
        for r in range(lo, hi):
            off = pl.multiple_of(table_ref[first + r] + after, ROW_TILE)
            scatter_row(off, r, p).start(priority=ROW_DMA_PRIORITY)

    def zero_after(value):
        return lax.bitcast_convert_type(value, jnp.int32) * plan_ref[PLAN_ZERO, b]

    def wait_scatter(p):
        for r in range(MOE_BLK):
            scatter_row(0, r, p).wait()

    @pl.when(b == 0)
    def _():
        ybuf0[...] = jnp.zeros_like(ybuf0)
        ybuf1[...] = jnp.zeros_like(ybuf1)
        start_scatter(0, 0)
        for c in weight_copies(plan_ref[PLAN_EXPERT, 0], 0):
            c.start(priority=WEIGHT_DMA_PRIORITY)

    def new_expert(w):
        for c in weight_copies(0, w):
            c.wait()

        @pl.when(plan_ref[PLAN_HAS_NEXT, b] == 1)
        def _():
            for c in weight_copies(plan_ref[PLAN_NEXT, b], 1 - w):
                c.start(priority=WEIGHT_DMA_PRIORITY)

        wgu_bf[...] = gu_stages[w][...].astype(_BF16)
        wdn_bf[...] = dn_stages[w][...].astype(_BF16)

    is_first = plan_ref[PLAN_FIRST, b] == 1
    wslot = plan_ref[PLAN_SLOT, b]
    pl.when(is_first & (wslot == 0))(lambda: new_expert(0))
    pl.when(is_first & (wslot == 1))(lambda: new_expert(1))

    def step(p):
        q = 1 - p

        @pl.when(b < n_used)
        def _():
            group = MOE_BLK // SCATTER_GROUPS
            start_scatter(b + 1, q, 0, group)
            x = _load_row_tiles(x_ref, MOE_BLK).astype(_BF16)
            hgu = jnp.dot(x, wgu_bf[...], preferred_element_type=_F32) + bgu_ref[...]
            for g in range(1, SCATTER_GROUPS):
                col = (2 * D_FF * g) // SCATTER_GROUPS - 1
                start_scatter(b + 1, q, g * group, (g + 1) * group, zero_after(hgu[0, col]))
            gate = jnp.minimum(hgu[:, :D_FF], SWIGLU_LIMIT)
            up = jnp.clip(hgu[:, D_FF:], -SWIGLU_LIMIT, SWIGLU_LIMIT)
            act = gate * jax.nn.sigmoid(SWIGLU_ALPHA * gate) * (up + 1.0)
            y = jnp.dot(act.astype(_BF16), wdn_bf[...], preferred_element_type=_F32) + bdn_ref[...]
            wait_scatter(p)
            _store_row_tiles(ybufs[p], y)

        @pl.when(b >= n_used)
        def _():
            start_scatter(b + 1, q)
            wait_scatter(p)
            ybufs[p][...] = jnp.zeros_like(ybufs[p])

        @pl.when(b == nb - 1)
        def _():
            start_scatter(b + 2, p)
            wait_scatter(p)
            wait_scatter(q)

    parity = lax.rem(b, 2)
    pl.when(parity == 0)(lambda: step(0))
    pl.when(parity == 1)(lambda: step(1))


def _moe_experts(plan, table, xs, wgu, bgu, wdn, bdn):
    grid_spec = pltpu.PrefetchScalarGridSpec(
        num_scalar_prefetch=2,
        grid=(N_BLOCKS,),
        in_specs=[
            pl.BlockSpec((MOE_BLK * ROW_TILE, LANES), lambda b, plan, table: (b, 0)),
            pl.BlockSpec(memory_space=pl.ANY),
            pl.BlockSpec((None, 1, 2 * D_FF), lambda b, plan, table: (plan[PLAN_EXPERT, b], 0, 0)),
            pl.BlockSpec(memory_space=pl.ANY),
            pl.BlockSpec((None, 1, D_MODEL), lambda b, plan, table: (plan[PLAN_EXPERT, b], 0, 0)),
        ],
        out_specs=pl.BlockSpec(memory_space=pl.ANY),
        scratch_shapes=[
            pltpu.VMEM((MOE_BLK * ROW_TILE, LANES), _F32),
            pltpu.VMEM((MOE_BLK * ROW_TILE, LANES), _F32),
            pltpu.VMEM((D_MODEL, 2 * D_FF), _F32),
            pltpu.VMEM((D_MODEL, 2 * D_FF), _F32),
            pltpu.VMEM((D_FF, D_MODEL), _F32),
            pltpu.VMEM((D_FF, D_MODEL), _F32),
            pltpu.VMEM((D_MODEL, 2 * D_FF), _BF16),
            pltpu.VMEM((D_FF, D_MODEL), _BF16),
            pltpu.SemaphoreType.DMA((2,)),
            pltpu.SemaphoreType.DMA((4,)),
        ],
    )
    return pl.pallas_call(
        _moe_kernel,
        grid_spec=grid_spec,
        out_shape=jax.ShapeDtypeStruct((OUT_ROWS * ROW_TILE, LANES), _F32),
        compiler_params=pltpu.CompilerParams(
            dimension_semantics=("arbitrary",), vmem_limit_bytes=VMEM_LIMIT),
        name="moe_experts",
    )(plan, table.reshape(-1), xs, wgu, bgu, wdn, bdn)


def _combine_kernel(x1_ref, g_ref, o0_ref, o1_ref, o2_ref, o3_ref, gfin_ref, yp_ref, ys_ref):
    i = pl.program_id(0)
    g = g_ref[...]
    o = [_load_row_tiles(r, COMB_TILE) for r in (o0_ref, o1_ref, o2_ref, o3_ref)]
    moe = (g[:, 0:1] * o[0] + g[:, 1:2] * o[1]) + (g[:, 2:3] * o[2] + g[:, 3:4] * o[3])
    y = _rms(x1_ref[...] + moe, gfin_ref[...])

    @pl.when(i < COMB_PROMPT_STEPS)
    def _():
        yp_ref[...] = y

    @pl.when(i == COMB_PROMPT_STEPS)
    def _():
        ys_ref[...] = y[:DEC_BATCH, :]


def _combine(x1, gates_tk, out4, g_final):
    slab_blocks = T_PAD // COMB_TILE

    def slab_spec(k):
        return pl.BlockSpec((COMB_TILE * ROW_TILE, LANES), lambda i, k=k: (k * slab_blocks + i, 0))

    return pl.pallas_call(
        _combine_kernel,
        grid=(COMB_PROMPT_STEPS + 1,),
        in_specs=[pl.BlockSpec((COMB_TILE, D_MODEL), lambda i: (i, 0)),
                  pl.BlockSpec((COMB_TILE, TOP_K), lambda i: (i, 0)),
                  slab_spec(0), slab_spec(1), slab_spec(2), slab_spec(3),
                  _const_spec(g_final.shape)],
        out_specs=[pl.BlockSpec((COMB_TILE, D_MODEL), lambda i: (jnp.minimum(i, COMB_PROMPT_STEPS - 1), 0)),
                   _const_spec((DEC_BATCH, D_MODEL))],
        out_shape=[jax.ShapeDtypeStruct((T_PROMPT, D_MODEL), _F32),
                   jax.ShapeDtypeStruct((DEC_BATCH, D_MODEL), _F32)],
        compiler_params=pltpu.CompilerParams(
            dimension_semantics=("arbitrary",), vmem_limit_bytes=VMEM_LIMIT),
        name="combine",
    )(x1, gates_tk, out4, out4, out4, out4, g_final)


def kernel(x_prompt, x_sample, state_pool, state_conv, meta_tokens, g_mix, w_in, w_pool_grp, pool_scale,
           w_pool_up, w_conv, w_conv_out, w_o, g_ffn, w_router, b_router, w_gu, b_gu, w_down, b_down, g_final):
    assert g_mix.shape[0] == 1, "single-layer step"
    grp = w_pool_grp[0].astype(_BF16)
    zero = jnp.zeros((POOL_GROUP, POOL_GROUP), _BF16)
    wgrp = jnp.stack([jnp.block([[grp[0], zero], [zero, grp[1]]]),
                      jnp.block([[grp[2], zero], [zero, grp[3]]])])
    w = (
        g_mix[0].reshape(1, D_MODEL),
        w_in[0].astype(_BF16),
        wgrp,
        pool_scale[0].reshape(1, D_POOL),
        w_pool_up[0].astype(_BF16),
        w_conv[0],
        w_conv_out[0].astype(_BF16),
        w_o[0].astype(_BF16),
        g_ffn[0].reshape(1, D_MODEL),
        w_router[0].T,
        b_router[0].reshape(N_EXPERTS, 1),
    )
    xs_pad = jnp.pad(x_sample.reshape(DEC_BATCH, D_MODEL), ((0, MIX_TILE - DEC_BATCH), (0, 0)))
    spool_t = jnp.transpose(state_pool[0], (1, 0, 2))
    sconv_t = jnp.transpose(state_conv[0], (1, 0, 2))
    x1, h2, idx, gate, rank, cnt, pool_p, conv_p, u_s, cv_s = _mixer(
        x_prompt, xs_pad, spool_t, sconv_t, meta_tokens, w)

    plan, disp, pads = _routing_tables(idx, rank, cnt)
    xs, inv = _dispatch(disp, pads, h2)
    warmup = _spare_row(jnp.arange(2 * MOE_BLK, dtype=jnp.int32)).reshape(2, MOE_BLK) * ROW_TILE
    dst_rows = jnp.concatenate([warmup, inv.reshape(N_BLOCKS, MOE_BLK)], axis=0)
    out4 = _moe_experts(plan, dst_rows, xs,
                        w_gu.reshape(N_EXPERTS, D_MODEL, 2 * D_FF), b_gu.reshape(N_EXPERTS, 1, 2 * D_FF),
                        w_down.reshape(N_EXPERTS, D_FF, D_MODEL), b_down.reshape(N_EXPERTS, 1, D_MODEL))

    y_p, y_s = _combine(x1, gate.T, out4, g_final.reshape(1, D_MODEL))

    new_pool_p = pool_p[:, POOL_HALO - POOL_BUF:, :][None]
    new_conv_p = conv_p[:, CONV_HALO - CONV_BUF:, :][None]
    new_pool_s = jnp.concatenate([state_pool[0][:, 1:, :], u_s[:, None, :]], axis=1)[None]
    new_conv_s = jnp.concatenate([state_conv[0][:, 1:, :], cv_s[:, None, :]], axis=1)[None]
    return (y_p.reshape(BATCH, SEQ, D_MODEL), y_s.reshape(DEC_BATCH, 1, D_MODEL),
            new_pool_p, new_conv_p, new_pool_s, new_conv_s)
```
